```python
import jax, jax.numpy as jnp
from jax import lax
import numpy as np

D_MODEL = 1024
BATCH = 4
SEQ = 8192
DEPTH = 1

NSA_HEADS = 8
NSA_KV_GROUPS = 2
NSA_HEAD_DIM = 64
CMP_LEN = 32
CMP_STRIDE = 16
CMP_HIDDEN = 256
SEL_BLOCK = 64
SEL_TOPK = 16
WINDOW = 512
MLA_HEADS = 8
MLA_NOPE = 64
MLA_ROPE = 32
MLA_V = 64
MLA_Q_RANK = 384
MLA_KV_RANK = 256
ROPE_THETA = 10000.0
D_FF = 4 * D_MODEL
Q_BLOCK = 128
EPS = 1e-6
FORCE_BONUS = 1000.0

NSA_HG = NSA_HEADS // NSA_KV_GROUPS
NSA_Q_DIM = NSA_HEADS * NSA_HEAD_DIM
NSA_KV_DIM = NSA_KV_GROUPS * NSA_HEAD_DIM
IN_SPLITS = (NSA_Q_DIM, NSA_KV_DIM, NSA_KV_DIM, NSA_KV_DIM, NSA_KV_DIM, NSA_KV_DIM, NSA_KV_DIM,
             3 * NSA_HEADS, MLA_Q_RANK, MLA_KV_RANK, MLA_ROPE, D_MODEL, D_MODEL)
IN_DIM = sum(IN_SPLITS)

kernel_name = 'hybrid_nsa_mla_gated_block'


def rmsnorm(x, g):
    xf = x.astype(jnp.float32)
    y = xf * lax.rsqrt(jnp.mean(xf * xf, axis=-1, keepdims=True) + EPS)
    return (y * g.astype(jnp.float32)).astype(x.dtype)


def modulate(h, shift, scale):
    return h * (1.0 + scale[:, None, :]) + shift[:, None, :]


def masked_softmax(s, mask):
    s = jnp.where(mask, s, -jnp.inf)
    m = jnp.max(s, axis=-1, keepdims=True)
    m = jnp.where(jnp.isfinite(m), m, 0.0)
    e = jnp.exp(s - m)
    return e / jnp.maximum(jnp.sum(e, axis=-1, keepdims=True), 1e-30)


def alibi_slopes(n):
    return 2.0 ** (-8.0 * jnp.arange(1, n + 1, dtype=jnp.float32) / n)


def rope(x, cos, sin):
    half = x.shape[-1] // 2
    x1, x2 = x[..., :half], x[..., half:]
    return jnp.concatenate([x1 * cos - x2 * sin, x2 * cos + x1 * sin], axis=-1).astype(x.dtype)


def compress(kv, pe, w1, w2):
    B, S, G, dh = kv.shape
    r = CMP_LEN // CMP_STRIDE
    n_chunks = S // CMP_STRIDE
    nc = n_chunks - r + 1
    chunks = kv.reshape(B, n_chunks, CMP_STRIDE, G, dh)
    blocks = jnp.concatenate([chunks[:, i:i + nc] for i in range(r)], axis=2)
    blocks = blocks + pe[None, None, :, None, :]
    flat = blocks.transpose(0, 3, 1, 2, 4).reshape(B, G, nc, CMP_LEN * dh)
    return jax.nn.silu(flat @ w1) @ w2


def nsa_attention(q, kc, vc, ks, vs, kw, vw, gates):
    B, G, Hg, S, dh = q.shape
    nc = kc.shape[2]
    ns = S // SEL_BLOCK
    n_sel = min(SEL_TOPK, ns)
    n_key_sel = n_sel * SEL_BLOCK
    scale = dh ** -0.5
    slopes = alibi_slopes(G * Hg).reshape(1, G, Hg, 1, 1)
    c_start = jnp.arange(nc) * CMP_STRIDE
    c_end = c_start + CMP_LEN - 1
    s_start = jnp.arange(ns) * SEL_BLOCK
    overlap = ((c_start[:, None] < s_start[None, :] + SEL_BLOCK)
               & (c_end[:, None] >= s_start[None, :])).astype(jnp.float32)
    ks_blocks = ks.reshape(B, G, ns, SEL_BLOCK, dh)
    vs_blocks = vs.reshape(B, G, ns, SEL_BLOCK, dh)
    kw_pad = jnp.pad(kw, ((0, 0), (0, 0), (WINDOW, 0), (0, 0)))
    vw_pad = jnp.pad(vw, ((0, 0), (0, 0), (WINDOW, 0), (0, 0)))
    blk = jnp.arange(ns)
    in_blk = jnp.arange(SEL_BLOCK)
    win_off = jnp.arange(Q_BLOCK + WINDOW) - WINDOW
    gather = jax.vmap(jax.vmap(lambda kb, ix: kb[ix]))

    def block(qb):
        t0 = qb * Q_BLOCK
        t = t0 + jnp.arange(Q_BLOCK)
        qq = lax.dynamic_slice_in_dim(q, t0, Q_BLOCK, axis=3)
        g = lax.dynamic_slice_in_dim(gates, t0, Q_BLOCK, axis=4)
        dist_c = (t[:, None] - c_end[None, :]).astype(jnp.float32)
        s = jnp.einsum('bghqd,bgkd->bghqk', qq, kc).astype(jnp.float32) * scale - slopes * dist_c
        p_cmp = masked_softmax(s, dist_c >= 0)
        o_cmp = jnp.einsum('bghqk,bgkd->bghqd', p_cmp.astype(vc.dtype), vc)
        p_slc = jnp.einsum('bghqk,ks->bgqs', p_cmp, overlap)
        cur = t // SEL_BLOCK
        valid = blk[None, :] <= cur[:, None]
        forced = (blk[None, :] == 0) | (blk[None, :] >= cur[:, None] - 1)
        score = jnp.where(valid, p_slc + FORCE_BONUS * forced.astype(jnp.float32), -jnp.inf)
        _, idx = lax.top_k(score, n_sel)
        spos = idx[..., None] * SEL_BLOCK + in_blk
        smask = (spos <= t[:, None, None]).reshape(B, G, Q_BLOCK, n_key_sel)
        dist_s = (t[:, None, None] - spos).reshape(B, G, Q_BLOCK, n_key_sel).astype(jnp.float32)
        k_sel = gather(ks_blocks, idx).reshape(B, G, Q_BLOCK, n_key_sel, dh)
        v_sel = gather(vs_blocks, idx).reshape(B, G, Q_BLOCK, n_key_sel, dh)
        s = jnp.einsum('bghqd,bgqkd->bghqk', qq, k_sel).astype(jnp.float32) * scale - slopes * dist_s[:, :, None]
        p = masked_softmax(s, smask[:, :, None])
        o_sel = jnp.einsum('bghqk,bgqkd->bghqd', p.astype(vs.dtype), v_sel)
        k_win = lax.dynamic_slice_in_dim(kw_pad, t0, Q_BLOCK + WINDOW, axis=2)
        v_win = lax.dynamic_slice_in_dim(vw_pad, t0, Q_BLOCK + WINDOW, axis=2)
        wpos = t0 + win_off
        dist_w = t[:, None] - wpos[None, :]
        wmask = (dist_w >= 0) & (dist_w < WINDOW) & (wpos[None, :] >= 0)
        s = jnp.einsum('bghqd,bgkd->bghqk', qq, k_win).astype(jnp.float32) * scale - slopes * dist_w.astype(jnp.float32)
        p = masked_softmax(s, wmask)
        o_win = jnp.einsum('bghqk,bgkd->bghqd', p.astype(vw.dtype), v_win)
        o = g[0] * o_cmp + g[1] * o_sel + g[2] * o_win
        return o.transpose(0, 3, 1, 2, 4).reshape(B, Q_BLOCK, G * Hg * dh)

    out = lax.map(block, jnp.arange(S // Q_BLOCK))
    return out.transpose(1, 0, 2, 3).reshape(B, S, G * Hg * dh)


def mla_attention(q, k, v):
    B, H, S, dqk = q.shape
    dv = v.shape[-1]
    scale = dqk ** -0.5
    kpos = jnp.arange(S)

    def block(qb):
        t0 = qb * Q_BLOCK
        t = t0 + jnp.arange(Q_BLOCK)
        qq = lax.dynamic_slice_in_dim(q, t0, Q_BLOCK, axis=2)
        s = jnp.einsum('bhqd,bhkd->bhqk', qq, k).astype(jnp.float32) * scale
        p = masked_softmax(s, kpos[None, :] <= t[:, None])
        o = jnp.einsum('bhqk,bhkd->bhqd', p.astype(v.dtype), v)
        return o.transpose(0, 2, 1, 3).reshape(B, Q_BLOCK, H * dv)

    out = lax.map(block, jnp.arange(S // Q_BLOCK))
    return out.transpose(1, 0, 2, 3).reshape(B, S, H * dv)


def setup_inputs(seed: int = 0) -> dict:
    key = jax.random.key(seed)
    ks = jax.random.split(key, 24)
    f32 = jnp.float32

    def nrm(k, shape, scale):
        return jax.random.normal(k, (DEPTH,) + shape, f32) * scale

    def gain(k, n):
        return 1.0 + 0.05 * jax.random.normal(k, (DEPTH, n), f32)

    dqk = MLA_NOPE + MLA_ROPE
    return {
        'x': jax.random.normal(ks[0], (BATCH, SEQ, D_MODEL), f32),
        'c': jax.random.normal(ks[1], (BATCH, D_MODEL), f32),
        'w_ada': nrm(ks[2], (D_MODEL, 6 * D_MODEL), 0.5 * D_MODEL ** -0.5),
        'b_ada': nrm(ks[3], (6 * D_MODEL,), 0.01),
        'g_mix': gain(ks[4], D_MODEL),
        'w_in': nrm(ks[5], (D_MODEL, IN_DIM), D_MODEL ** -0.5),
        'pe_ck': nrm(ks[6], (CMP_LEN, NSA_HEAD_DIM), 0.1),
        'w_ck1': nrm(ks[7], (CMP_LEN * NSA_HEAD_DIM, CMP_HIDDEN), (CMP_LEN * NSA_HEAD_DIM) ** -0.5),
        'w_ck2': nrm(ks[8], (CMP_HIDDEN, NSA_HEAD_DIM), CMP_HIDDEN ** -0.5),
        'pe_cv': nrm(ks[9], (CMP_LEN, NSA_HEAD_DIM), 0.1),
        'w_cv1': nrm(ks[10], (CMP_LEN * NSA_HEAD_DIM, CMP_HIDDEN), (CMP_LEN * NSA_HEAD_DIM) ** -0.5),
        'w_cv2': nrm(ks[11], (CMP_HIDDEN, NSA_HEAD_DIM), CMP_HIDDEN ** -0.5),
        'g_cq': gain(ks[12], MLA_Q_RANK),
        'w_uq': nrm(ks[13], (MLA_Q_RANK, MLA_HEADS * dqk), MLA_Q_RANK ** -0.5),
        'g_ckv': gain(ks[14], MLA_KV_RANK),
        'w_uk': nrm(ks[15], (MLA_KV_RANK, MLA_HEADS * MLA_NOPE), MLA_KV_RANK ** -0.5),
        'w_uv': nrm(ks[16], (MLA_KV_RANK, MLA_HEADS * MLA_V), MLA_KV_RANK ** -0.5),
        'w_o_nsa': nrm(ks[17], (NSA_Q_DIM, D_MODEL), NSA_Q_DIM ** -0.5),
        'w_o_mla': nrm(ks[18], (MLA_HEADS * MLA_V, D_MODEL), (MLA_HEADS * MLA_V) ** -0.5),
        'w_out': nrm(ks[19], (D_MODEL, D_MODEL), D_MODEL ** -0.5),
        'g_mlp': gain(ks[20], D_MODEL),
        'w_fc1': nrm(ks[21], (D_MODEL, D_FF), D_MODEL ** -0.5),
        'w_fc2': nrm(ks[22], (D_FF, D_MODEL), D_FF ** -0.5),
        'g_final': 1.0 + 0.05 * jax.random.normal(ks[23], (D_MODEL,), f32),
    }


def reference(x, c, w_ada, b_ada, g_mix, w_in, pe_ck, w_ck1, w_ck2, pe_cv, w_cv1, w_cv2,
              g_cq, w_uq, g_ckv, w_uk, w_uv, w_o_nsa, w_o_mla, w_out, g_mlp, w_fc1, w_fc2, g_final):
    B, S, _ = x.shape
    G, Hg, dh = NSA_KV_GROUPS, NSA_HG, NSA_HEAD_DIM
    split_points = np.cumsum(IN_SPLITS)[:-1].tolist()
    half = MLA_ROPE // 2
    pos = jnp.arange(S, dtype=jnp.float32)
    inv_freq = ROPE_THETA ** (-jnp.arange(half, dtype=jnp.float32) / half)
    ang = pos[:, None] * inv_freq[None, :]
    cos, sin = jnp.cos(ang), jnp.sin(ang)

    def kv_groups(t):
        return t.reshape(B, S, G, dh)

    for layer in range(DEPTH):
        mod = c @ w_ada[layer] + b_ada[layer]
        sh_a, sc_a, gt_a, sh_m, sc_m, gt_m = jnp.split(mod, 6, axis=-1)

        h = modulate(rmsnorm(x, g_mix[layer]), sh_a, sc_a)
        z = h @ w_in[layer]
        (zq, zkc, zvc, zks, zvs, zkw, zvw, zg, zqd, zkvd, zkr, zga, zgb) = jnp.split(z, split_points, axis=-1)

        q_nsa = zq.reshape(B, S, G, Hg, dh).transpose(0, 2, 3, 1, 4)
        kc = compress(kv_groups(zkc), pe_ck[layer], w_ck1[layer], w_ck2[layer])
        vc = compress(kv_groups(zvc), pe_cv[layer], w_cv1[layer], w_cv2[layer])
        k_s = kv_groups(zks).transpose(0, 2, 1, 3)
        v_s = kv_groups(zvs).transpose(0, 2, 1, 3)
        k_w = kv_groups(zkw).transpose(0, 2, 1, 3)
        v_w = kv_groups(zvw).transpose(0, 2, 1, 3)
        nsa_gates = jax.nn.sigmoid(zg).reshape(B, S, 3, G, Hg).transpose(2, 0, 3, 4, 1)[..., None]
        o_nsa = nsa_attention(q_nsa, kc, vc, k_s, v_s, k_w, v_w, nsa_gates)

        cq = rmsnorm(zqd, g_cq[layer])
        qm = (cq @ w_uq[layer]).reshape(B, S, MLA_HEADS, MLA_NOPE + MLA_ROPE)
        q_rot = rope(qm[..., MLA_NOPE:], cos[:, None, :], sin[:, None, :])
        q_mla = jnp.concatenate([qm[..., :MLA_NOPE], q_rot], axis=-1)
        ckv = rmsnorm(zkvd, g_ckv[layer])
        k_nope = (ckv @ w_uk[layer]).reshape(B, S, MLA_HEADS, MLA_NOPE)
        v_mla = (ckv @ w_uv[layer]).reshape(B, S, MLA_HEADS, MLA_V)
        k_rot = rope(zkr, cos, sin)
        k_mla = jnp.concatenate(
            [k_nope, jnp.broadcast_to(k_rot[:, :, None, :], (B, S, MLA_HEADS, MLA_ROPE))], axis=-1)
        o_mla = mla_attention(q_mla.transpose(0, 2, 1, 3), k_mla.transpose(0, 2, 1, 3),
                              v_mla.transpose(0, 2, 1, 3))

        y = jax.nn.sigmoid(zga) * (o_nsa @ w_o_nsa[layer]) + jax.nn.sigmoid(zgb) * (o_mla @ w_o_mla[layer])
        x = x + gt_a[:, None, :] * (y @ w_out[layer])

        h = modulate(rmsnorm(x, g_mlp[layer]), sh_m, sc_m)
        x = x + gt_m[:, None, :] * (jnp.square(jax.nn.relu(h @ w_fc1[layer])) @ w_fc2[layer])

    return rmsnorm(x, g_final)
```

```python
import functools

import jax
import jax.numpy as jnp
from jax import lax
from jax.experimental import pallas as pl
from jax.experimental.pallas import tpu as pltpu

F32 = jnp.float32
BF16 = jnp.bfloat16

D_MODEL = 1024
NSA_GROUPS = 2
NSA_HG = 4
NSA_DH = 64
CMP_LEN = 32
CMP_STRIDE = 16
CMP_HIDDEN = 256
SEL_BLOCK = 64
SEL_TOPK = 16
WINDOW = 512
MLA_HEADS = 8
MLA_NOPE = 64
MLA_ROPE = 32
MLA_V = 64
MLA_Q_RANK = 384
MLA_KV_RANK = 256
ROPE_THETA = 10000.0
D_FF = 4 * D_MODEL
EPS = 1e-6
FORCE_BONUS = 1000.0

LANE = 128
HALF = LANE // 2
NEG = -1e30
VMEM_LIMIT = 52 * 1024 * 1024

OFF_Q = 0
OFF_KC = 512
OFF_VC = 640
OFF_KS = 768
OFF_VS = 1024
OFF_KW = 1280
OFF_VW = 1536
OFF_G = 1792
OFF_QD = 2048
OFF_KVD = 2432
OFF_KRA = 2688
OFF_KRB = 2816
OFF_GA = 2944
OFF_GB = 3968
IN_COLS = 4992


def _dot(a, b):
    return jnp.dot(a, b, preferred_element_type=F32)


def _dot_nt(a, b):
    return lax.dot_general(a, b, (((1,), (1,)), ((), ())), preferred_element_type=F32)


def _rms(x, g):
    return x * lax.rsqrt(jnp.mean(x * x, axis=-1, keepdims=True) + EPS) * g


def _sigmoid(x):
    return 1.0 / (1.0 + jnp.exp(-x))


def _params(sem):
    return pltpu.CompilerParams(dimension_semantics=sem, vmem_limit_bytes=VMEM_LIMIT)


def _ada_body(c_ref, w_ref, b_ref, o_ref):
    o_ref[...] = jnp.dot(c_ref[...], w_ref[...], preferred_element_type=F32,
                         precision=lax.Precision.HIGHEST) + b_ref[...]


def _ada(c, w, b):
    nb, d = c.shape
    n = w.shape[1]
    rows = 8
    cp = jnp.zeros((rows, d), F32).at[:nb].set(c)
    tn = 1536
    out = pl.pallas_call(
        _ada_body,
        grid=(n // tn,),
        in_specs=[pl.BlockSpec((rows, d), lambda j: (0, 0)),
                  pl.BlockSpec((d, tn), lambda j: (0, j)),
                  pl.BlockSpec((1, tn), lambda j: (0, j))],
        out_specs=pl.BlockSpec((rows, tn), lambda j: (0, j)),
        out_shape=jax.ShapeDtypeStruct((rows, n), F32),
        compiler_params=_params(("arbitrary",)),
        name="ada",
    )(cp, w, b.reshape(1, n))
    return out[:nb].reshape(nb, 6, d)


def _inproj_body(x_ref, mod_ref, gmix_ref, w_ref, gcq_ref, wq2_ref, gckv_ref, wkv_ref,
                 cs_ref, sn_ref,
                 q_ref, kc_ref, vc_ref, ks_ref, vs_ref, kw_ref, vw_ref, g_ref,
                 qm_ref, km_ref, vm_ref, ga_ref, gb_ref):
    x = x_ref[0]
    h = _rms(x, gmix_ref[...]) * (1.0 + mod_ref[0, 1:2, :]) + mod_ref[0, 0:1, :]
    hb = h.astype(BF16)

    def proj(lo, n):
        return _dot(hb, w_ref[:, lo:lo + n])

    q_ref[0] = (proj(OFF_Q, 512) * (NSA_DH ** -0.5)).astype(BF16)
    kc_ref[0] = proj(OFF_KC, 128).astype(BF16)
    vc_ref[0] = proj(OFF_VC, 128).astype(BF16)
    ks_ref[0] = proj(OFF_KS, 256).astype(BF16)
    vs_ref[0] = proj(OFF_VS, 256).astype(BF16)
    kw_ref[0] = proj(OFF_KW, 256).astype(BF16)
    vw_ref[0] = proj(OFF_VW, 256).astype(BF16)
    g_ref[0] = _sigmoid(proj(OFF_G, 256))
    ga_ref[0] = _sigmoid(proj(OFF_GA, 1024)).astype(BF16)
    gb_ref[0] = _sigmoid(proj(OFF_GB, 1024)).astype(BF16)

    cs = cs_ref[...]
    sn = sn_ref[...]
    nh = MLA_HEADS * LANE
    cq = _rms(proj(OFF_QD, MLA_Q_RANK), gcq_ref[...]).astype(BF16)
    q2 = _dot(cq, wq2_ref[...])
    scale = (MLA_NOPE + MLA_ROPE) ** -0.5
    for hh in range(MLA_HEADS):
        a = q2[:, LANE * hh:LANE * (hh + 1)]
        b = q2[:, nh + LANE * hh:nh + LANE * (hh + 1)]
        qm_ref[0, :, LANE * hh:LANE * (hh + 1)] = ((a * cs + b * sn) * scale).astype(BF16)
    ckv = _rms(proj(OFF_KVD, MLA_KV_RANK), gckv_ref[...]).astype(BF16)
    kv = _dot(ckv, wkv_ref[...])
    kr = proj(OFF_KRA, LANE) * cs + proj(OFF_KRB, LANE) * sn
    for hh in range(MLA_HEADS):
        km_ref[0, :, LANE * hh:LANE * (hh + 1)] = (kv[:, LANE * hh:LANE * (hh + 1)] + kr).astype(BF16)
    vm_ref[0] = kv[:, nh:].astype(BF16)


def _inproj(x, mod, g_mix, w_all, g_cq, wq2, g_ckv, wkv, cs, sn, tm):
    B, S, D = x.shape
    row = lambda n: pl.BlockSpec((1, tm, n), lambda b, i: (b, i, 0))
    full = lambda a: pl.BlockSpec(a.shape, lambda b, i: (0,) * a.ndim)
    outs = [(512, BF16), (128, BF16), (128, BF16), (256, BF16), (256, BF16), (256, BF16), (256, BF16),
            (256, F32), (1024, BF16), (1024, BF16), (512, BF16), (1024, BF16), (1024, BF16)]
    return pl.pallas_call(
        _inproj_body,
        grid=(B, S // tm),
        in_specs=[row(D),
                  pl.BlockSpec((1, 6, D), lambda b, i: (b, 0, 0)),
                  full(g_mix), full(w_all), full(g_cq), full(wq2), full(g_ckv), full(wkv),
                  pl.BlockSpec((tm, LANE), lambda b, i: (i, 0)),
                  pl.BlockSpec((tm, LANE), lambda b, i: (i, 0))],
        out_specs=[row(n) for n, _ in outs],
        out_shape=[jax.ShapeDtypeStruct((B, S, n), dt) for n, dt in outs],
        compiler_params=_params(("parallel", "parallel")),
        name="inproj",
    )(x, mod, g_mix, w_all, g_cq, wq2, g_ckv, wkv, cs, sn)


def _compress_one(x, w1big, pe, w1f, w2d, o_ref):
    hcat = _dot(x, w1big)
    const = jnp.dot(pe, w1f, preferred_element_type=F32, precision=lax.Precision.HIGHEST)[0:1]
    n = x.shape[0]
    rowid = lax.broadcasted_iota(jnp.int32, (n, LANE), 0)
    hid_w = CMP_HIDDEN
    for g in range(NSA_GROUPS):
        a = hcat[:, 2 * hid_w * g:2 * hid_w * g + hid_w]
        b = hcat[:, 2 * hid_w * g + hid_w:2 * hid_w * (g + 1)]
        hid = a + pltpu.roll(b, n - 1, 0) + const
        act = hid * _sigmoid(hid)
        out = _dot(act.astype(BF16), w2d)
        o_ref[0, g] = jnp.where(rowid < n - 1, out, 0.0).astype(BF16)


def _compress_body(xk_ref, xv_ref, w1k_ref, w1v_ref, pek_ref, pev_ref, w1kf_ref, w1vf_ref,
                   w2k_ref, w2v_ref, kc_ref, vc_ref):
    _compress_one(xk_ref[0], w1k_ref[...], pek_ref[...], w1kf_ref[...], w2k_ref[...], kc_ref)
    _compress_one(xv_ref[0], w1v_ref[...], pev_ref[...], w1vf_ref[...], w2v_ref[...], vc_ref)


def _compress(xk, xv, w1k, w1v, pek, pev, w1kf, w1vf, w2k, w2v):
    B, n, width = xk.shape
    full = lambda a: pl.BlockSpec(a.shape, lambda b: (0,) * a.ndim)
    xs = pl.BlockSpec((1, n, width), lambda b: (b, 0, 0))
    os_ = pl.BlockSpec((1, NSA_GROUPS, n, LANE), lambda b: (b, 0, 0, 0))
    osh = jax.ShapeDtypeStruct((B, NSA_GROUPS, n, LANE), BF16)
    return pl.pallas_call(
        _compress_body,
        grid=(B,),
        in_specs=[xs, xs, full(w1k), full(w1v), full(pek), full(pev), full(w1kf), full(w1vf),
                  full(w2k), full(w2v)],
        out_specs=[os_, os_],
        out_shape=[osh, osh],
        compiler_params=_params(("parallel",)),
        name="compress",
    )(xk, xv, w1k, w1v, pek, pev, w1kf, w1vf, w2k, w2v)


def _stack_heads(q_ref, tq):
    upper = lax.broadcasted_iota(jnp.int32, (tq, LANE), 1) >= HALF
    parts = []
    for hh in range(NSA_HG):
        slab = q_ref[0, :, LANE * (hh // 2):LANE * (hh // 2 + 1)]
        keep = upper if hh % 2 else jnp.logical_not(upper)
        parts.append(jnp.where(keep, slab, jnp.zeros_like(slab)))
    return jnp.concatenate(parts, axis=0)


def _slope(g, hh):
    return jnp.where(g == 0, 1.0, 2.0 ** -NSA_HG).astype(F32) * (2.0 ** -(hh + 1))


def _softmax_rows(s, mask):
    sm = jnp.where(mask, s, NEG)
    m = jnp.max(sm, axis=-1, keepdims=True)
    e = jnp.where(mask, jnp.exp(sm - m), 0.0)
    return e / jnp.maximum(jnp.sum(e, axis=-1, keepdims=True), 1e-30)


def _pair(lo, hi):
    upper = lax.broadcasted_iota(jnp.int32, lo.shape, 1) >= HALF
    return jnp.where(upper, hi, lo)


def _nsa_cw_body(q_ref, kc_ref, vc_ref, kw_ref, vw_ref, g_ref, ovt_ref, ocw_ref, sel_ref, *, tq, seq):
    g = pl.program_id(1)
    t0 = pl.program_id(2) * tq
    qst = _stack_heads(q_ref, tq)
    gates = g_ref[0]

    kc = kc_ref[0, 0]
    vc = vc_ref[0, 0]
    ncp = kc.shape[0]
    s_all = _dot_nt(qst, kc)
    t_c = t0 + lax.broadcasted_iota(jnp.int32, (tq, ncp), 0)
    c_end = lax.broadcasted_iota(jnp.int32, (tq, ncp), 1) * CMP_STRIDE + (CMP_LEN - 1)
    dist_c = (t_c - c_end).astype(F32)
    mask_c = dist_c >= 0.0
    psum = jnp.zeros((tq, ncp), F32)
    o_cmp = []
    for hh in range(NSA_HG):
        p = _softmax_rows(s_all[hh * tq:(hh + 1) * tq] - _slope(g, hh) * dist_c, mask_c)
        psum = psum + p
        o_cmp.append(_dot(p.astype(BF16), vc))

    ovt = ovt_ref[...]
    p1 = psum.astype(BF16)
    r1 = psum - p1.astype(F32)
    p2 = r1.astype(BF16)
    p3 = (r1 - p2.astype(F32)).astype(BF16)
    p_slc = _dot_nt(ovt, p1) + _dot_nt(ovt, p2) + _dot_nt(ovt, p3)
    nsp = ovt.shape[0]
    blk = lax.broadcasted_iota(jnp.int32, (nsp, tq), 0)
    cur = (t0 + lax.broadcasted_iota(jnp.int32, (nsp, tq), 1)) // SEL_BLOCK
    valid = blk <= cur
    forced = (blk == 0) | (blk >= cur - 1)
    score = jnp.where(valid, p_slc + jnp.where(forced, FORCE_BONUS, 0.0), -1.0)
    picked = jnp.zeros((nsp, tq), F32)
    for _ in range(min(SEL_TOPK, seq // SEL_BLOCK)):
        m = jnp.max(score, axis=0, keepdims=True)
        first = jnp.min(jnp.where(score == m, blk, nsp), axis=0, keepdims=True)
        hit = blk == first
        picked = jnp.where(hit, 1.0, picked)
        score = jnp.where(hit, -1.0, score)
    sel_t = jnp.where(valid, picked, 0.0)
    sel_ref[0, 0] = sel_t.T.astype(BF16)

    wk = WINDOW + tq
    k_start = pl.multiple_of(jnp.clip(t0 - WINDOW, 0, seq - wk), LANE)
    kw = kw_ref[0, pl.ds(k_start, wk), :]
    vw = vw_ref[0, pl.ds(k_start, wk), :]
    s_all = _dot_nt(qst, kw)
    t_w = t0 + lax.broadcasted_iota(jnp.int32, (tq, wk), 0)
    dist_i = t_w - (k_start + lax.broadcasted_iota(jnp.int32, (tq, wk), 1))
    mask_w = (dist_i >= 0) & (dist_i < WINDOW)
    dist_w = dist_i.astype(F32)
    outs = []
    for hh in range(NSA_HG):
        p = _softmax_rows(s_all[hh * tq:(hh + 1) * tq] - _slope(g, hh) * dist_w, mask_w)
        o_win = _dot(p.astype(BF16), vw)
        outs.append(gates[:, hh:hh + 1] * o_cmp[hh] + gates[:, 2 * NSA_HG + hh:2 * NSA_HG + hh + 1] * o_win)
    ocw_ref[0, :, 0:LANE] = _pair(outs[0], outs[1])
    ocw_ref[0, :, LANE:2 * LANE] = _pair(outs[2], outs[3])


def _nsa_cw(q, kcd, vcd, kwd, vwd, gates, ovt, tq):
    B, S, _ = q.shape
    ncp = kcd.shape[2]
    nsp = ovt.shape[0]
    qs = pl.BlockSpec((1, tq, 2 * LANE), lambda b, g, i: (b, i, g))
    cs_ = pl.BlockSpec((1, 1, ncp, LANE), lambda b, g, i: (b, g, 0, 0))
    ws = pl.BlockSpec((1, S, LANE), lambda b, g, i: (b, 0, g))
    gs = pl.BlockSpec((1, tq, LANE), lambda b, g, i: (b, i, g))
    return pl.pallas_call(
        functools.partial(_nsa_cw_body, tq=tq, seq=S),
        grid=(B, NSA_GROUPS, S // tq),
        in_specs=[qs, cs_, cs_, ws, ws, gs, pl.BlockSpec(ovt.shape, lambda b, g, i: (0, 0))],
        out_specs=[qs, pl.BlockSpec((1, 1, tq, nsp), lambda b, g, i: (b, g, i, 0))],
        out_shape=[jax.ShapeDtypeStruct((B, S, 4 * LANE), F32),
                   jax.ShapeDtypeStruct((B, NSA_GROUPS, S, nsp), BF16)],
        compiler_params=_params(("parallel", "parallel", "arbitrary")),
        name="nsa_cw",
    )(q, kcd, vcd, kwd, vwd, gates, ovt)


def _nsa_sel_body(q_ref, ks_ref, vs_ref, sel_ref, e_ref, g_ref, ocw_ref, o_ref,
                  m_sc, l_sc, acc_sc, *, tq, tk):
    g = pl.program_id(1)
    t0 = pl.program_id(2) * tq
    qst = _stack_heads(q_ref, tq)
    sel = sel_ref[0, 0]
    t_q = t0 + lax.broadcasted_iota(jnp.int32, (tq, tk), 0)
    col = lax.broadcasted_iota(jnp.int32, (tq, tk), 1)
    m_sc[...] = jnp.full(m_sc.shape, NEG, F32)
    l_sc[...] = jnp.zeros(l_sc.shape, F32)
    acc_sc[...] = jnp.zeros(acc_sc.shape, F32)
    j_diag = t0 // tk

    def step(it, carry):
        k0 = pl.multiple_of((j_diag - it) * tk, tk)
        k = ks_ref[0, pl.ds(k0, tk), :]
        v = vs_ref[0, pl.ds(k0, tk), :]
        s_all = _dot_nt(qst, k)
        chosen = _dot(sel, e_ref[:, pl.ds(k0, tk)])
        dist_i = t_q - (k0 + col)
        ok = (chosen > 0.5) & (dist_i >= 0)
        dist = dist_i.astype(F32)
        for hh in range(NSA_HG):
            rows = slice(hh * tq, (hh + 1) * tq)
            s = jnp.where(ok, s_all[rows] - _slope(g, hh) * dist, NEG)
            m_prev = m_sc[rows]
            m_new = jnp.maximum(m_prev, jnp.max(s, axis=-1, keepdims=True))
            alpha = jnp.exp(m_prev - m_new)
            p = jnp.exp(s - m_new)
            l_sc[rows] = alpha * l_sc[rows] + jnp.sum(p, axis=-1, keepdims=True)
            acc_sc[rows] = alpha * acc_sc[rows] + _dot(p.astype(BF16), v)
            m_sc[rows] = m_new
        return carry

    lax.fori_loop(0, j_diag + 1, step, 0)

    gates = g_ref[0]
    outs = []
    for hh in range(NSA_HG):
        rows = slice(hh * tq, (hh + 1) * tq)
        o_sel = acc_sc[rows] / jnp.maximum(l_sc[rows], 1e-30)
        outs.append(gates[:, NSA_HG + hh:NSA_HG + hh + 1] * o_sel)
    o_ref[0, :, 0:LANE] = (ocw_ref[0, :, 0:LANE] + _pair(outs[0], outs[1])).astype(BF16)
    o_ref[0, :, LANE:2 * LANE] = (ocw_ref[0, :, LANE:2 * LANE] + _pair(outs[2], outs[3])).astype(BF16)


def _nsa_sel(q, ksd, vsd, sel, emat, gates, ocw, tq, tk):
    B, S, _ = q.shape
    nsp = sel.shape[3]
    qs = pl.BlockSpec((1, tq, 2 * LANE), lambda b, g, i: (b, i, g))
    ws = pl.BlockSpec((1, S, LANE), lambda b, g, i: (b, 0, g))
    gs = pl.BlockSpec((1, tq, LANE), lambda b, g, i: (b, i, g))
    return pl.pallas_call(
        functools.partial(_nsa_sel_body, tq=tq, tk=tk),
        grid=(B, NSA_GROUPS, S // tq),
        in_specs=[qs, ws, ws, pl.BlockSpec((1, 1, tq, nsp), lambda b, g, i: (b, g, i, 0)),
                  pl.BlockSpec(emat.shape, lambda b, g, i: (0, 0)), gs, qs],
        out_specs=qs,
        out_shape=jax.ShapeDtypeStruct((B, S, 4 * LANE), BF16),
        scratch_shapes=[pltpu.VMEM((NSA_HG * tq, 1), F32), pltpu.VMEM((NSA_HG * tq, 1), F32),
                        pltpu.VMEM((NSA_HG * tq, LANE), F32)],
        compiler_params=_params(("parallel", "parallel", "arbitrary")),
        name="nsa_sel",
    )(q, ksd, vsd, sel, emat, gates, ocw)


def _mla_body(q_ref, k_ref, v_ref, o_ref, m_sc, l_sc, acc_sc, *, tq, tk):
    t0 = pl.program_id(2) * tq
    j_diag = t0 // tk
    t_q = t0 + lax.broadcasted_iota(jnp.int32, (tq, tk), 0)
    col = lax.broadcasted_iota(jnp.int32, (tq, tk), 1)
    outs = []
    for hh in range(2):
        lanes = slice(LANE * hh, LANE * (hh + 1))
        q = q_ref[0, :, lanes]
        m_sc[...] = jnp.full(m_sc.shape, NEG, F32)
        l_sc[...] = jnp.zeros(l_sc.shape, F32)
        acc_sc[...] = jnp.zeros(acc_sc.shape, F32)

        def update(k0, masked, q=q, lanes=lanes):
            s = _dot_nt(q, k_ref[0, pl.ds(k0, tk), lanes])
            if masked:
                s = jnp.where(k0 + col <= t_q, s, NEG)
            m_prev = m_sc[...]
            m_new = jnp.maximum(m_prev, jnp.max(s, axis=-1, keepdims=True))
            alpha = jnp.exp(m_prev - m_new)
            p = jnp.exp(s - m_new)
            l_sc[...] = alpha * l_sc[...] + jnp.sum(p, axis=-1, keepdims=True)
            acc_sc[...] = alpha * acc_sc[...] + _dot(p.astype(BF16), v_ref[0, pl.ds(k0, tk), :])
            m_sc[...] = m_new

        def step(j, carry, update=update):
            update(pl.multiple_of(j * tk, tk), False)
            return carry

        lax.fori_loop(0, j_diag, step, 0)
        update(pl.multiple_of(j_diag * tk, tk), True)
        outs.append(acc_sc[...] / jnp.maximum(l_sc[...], 1e-30))
    o_ref[0] = _pair(outs[0], outs[1]).astype(BF16)


def _mla(qm, km, vm, tq, tk):
    B, S, _ = qm.shape
    return pl.pallas_call(
        functools.partial(_mla_body, tq=tq, tk=tk),
        grid=(B, MLA_HEADS // 2, S // tq),
        in_specs=[pl.BlockSpec((1, tq, 2 * LANE), lambda b, p, i: (b, i, p)),
                  pl.BlockSpec((1, S, 2 * LANE), lambda b, p, i: (b, 0, p)),
                  pl.BlockSpec((1, S, LANE), lambda b, p, i: (b, 0, p))],
        out_specs=pl.BlockSpec((1, tq, LANE), lambda b, p, i: (b, i, p)),
        out_shape=jax.ShapeDtypeStruct((B, S, MLA_HEADS * MLA_V), BF16),
        scratch_shapes=[pltpu.VMEM((tq, 1), F32), pltpu.VMEM((tq, 1), F32), pltpu.VMEM((tq, LANE), F32)],
        compiler_params=_params(("parallel", "parallel", "arbitrary")),
        name="mla",
    )(qm, km, vm)


def _merge_body(on_ref, om_ref, ga_ref, gb_ref, x_ref, mod_ref, won_ref, wom_ref, wout_ref, o_ref):
    y = (ga_ref[0].astype(F32) * _dot(on_ref[0], won_ref[...])
         + gb_ref[0].astype(F32) * _dot(om_ref[0], wom_ref[...]))
    o_ref[0] = x_ref[0] + mod_ref[0, 2:3, :] * _dot(y.astype(BF16), wout_ref[...])


def _merge(o_nsa, o_mla, ga, gb, x, mod, w_on, w_om, w_out, tm):
    B, S, D = x.shape
    row = lambda n: pl.BlockSpec((1, tm, n), lambda b, i: (b, i, 0))
    full = lambda a: pl.BlockSpec(a.shape, lambda b, i: (0,) * a.ndim)
    return pl.pallas_call(
        _merge_body,
        grid=(B, S // tm),
        in_specs=[row(o_nsa.shape[2]), row(o_mla.shape[2]), row(D), row(D), row(D),
                  pl.BlockSpec((1, 6, D), lambda b, i: (b, 0, 0)), full(w_on), full(w_om), full(w_out)],
        out_specs=row(D),
        out_shape=jax.ShapeDtypeStruct((B, S, D), F32),
        compiler_params=_params(("parallel", "parallel")),
        name="merge",
    )(o_nsa, o_mla, ga, gb, x, mod, w_on, w_om, w_out)


def _mlp_body(x_ref, mod_ref, gmlp_ref, w1_ref, w2_ref, gfin_ref, o_ref, *, fc, final):
    x1 = x_ref[0]
    h = _rms(x1, gmlp_ref[...]) * (1.0 + mod_ref[0, 4:5, :]) + mod_ref[0, 3:4, :]
    hb = h.astype(BF16)
    acc = jnp.zeros(x1.shape, F32)
    for c in range(w1_ref.shape[1] // fc):
        a = jnp.maximum(_dot(hb, w1_ref[:, c * fc:(c + 1) * fc]), 0.0)
        acc = acc + _dot((a * a).astype(BF16), w2_ref[c * fc:(c + 1) * fc, :])
    x2 = x1 + mod_ref[0, 5:6, :] * acc
    o_ref[0] = _rms(x2, gfin_ref[...]) if final else x2


def _mlp(x1, mod, g_mlp, w1, w2, g_fin, tm, fc, final):
    B, S, D = x1.shape
    row = pl.BlockSpec((1, tm, D), lambda b, i: (b, i, 0))
    full = lambda a: pl.BlockSpec(a.shape, lambda b, i: (0,) * a.ndim)
    return pl.pallas_call(
        functools.partial(_mlp_body, fc=fc, final=final),
        grid=(B, S // tm),
        in_specs=[row, pl.BlockSpec((1, 6, D), lambda b, i: (b, 0, 0)), full(g_mlp), full(w1), full(w2),
                  full(g_fin)],
        out_specs=row,
        out_shape=jax.ShapeDtypeStruct((B, S, D), F32),
        compiler_params=_params(("parallel", "parallel")),
        name="mlp",
    )(x1, mod, g_mlp, w1, w2, g_fin)


def _dup_groups(w):
    g0, g1 = w[:, :NSA_DH], w[:, NSA_DH:]
    return jnp.concatenate([g0, g0, g1, g1], axis=1)


def _inproj_weight(w_in):
    d = w_in.shape[0]
    o = 0
    parts = {}
    for name, n in (("q", 512), ("kc", 128), ("vc", 128), ("ks", 128), ("vs", 128), ("kw", 128),
                    ("vw", 128), ("g", 24), ("qd", MLA_Q_RANK), ("kvd", MLA_KV_RANK), ("kr", MLA_ROPE),
                    ("ga", D_MODEL), ("gb", D_MODEL)):
        parts[name] = w_in[:, o:o + n]
        o += n
    wg = parts["g"].reshape(d, 3, NSA_GROUPS, NSA_HG).transpose(0, 2, 1, 3).reshape(d, NSA_GROUPS, 3 * NSA_HG)
    wg = jnp.pad(wg, ((0, 0), (0, 0), (0, LANE - 3 * NSA_HG))).reshape(d, NSA_GROUPS * LANE)
    half = MLA_ROPE // 2
    x1, x2 = parts["kr"][:, :half], parts["kr"][:, half:]
    z_lo = jnp.zeros((d, MLA_NOPE), F32)
    z_hi = jnp.zeros((d, LANE - MLA_NOPE - MLA_ROPE), F32)
    kra = jnp.concatenate([z_lo, x1, x2, z_hi], axis=1)
    krb = jnp.concatenate([z_lo, -x2, x1, z_hi], axis=1)
    w_all = jnp.concatenate(
        [parts["q"], parts["kc"], parts["vc"], _dup_groups(parts["ks"]), _dup_groups(parts["vs"]),
         _dup_groups(parts["kw"]), _dup_groups(parts["vw"]), wg, parts["qd"], parts["kvd"], kra, krb,
         parts["ga"], parts["gb"]], axis=1)
    assert w_all.shape[1] == IN_COLS
    return w_all.astype(BF16)


def _mla_q_weight(w_uq):
    r = w_uq.shape[0]
    w = w_uq.reshape(r, MLA_HEADS, MLA_NOPE + MLA_ROPE)
    half = MLA_ROPE // 2
    nope, x1, x2 = w[..., :MLA_NOPE], w[..., MLA_NOPE:MLA_NOPE + half], w[..., MLA_NOPE + half:]
    z_hi = jnp.zeros((r, MLA_HEADS, LANE - MLA_NOPE - MLA_ROPE), F32)
    a = jnp.concatenate([nope, x1, x2, z_hi], axis=-1).reshape(r, MLA_HEADS * LANE)
    b = jnp.concatenate([jnp.zeros_like(nope), -x2, x1, z_hi], axis=-1).reshape(r, MLA_HEADS * LANE)
    return jnp.concatenate([a, b], axis=1).astype(BF16)


def _mla_kv_weight(w_uk, w_uv):
    r = w_uk.shape[0]
    k = w_uk.reshape(r, MLA_HEADS, MLA_NOPE)
    k = jnp.pad(k, ((0, 0), (0, 0), (0, LANE - MLA_NOPE))).reshape(r, MLA_HEADS * LANE)
    return jnp.concatenate([k, w_uv], axis=1).astype(BF16)


def _rope_tables(seq):
    half = MLA_ROPE // 2
    pos = jnp.arange(seq, dtype=F32)
    inv_freq = ROPE_THETA ** (-jnp.arange(half, dtype=F32) / half)
    ang = pos[:, None] * inv_freq[None, :]
    cos, sin = jnp.cos(ang), jnp.sin(ang)
    ones = jnp.ones((seq, MLA_NOPE), F32)
    z_lo = jnp.zeros((seq, MLA_NOPE), F32)
    z_hi = jnp.zeros((seq, LANE - MLA_NOPE - MLA_ROPE), F32)
    return (jnp.concatenate([ones, cos, cos, z_hi], axis=1),
            jnp.concatenate([z_lo, sin, sin, z_hi], axis=1))


def _compress_weights(pe, w1, w2):
    hid = w1.shape[1]
    wa = w1[:CMP_STRIDE * NSA_DH].reshape(CMP_STRIDE, NSA_DH, hid)
    wb = w1[CMP_STRIDE * NSA_DH:].reshape(CMP_STRIDE, NSA_DH, hid)
    z = jnp.zeros_like(wa)
    cols = []
    for g in range(NSA_GROUPS):
        for w in (wa, wb):
            slots = [z, z]
            slots[g] = w
            cols.append(jnp.stack(slots, axis=1).reshape(CMP_STRIDE * NSA_GROUPS * NSA_DH, hid))
    w1big = jnp.concatenate(cols, axis=1).astype(BF16)
    pe8 = jnp.zeros((8, CMP_LEN * NSA_DH), F32).at[0].set(pe.reshape(-1))
    w2d = jnp.concatenate([w2, w2], axis=1).astype(BF16)
    return w1big, pe8, w1, w2d


def _selection_tables(seq):
    ns = seq // SEL_BLOCK
    nsp = -(-ns // LANE) * LANE
    ncp = seq // CMP_STRIDE
    s_id = jnp.arange(nsp)[:, None]
    c_start = jnp.arange(ncp)[None, :] * CMP_STRIDE
    ovt = ((c_start < s_id * SEL_BLOCK + SEL_BLOCK) & (c_start + CMP_LEN - 1 >= s_id * SEL_BLOCK)
           & (jnp.arange(ncp)[None, :] < ncp - 1)).astype(BF16)
    emat = (jnp.arange(seq)[None, :] // SEL_BLOCK == s_id).astype(BF16)
    return ovt, emat


def kernel(x, c, w_ada, b_ada, g_mix, w_in, pe_ck, w_ck1, w_ck2, pe_cv, w_cv1, w_cv2, g_cq, w_uq, g_ckv,
           w_uk, w_uv, w_o_nsa, w_o_mla, w_out, g_mlp, w_fc1, w_fc2, g_final):
    B, S, D = x.shape
    depth = w_ada.shape[0]
    tm_in = min(256, S)
    tm_out = min(512, S)
    tq_nsa = 128
    tk_sel = min(512, S)
    tq_mla = min(256, S)
    tk_mla = min(512, S)
    cs, sn = _rope_tables(S)
    ovt, emat = _selection_tables(S)
    n_chunks = S // CMP_STRIDE
    for layer in range(depth):
        mod = _ada(c, w_ada[layer], b_ada[layer])
        (q, kc_in, vc_in, ksd, vsd, kwd, vwd, gates, qm, km, vm, ga, gb) = _inproj(
            x, mod, g_mix[layer][None], _inproj_weight(w_in[layer]), g_cq[layer][None],
            _mla_q_weight(w_uq[layer]), g_ckv[layer][None], _mla_kv_weight(w_uk[layer], w_uv[layer]),
            cs, sn, tm_in)
        w1k, pek, w1kf, w2k = _compress_weights(pe_ck[layer], w_ck1[layer], w_ck2[layer])
        w1v, pev, w1vf, w2v = _compress_weights(pe_cv[layer], w_cv1[layer], w_cv2[layer])
        width = CMP_STRIDE * NSA_GROUPS * NSA_DH
        kcd, vcd = _compress(kc_in.reshape(B, n_chunks, width), vc_in.reshape(B, n_chunks, width),
                             w1k, w1v, pek, pev, w1kf, w1vf, w2k, w2v)
        ocw, sel = _nsa_cw(q, kcd, vcd, kwd, vwd, gates, ovt, tq_nsa)
        o_nsa = _nsa_sel(q, ksd, vsd, sel, emat, gates, ocw, tq_nsa, tk_sel)
        o_mla = _mla(qm, km, vm, tq_mla, tk_mla)
        x1 = _merge(o_nsa, o_mla, ga, gb, x, mod, w_o_nsa[layer].astype(BF16), w_o_mla[layer].astype(BF16),
                    w_out[layer].astype(BF16), tm_out)
        x = _mlp(x1, mod, g_mlp[layer][None], w_fc1[layer].astype(BF16), w_fc2[layer].astype(BF16),
                 g_final[None], tm_out, 1024, layer == depth - 1)
    return x
```

```python
import functools
import math

import jax
import jax.numpy as jnp
from jax import lax
from jax.experimental import pallas as pl
from jax.experimental.pallas import tpu as pltpu

F32 = jnp.float32
BF16 = jnp.bfloat16

D_MODEL = 1024
NSA_GROUPS = 2
NSA_HG = 4
NSA_DH = 64
CMP_LEN = 32
CMP_STRIDE = 16
CMP_HIDDEN = 256
SEL_BLOCK = 64
SEL_TOPK = 16
WINDOW = 512
MLA_HEADS = 8
MLA_NOPE = 64
MLA_ROPE = 32
MLA_V = 64
MLA_Q_RANK = 384
MLA_KV_RANK = 256
ROPE_THETA = 10000.0
D_FF = 4 * D_MODEL
EPS = 1e-6
FORCE_BONUS = 1000.0

LANE = 128
HALF = LANE // 2
NEG = -1e30
LOG2E = math.log2(math.e)
VMEM_LIMIT = 52 * 1024 * 1024
GATE_ROWS = 16

OFF_Q = 0
OFF_KC = 512
OFF_VC = 640
OFF_KS = 768
OFF_VS = 1024
OFF_KW = 1152
OFF_VW = 1408
OFF_G = 1536
OFF_QD = 1664
OFF_KVD = 2048
OFF_KRA = 2304
OFF_KRB = 2432
OFF_GA = 2560
OFF_GB = 3584
IN_COLS = 4608


def _dot(a, b):
    return jnp.dot(a, b, preferred_element_type=F32)


def _dot_nt(a, b):
    return lax.dot_general(a, b, (((1,), (1,)), ((), ())), preferred_element_type=F32)


def _rms(x, g):
    return x * lax.rsqrt(jnp.mean(x * x, axis=-1, keepdims=True) + EPS) * g


def _sigmoid(x):
    return 1.0 / (1.0 + jnp.exp(-x))


def _params(sem):
    return pltpu.CompilerParams(dimension_semantics=sem, vmem_limit_bytes=VMEM_LIMIT)


def _ada_body(c_ref, w_ref, b_ref, o_ref):
    o_ref[...] = jnp.dot(c_ref[...], w_ref[...], preferred_element_type=F32,
                         precision=lax.Precision.HIGHEST) + b_ref[...]


def _ada(c, w, b):
    nb, d = c.shape
    n = w.shape[1]
    rows = 8
    cp = jnp.zeros((rows, d), F32).at[:nb].set(c)
    tn = 1536
    out = pl.pallas_call(
        _ada_body,
        grid=(n // tn,),
        in_specs=[pl.BlockSpec((rows, d), lambda j: (0, 0)),
                  pl.BlockSpec((d, tn), lambda j: (0, j)),
                  pl.BlockSpec((1, tn), lambda j: (0, j))],
        out_specs=pl.BlockSpec((rows, tn), lambda j: (0, j)),
        out_shape=jax.ShapeDtypeStruct((rows, n), F32),
        compiler_params=_params(("arbitrary",)),
        name="ada",
    )(cp, w, b.reshape(1, n))
    return out[:nb].reshape(nb, 6, d)


def _inproj_body(x_ref, mod_ref, gmix_ref, w_ref, gcq_ref, wq2_ref, gckv_ref, wkv_ref,
                 cs_ref, sn_ref,
                 q_ref, kc_ref, vc_ref, ks_ref, vst_ref, kw_ref, vwt_ref, gt_ref,
                 qm_ref, km_ref, vmt_ref, ga_ref, gb_ref):
    x = x_ref[0]
    h = _rms(x, gmix_ref[...]) * (1.0 + mod_ref[0, 1:2, :]) + mod_ref[0, 0:1, :]
    hb = h.astype(BF16)

    def proj(lo, n):
        return _dot(hb, w_ref[:, lo:lo + n])

    q_ref[0] = (proj(OFF_Q, 512) * (NSA_DH ** -0.5 * LOG2E)).astype(BF16)
    kc_ref[0] = proj(OFF_KC, 128).astype(BF16)
    vc_ref[0] = proj(OFF_VC, 128).astype(BF16)
    ks_ref[0] = proj(OFF_KS, 256).astype(BF16)
    kw_ref[0] = proj(OFF_KW, 256).astype(BF16)
    vst_ref[0] = proj(OFF_VS, LANE).T.astype(BF16)
    vwt_ref[0] = proj(OFF_VW, LANE).T.astype(BF16)
    gt_ref[0] = _sigmoid(proj(OFF_G, LANE)).T[0:NSA_GROUPS * GATE_ROWS]
    ga_ref[0] = _sigmoid(proj(OFF_GA, 1024)).astype(BF16)
    gb_ref[0] = _sigmoid(proj(OFF_GB, 1024)).astype(BF16)

    cs = cs_ref[...]
    sn = sn_ref[...]
    nh = MLA_HEADS * LANE
    cq = _rms(proj(OFF_QD, MLA_Q_RANK), gcq_ref[...]).astype(BF16)
    q2 = _dot(cq, wq2_ref[...])
    scale = (MLA_NOPE + MLA_ROPE) ** -0.5 * LOG2E
    for hh in range(MLA_HEADS):
        a = q2[:, LANE * hh:LANE * (hh + 1)]
        b = q2[:, nh + LANE * hh:nh + LANE * (hh + 1)]
        qm_ref[0, :, LANE * hh:LANE * (hh + 1)] = ((a * cs + b * sn) * scale).astype(BF16)
    ckv = _rms(proj(OFF_KVD, MLA_KV_RANK), gckv_ref[...]).astype(BF16)
    kv = _dot(ckv, wkv_ref[...])
    kr = proj(OFF_KRA, LANE) * cs + proj(OFF_KRB, LANE) * sn
    for hh in range(MLA_HEADS):
        km_ref[0, :, LANE * hh:LANE * (hh + 1)] = (kv[:, LANE * hh:LANE * (hh + 1)] + kr).astype(BF16)
    vmt_ref[0] = kv[:, nh:].T.astype(BF16)


def _inproj(x, mod, g_mix, w_all, g_cq, wq2, g_ckv, wkv, cs, sn, tm):
    B, S, D = x.shape
    row = lambda n: pl.BlockSpec((1, tm, n), lambda b, i: (b, i, 0))
    col = lambda n: pl.BlockSpec((1, n, tm), lambda b, i: (b, 0, i))
    full = lambda a: pl.BlockSpec(a.shape, lambda b, i: (0,) * a.ndim)
    tok = lambda n, dt: (row(n), jax.ShapeDtypeStruct((B, S, n), dt))
    tr = lambda n, dt: (col(n), jax.ShapeDtypeStruct((B, n, S), dt))
    outs = [tok(512, BF16), tok(128, BF16), tok(128, BF16), tok(256, BF16), tr(LANE, BF16),
            tok(256, BF16), tr(LANE, BF16), tr(NSA_GROUPS * GATE_ROWS, F32),
            tok(1024, BF16), tok(1024, BF16), tr(MLA_HEADS * MLA_V, BF16), tok(1024, BF16), tok(1024, BF16)]
    return pl.pallas_call(
        _inproj_body,
        grid=(B, S // tm),
        in_specs=[row(D),
                  pl.BlockSpec((1, 6, D), lambda b, i: (b, 0, 0)),
                  full(g_mix), full(w_all), full(g_cq), full(wq2), full(g_ckv), full(wkv),
                  pl.BlockSpec((tm, LANE), lambda b, i: (i, 0)),
                  pl.BlockSpec((tm, LANE), lambda b, i: (i, 0))],
        out_specs=[o[0] for o in outs],
        out_shape=[o[1] for o in outs],
        compiler_params=_params(("parallel", "parallel")),
        name="inproj",
    )(x, mod, g_mix, w_all, g_cq, wq2, g_ckv, wkv, cs, sn)


def _compress_one(x, w1big, pe, w1f, w2d):
    hcat = _dot(x, w1big)
    const = jnp.dot(pe, w1f, preferred_element_type=F32, precision=lax.Precision.HIGHEST)[0:1]
    n = x.shape[0]
    rowid = lax.broadcasted_iota(jnp.int32, (n, LANE), 0)
    hid_w = CMP_HIDDEN
    outs = []
    for g in range(NSA_GROUPS):
        a = hcat[:, 2 * hid_w * g:2 * hid_w * g + hid_w]
        b = hcat[:, 2 * hid_w * g + hid_w:2 * hid_w * (g + 1)]
        hid = a + pltpu.roll(b, n - 1, 0) + const
        act = hid * _sigmoid(hid)
        out = _dot(act.astype(BF16), w2d)
        outs.append(jnp.where(rowid < n - 1, out, 0.0))
    return outs


def _compress_body(xk_ref, xv_ref, w1k_ref, w1v_ref, pek_ref, pev_ref, w1kf_ref, w1vf_ref,
                   w2k_ref, w2v_ref, kc_ref, vct_ref):
    kc = _compress_one(xk_ref[0], w1k_ref[...], pek_ref[...], w1kf_ref[...], w2k_ref[...])
    vc = _compress_one(xv_ref[0], w1v_ref[...], pev_ref[...], w1vf_ref[...], w2v_ref[...])
    for g in range(NSA_GROUPS):
        kc_ref[0, g] = kc[g].astype(BF16)
        vct_ref[0, g] = vc[g].T[0:NSA_DH].astype(BF16)


def _compress(xk, xv, w1k, w1v, pek, pev, w1kf, w1vf, w2k, w2v):
    B, n, width = xk.shape
    full = lambda a: pl.BlockSpec(a.shape, lambda b: (0,) * a.ndim)
    xs = pl.BlockSpec((1, n, width), lambda b: (b, 0, 0))
    return pl.pallas_call(
        _compress_body,
        grid=(B,),
        in_specs=[xs, xs, full(w1k), full(w1v), full(pek), full(pev), full(w1kf), full(w1vf),
                  full(w2k), full(w2v)],
        out_specs=[pl.BlockSpec((1, NSA_GROUPS, n, LANE), lambda b: (b, 0, 0, 0)),
                   pl.BlockSpec((1, NSA_GROUPS, NSA_DH, n), lambda b: (b, 0, 0, 0))],
        out_shape=[jax.ShapeDtypeStruct((B, NSA_GROUPS, n, LANE), BF16),
                   jax.ShapeDtypeStruct((B, NSA_GROUPS, NSA_DH, n), BF16)],
        compiler_params=_params(("parallel",)),
        name="compress",
    )(xk, xv, w1k, w1v, pek, pev, w1kf, w1vf, w2k, w2v)


def _stack_heads(q_ref, tq):
    upper = lax.broadcasted_iota(jnp.int32, (tq, LANE), 1) >= HALF
    parts = []
    for hh in range(NSA_HG):
        slab = q_ref[0, :, LANE * (hh // 2):LANE * (hh // 2 + 1)]
        keep = upper if hh % 2 else jnp.logical_not(upper)
        parts.append(jnp.where(keep, slab, jnp.zeros_like(slab)))
    return jnp.concatenate(parts, axis=0)


def _slope(g, hh):
    return jnp.where(g == 0, 1.0, 2.0 ** -NSA_HG).astype(F32) * (2.0 ** -(hh + 1) * LOG2E)


def _softmax_keys(s, mask):
    sm = jnp.where(mask, s, NEG)
    m = jnp.max(sm, axis=0, keepdims=True)
    e = jnp.where(mask, jnp.exp2(sm - m), 0.0)
    return e * (1.0 / jnp.maximum(jnp.sum(e, axis=0, keepdims=True), 1e-30))


def _nsa_cw_body(q_ref, kc_ref, vct_ref, kw_ref, vwt_ref, gt_ref, ovt_ref, ocw_ref, sel_ref, *, tq, seq):
    g = pl.program_id(1)
    t0 = pl.program_id(2) * tq
    qst = _stack_heads(q_ref, tq)
    gates = gt_ref[0]

    kc = kc_ref[0, 0]
    vct = vct_ref[0, 0]
    ncp = kc.shape[0]
    st = _dot_nt(kc, qst)
    t_c = t0 + lax.broadcasted_iota(jnp.int32, (ncp, tq), 1)
    c_end = lax.broadcasted_iota(jnp.int32, (ncp, tq), 0) * CMP_STRIDE + (CMP_LEN - 1)
    dist_c = (t_c - c_end).astype(F32)
    mask_c = dist_c >= 0.0
    psum = jnp.zeros((ncp, tq), F32)
    o_cmp = []
    for hh in range(NSA_HG):
        p = _softmax_keys(st[:, hh * tq:(hh + 1) * tq] - _slope(g, hh) * dist_c, mask_c)
        psum = psum + p
        o_cmp.append(_dot(vct, p.astype(BF16)))

    ovt = ovt_ref[...]
    p1 = psum.astype(BF16)
    r1 = psum - p1.astype(F32)
    p2 = r1.astype(BF16)
    p3 = (r1 - p2.astype(F32)).astype(BF16)
    p_slc = _dot(ovt, p1) + _dot(ovt, p2) + _dot(ovt, p3)
    nsp = ovt.shape[0]
    blk = lax.broadcasted_iota(jnp.int32, (nsp, tq), 0)
    cur = (t0 + lax.broadcasted_iota(jnp.int32, (nsp, tq), 1)) // SEL_BLOCK
    valid = blk <= cur
    forced = (blk == 0) | (blk >= cur - 1)
    score = jnp.where(valid, p_slc + jnp.where(forced, FORCE_BONUS, 0.0), -1.0)
    picked = jnp.zeros((nsp, tq), F32)
    for _ in range(min(SEL_TOPK, seq // SEL_BLOCK)):
        m = jnp.max(score, axis=0, keepdims=True)
        first = jnp.min(jnp.where(score == m, blk, nsp), axis=0, keepdims=True)
        hit = blk == first
        picked = jnp.where(hit, 1.0, picked)
        score = jnp.where(hit, -1.0, score)
    sel_ref[0, 0] = jnp.where(valid, picked, 0.0)

    wk = WINDOW + tq
    k_start = pl.multiple_of(jnp.clip(t0 - WINDOW, 0, seq - wk), LANE)
    kw = kw_ref[0, pl.ds(k_start, wk), :]
    vwt = vwt_ref[0, :, pl.ds(k_start, wk)]
    st = _dot_nt(kw, qst)
    t_w = t0 + lax.broadcasted_iota(jnp.int32, (wk, tq), 1)
    dist_i = t_w - (k_start + lax.broadcasted_iota(jnp.int32, (wk, tq), 0))
    mask_w = (dist_i >= 0) & (dist_i < WINDOW)
    dist_w = dist_i.astype(F32)
    for hh in range(NSA_HG):
        p = _softmax_keys(st[:, hh * tq:(hh + 1) * tq] - _slope(g, hh) * dist_w, mask_w)
        o_win = _dot(vwt, p.astype(BF16))
        ocw_ref[0, 0, NSA_DH * hh:NSA_DH * (hh + 1), :] = (
            gates[hh:hh + 1] * o_cmp[hh] + gates[2 * NSA_HG + hh:2 * NSA_HG + hh + 1] * o_win)


def _nsa_cw(q, kcd, vct, kwd, vwt, gt, ovt, tq):
    B, S, _ = q.shape
    ncp = kcd.shape[2]
    nsp = ovt.shape[0]
    return pl.pallas_call(
        functools.partial(_nsa_cw_body, tq=tq, seq=S),
        grid=(B, NSA_GROUPS, S // tq),
        in_specs=[pl.BlockSpec((1, tq, 2 * LANE), lambda b, g, i: (b, i, g)),
                  pl.BlockSpec((1, 1, ncp, LANE), lambda b, g, i: (b, g, 0, 0)),
                  pl.BlockSpec((1, 1, NSA_DH, ncp), lambda b, g, i: (b, g, 0, 0)),
                  pl.BlockSpec((1, S, LANE), lambda b, g, i: (b, 0, g)),
                  pl.BlockSpec((1, NSA_DH, S), lambda b, g, i: (b, g, 0)),
                  pl.BlockSpec((1, GATE_ROWS, tq), lambda b, g, i: (b, g, i)),
                  pl.BlockSpec(ovt.shape, lambda b, g, i: (0, 0))],
        out_specs=[pl.BlockSpec((1, 1, NSA_HG * NSA_DH, tq), lambda b, g, i: (b, g, 0, i)),
                   pl.BlockSpec((1, 1, nsp, tq), lambda b, g, i: (b, g, 0, i))],
        out_shape=[jax.ShapeDtypeStruct((B, NSA_GROUPS, NSA_HG * NSA_DH, S), F32),
                   jax.ShapeDtypeStruct((B, NSA_GROUPS, nsp, S), F32)],
        compiler_params=_params(("parallel", "parallel", "arbitrary")),
        name="nsa_cw",
    )(q, kcd, vct, kwd, vwt, gt, ovt)


def _nsa_sel_body(q_ref, ks_ref, vst_ref, sel_ref, gt_ref, ocw_ref, o_ref,
                  m_sc, l_sc, acc_sc, skb_sc, *, tq, tk):
    g = pl.program_id(1)
    t0 = pl.program_id(2) * tq
    qst = _stack_heads(q_ref, tq)
    key_i = lax.broadcasted_iota(jnp.int32, (tk, tq), 0)
    qry_i = lax.broadcasted_iota(jnp.int32, (tk, tq), 1)
    key_f = key_i.astype(F32)
    for hh in range(NSA_HG):
        skb_sc[hh] = _slope(g, hh) * key_f
    m_sc[...] = jnp.full(m_sc.shape, NEG, F32)
    l_sc[...] = jnp.zeros(l_sc.shape, F32)
    acc_sc[...] = jnp.zeros(acc_sc.shape, F32)
    j_diag = t0 // tk

    def tile(k0, diagonal):
        k = ks_ref[0, pl.ds(k0, tk), :]
        vt = vst_ref[0, :, pl.ds(k0, tk)]
        st = _dot_nt(k, qst)
        b0 = k0 // SEL_BLOCK
        pieces = []
        for r in range(tk // SEL_BLOCK):
            chosen = sel_ref[0, 0, pl.ds(b0 + r, 1), :]
            pieces.append(jnp.broadcast_to(jnp.where(chosen > 0.5, 0.0, NEG), (SEL_BLOCK, tq)))
        madd = jnp.concatenate(pieces, axis=0)
        if diagonal:
            madd = jnp.where(key_i + k0 <= qry_i + t0, madd, NEG)
        for hh in range(NSA_HG):
            c = _slope(g, hh) * (k0 - t0).astype(F32)
            z = st[:, hh * tq:(hh + 1) * tq] + skb_sc[hh] + madd
            m_prev = m_sc[hh]
            m_new = jnp.maximum(m_prev, jnp.max(z, axis=0, keepdims=True) + c)
            alpha = jnp.exp2(m_prev - m_new)
            p = jnp.exp2(z - (m_new - c))
            l_sc[hh] = alpha * l_sc[hh] + jnp.sum(p, axis=0, keepdims=True)
            acc_sc[hh] = alpha * acc_sc[hh] + _dot(vt, p.astype(BF16))
            m_sc[hh] = m_new

    tile(pl.multiple_of(j_diag * tk, tk), True)

    def step(it, carry):
        tile(pl.multiple_of((j_diag - 1 - it) * tk, tk), False)
        return carry

    lax.fori_loop(0, j_diag, step, 0)

    gates = gt_ref[0]
    for pair in range(NSA_HG // 2):
        halves = []
        for hh in (2 * pair, 2 * pair + 1):
            w = gates[NSA_HG + hh:NSA_HG + hh + 1] / jnp.maximum(l_sc[hh], 1e-30)
            halves.append(acc_sc[hh] * w + ocw_ref[0, 0, NSA_DH * hh:NSA_DH * (hh + 1), :])
        o_ref[0, :, LANE * pair:LANE * (pair + 1)] = jnp.concatenate(halves, axis=0).T.astype(BF16)


def _nsa_sel(q, ksd, vst, sel, gt, ocw, tq, tk):
    B, S, _ = q.shape
    nsp = sel.shape[2]
    qs = pl.BlockSpec((1, tq, 2 * LANE), lambda b, g, i: (b, i, g))
    return pl.pallas_call(
        functools.partial(_nsa_sel_body, tq=tq, tk=tk),
        grid=(B, NSA_GROUPS, S // tq),
        in_specs=[qs,
                  pl.BlockSpec((1, S, LANE), lambda b, g, i: (b, 0, g)),
                  pl.BlockSpec((1, NSA_DH, S), lambda b, g, i: (b, g, 0)),
                  pl.BlockSpec((1, 1, nsp, tq), lambda b, g, i: (b, g, 0, i)),
                  pl.BlockSpec((1, GATE_ROWS, tq), lambda b, g, i: (b, g, i)),
                  pl.BlockSpec((1, 1, NSA_HG * NSA_DH, tq), lambda b, g, i: (b, g, 0, i))],
        out_specs=qs,
        out_shape=jax.ShapeDtypeStruct((B, S, 4 * LANE), BF16),
        scratch_shapes=[pltpu.VMEM((NSA_HG, 1, tq), F32), pltpu.VMEM((NSA_HG, 1, tq), F32),
                        pltpu.VMEM((NSA_HG, NSA_DH, tq), F32), pltpu.VMEM((NSA_HG, tk, tq), F32)],
        compiler_params=_params(("parallel", "parallel", "arbitrary")),
        name="nsa_sel",
    )(q, ksd, vst, sel, gt, ocw)


def _mla_body(q_ref, k_ref, vt_ref, o_ref, m_sc, l_sc, acc_sc, *, tq):
    t0 = pl.program_id(2) * tq
    causal = (lax.broadcasted_iota(jnp.int32, (tq, tq), 0)
              <= lax.broadcasted_iota(jnp.int32, (tq, tq), 1))
    m_sc[...] = jnp.full(m_sc.shape, NEG, F32)
    l_sc[...] = jnp.zeros(l_sc.shape, F32)
    acc_sc[...] = jnp.zeros(acc_sc.shape, F32)

    def tile(k0, diagonal):
        for hh in range(2):
            lanes = slice(LANE * hh, LANE * (hh + 1))
            st = _dot_nt(k_ref[0, pl.ds(k0, tq), lanes], q_ref[0, :, lanes])
            if diagonal:
                st = jnp.where(causal, st, NEG)
            m_prev = m_sc[hh]
            m_new = jnp.maximum(m_prev, jnp.max(st, axis=0, keepdims=True))
            alpha = jnp.exp2(m_prev - m_new)
            p = jnp.exp2(st - m_new)
            l_sc[hh] = alpha * l_sc[hh] + jnp.sum(p, axis=0, keepdims=True)
            acc_sc[hh] = alpha * acc_sc[hh] + _dot(vt_ref[0, MLA_V * hh:MLA_V * (hh + 1), pl.ds(k0, tq)],
                                                   p.astype(BF16))
            m_sc[hh] = m_new

    def step(j, carry):
        tile(pl.multiple_of(j * tq, tq), False)
        return carry

    lax.fori_loop(0, t0 // tq, step, 0)
    tile(pl.multiple_of(t0, tq), True)
    halves = [acc_sc[hh] * (1.0 / jnp.maximum(l_sc[hh], 1e-30)) for hh in range(2)]
    o_ref[0] = jnp.concatenate(halves, axis=0).T.astype(BF16)


def _mla(qm, km, vmt, tq):
    B, S, _ = qm.shape
    return pl.pallas_call(
        functools.partial(_mla_body, tq=tq),
        grid=(B, MLA_HEADS // 2, S // tq),
        in_specs=[pl.BlockSpec((1, tq, 2 * LANE), lambda b, p, i: (b, i, p)),
                  pl.BlockSpec((1, S, 2 * LANE), lambda b, p, i: (b, 0, p)),
                  pl.BlockSpec((1, 2 * MLA_V, S), lambda b, p, i: (b, p, 0))],
        out_specs=pl.BlockSpec((1, tq, LANE), lambda b, p, i: (b, i, p)),
        out_shape=jax.ShapeDtypeStruct((B, S, MLA_HEADS * MLA_V), BF16),
        scratch_shapes=[pltpu.VMEM((2, 1, tq), F32), pltpu.VMEM((2, 1, tq), F32),
                        pltpu.VMEM((2, MLA_V, tq), F32)],
        compiler_params=_params(("parallel", "parallel", "arbitrary")),
        name="mla",
    )(qm, km, vmt)


def _merge_body(on_ref, om_ref, ga_ref, gb_ref, x_ref, mod_ref, won_ref, wom_ref, wout_ref, o_ref):
    y = (ga_ref[0].astype(F32) * _dot(on_ref[0], won_ref[...])
         + gb_ref[0].astype(F32) * _dot(om_ref[0], wom_ref[...]))
    o_ref[0] = x_ref[0] + mod_ref[0, 2:3, :] * _dot(y.astype(BF16), wout_ref[...])


def _merge(o_nsa, o_mla, ga, gb, x, mod, w_on, w_om, w_out, tm):
    B, S, D = x.shape
    row = lambda n: pl.BlockSpec((1, tm, n), lambda b, i: (b, i, 0))
    full = lambda a: pl.BlockSpec(a.shape, lambda b, i: (0,) * a.ndim)
    return pl.pallas_call(
        _merge_body,
        grid=(B, S // tm),
        in_specs=[row(o_nsa.shape[2]), row(o_mla.shape[2]), row(D), row(D), row(D),
                  pl.BlockSpec((1, 6, D), lambda b, i: (b, 0, 0)), full(w_on), full(w_om), full(w_out)],
        out_specs=row(D),
        out_shape=jax.ShapeDtypeStruct((B, S, D), F32),
        compiler_params=_params(("parallel", "parallel")),
        name="merge",
    )(o_nsa, o_mla, ga, gb, x, mod, w_on, w_om, w_out)


def _mlp_body(x_ref, mod_ref, gmlp_ref, w1_ref, w2_ref, gfin_ref, o_ref, *, fc, final):
    x1 = x_ref[0]
    h = _rms(x1, gmlp_ref[...]) * (1.0 + mod_ref[0, 4:5, :]) + mod_ref[0, 3:4, :]
    hb = h.astype(BF16)
    acc = jnp.zeros(x1.shape, F32)
    for c in range(w1_ref.shape[1] // fc):
        a = jnp.maximum(_dot(hb, w1_ref[:, c * fc:(c + 1) * fc]), 0.0)
        acc = acc + _dot((a * a).astype(BF16), w2_ref[c * fc:(c + 1) * fc, :])
    x2 = x1 + mod_ref[0, 5:6, :] * acc
    o_ref[0] = _rms(x2, gfin_ref[...]) if final else x2


def _mlp(x1, mod, g_mlp, w1, w2, g_fin, tm, fc, final):
    B, S, D = x1.shape
    row = pl.BlockSpec((1, tm, D), lambda b, i: (b, i, 0))
    full = lambda a: pl.BlockSpec(a.shape, lambda b, i: (0,) * a.ndim)
    return pl.pallas_call(
        functools.partial(_mlp_body, fc=fc, final=final),
        grid=(B, S // tm),
        in_specs=[row, pl.BlockSpec((1, 6, D), lambda b, i: (b, 0, 0)), full(g_mlp), full(w1), full(w2),
                  full(g_fin)],
        out_specs=row,
        out_shape=jax.ShapeDtypeStruct((B, S, D), F32),
        compiler_params=_params(("parallel", "parallel")),
        name="mlp",
    )(x1, mod, g_mlp, w1, w2, g_fin)


def _dup_groups(w):
    g0, g1 = w[:, :NSA_DH], w[:, NSA_DH:]
    return jnp.concatenate([g0, g0, g1, g1], axis=1)


def _inproj_weight(w_in):
    d = w_in.shape[0]
    o = 0
    parts = {}
    for name, n in (("q", 512), ("kc", 128), ("vc", 128), ("ks", 128), ("vs", 128), ("kw", 128),
                    ("vw", 128), ("g", 24), ("qd", MLA_Q_RANK), ("kvd", MLA_KV_RANK), ("kr", MLA_ROPE),
                    ("ga", D_MODEL), ("gb", D_MODEL)):
        parts[name] = w_in[:, o:o + n]
        o += n
    wg = parts["g"].reshape(d, 3, NSA_GROUPS, NSA_HG).transpose(0, 2, 1, 3).reshape(d, NSA_GROUPS, 3 * NSA_HG)
    wg = jnp.pad(wg, ((0, 0), (0, 0), (0, GATE_ROWS - 3 * NSA_HG))).reshape(d, NSA_GROUPS * GATE_ROWS)
    wg = jnp.pad(wg, ((0, 0), (0, LANE - NSA_GROUPS * GATE_ROWS)))
    half = MLA_ROPE // 2
    x1, x2 = parts["kr"][:, :half], parts["kr"][:, half:]
    z_lo = jnp.zeros((d, MLA_NOPE), F32)
    z_hi = jnp.zeros((d, LANE - MLA_NOPE - MLA_ROPE), F32)
    kra = jnp.concatenate([z_lo, x1, x2, z_hi], axis=1)
    krb = jnp.concatenate([z_lo, -x2, x1, z_hi], axis=1)
    w_all = jnp.concatenate(
        [parts["q"], parts["kc"], parts["vc"], _dup_groups(parts["ks"]), parts["vs"],
         _dup_groups(parts["kw"]), parts["vw"], wg, parts["qd"], parts["kvd"], kra, krb,
         parts["ga"], parts["gb"]], axis=1)
    assert w_all.shape[1] == IN_COLS
    return w_all.astype(BF16)


def _mla_q_weight(w_uq):
    r = w_uq.shape[0]
    w = w_uq.reshape(r, MLA_HEADS, MLA_NOPE + MLA_ROPE)
    half = MLA_ROPE // 2
    nope, x1, x2 = w[..., :MLA_NOPE], w[..., MLA_NOPE:MLA_NOPE + half], w[..., MLA_NOPE + half:]
    z_hi = jnp.zeros((r, MLA_HEADS, LANE - MLA_NOPE - MLA_ROPE), F32)
    a = jnp.concatenate([nope, x1, x2, z_hi], axis=-1).reshape(r, MLA_HEADS * LANE)
    b = jnp.concatenate([jnp.zeros_like(nope), -x2, x1, z_hi], axis=-1).reshape(r, MLA_HEADS * LANE)
    return jnp.concatenate([a, b], axis=1).astype(BF16)


def _mla_kv_weight(w_uk, w_uv):
    r = w_uk.shape[0]
    k = w_uk.reshape(r, MLA_HEADS, MLA_NOPE)
    k = jnp.pad(k, ((0, 0), (0, 0), (0, LANE - MLA_NOPE))).reshape(r, MLA_HEADS * LANE)
    return jnp.concatenate([k, w_uv], axis=1).astype(BF16)


def _rope_tables(seq):
    half = MLA_ROPE // 2
    pos = jnp.arange(seq, dtype=F32)
    inv_freq = ROPE_THETA ** (-jnp.arange(half, dtype=F32) / half)
    ang = pos[:, None] * inv_freq[None, :]
    cos, sin = jnp.cos(ang), jnp.sin(ang)
    ones = jnp.ones((seq, MLA_NOPE), F32)
    z_lo = jnp.zeros((seq, MLA_NOPE), F32)
    z_hi = jnp.zeros((seq, LANE - MLA_NOPE - MLA_ROPE), F32)
    return (jnp.concatenate([ones, cos, cos, z_hi], axis=1),
            jnp.concatenate([z_lo, sin, sin, z_hi], axis=1))


def _compress_weights(pe, w1, w2):
    hid = w1.shape[1]
    wa = w1[:CMP_STRIDE * NSA_DH].reshape(CMP_STRIDE, NSA_DH, hid)
    wb = w1[CMP_STRIDE * NSA_DH:].reshape(CMP_STRIDE, NSA_DH, hid)
    z = jnp.zeros_like(wa)
    cols = []
    for g in range(NSA_GROUPS):
        for w in (wa, wb):
            slots = [z, z]
            slots[g] = w
            cols.append(jnp.stack(slots, axis=1).reshape(CMP_STRIDE * NSA_GROUPS * NSA_DH, hid))
    w1big = jnp.concatenate(cols, axis=1).astype(BF16)
    pe8 = jnp.zeros((8, CMP_LEN * NSA_DH), F32).at[0].set(pe.reshape(-1))
    w2d = jnp.concatenate([w2, w2], axis=1).astype(BF16)
    return w1big, pe8, w1, w2d


def _overlap_table(seq):
    ns = seq // SEL_BLOCK
    nsp = -(-ns // LANE) * LANE
    ncp = seq // CMP_STRIDE
    s_id = jnp.arange(nsp)[:, None]
    c_id = jnp.arange(ncp)[None, :]
    c_start = c_id * CMP_STRIDE
    return ((c_start < s_id * SEL_BLOCK + SEL_BLOCK) & (c_start + CMP_LEN - 1 >= s_id * SEL_BLOCK)
            & (c_id < ncp - 1)).astype(BF16)


def kernel(x, c, w_ada, b_ada, g_mix, w_in, pe_ck, w_ck1, w_ck2, pe_cv, w_cv1, w_cv2, g_cq, w_uq, g_ckv,
           w_uk, w_uv, w_o_nsa, w_o_mla, w_out, g_mlp, w_fc1, w_fc2, g_final):
    B, S, D = x.shape
    depth = w_ada.shape[0]
    tm_in = min(256, S)
    tm_out = min(512, S)
    tq_nsa = min(256, S)
    tk_sel = min(512, S)
    tq_mla = min(512, S)
    cs, sn = _rope_tables(S)
    ovt = _overlap_table(S)
    n_chunks = S // CMP_STRIDE
    for layer in range(depth):
        mod = _ada(c, w_ada[layer], b_ada[layer])
        (q, kc_in, vc_in, ksd, vst, kwd, vwt, gt, qm, km, vmt, ga, gb) = _inproj(
            x, mod, g_mix[layer][None], _inproj_weight(w_in[layer]), g_cq[layer][None],
            _mla_q_weight(w_uq[layer]), g_ckv[layer][None], _mla_kv_weight(w_uk[layer], w_uv[layer]),
            cs, sn, tm_in)
        w1k, pek, w1kf, w2k = _compress_weights(pe_ck[layer], w_ck1[layer], w_ck2[layer])
        w1v, pev, w1vf, w2v = _compress_weights(pe_cv[layer], w_cv1[layer], w_cv2[layer])
        width = CMP_STRIDE * NSA_GROUPS * NSA_DH
        kcd, vct = _compress(kc_in.reshape(B, n_chunks, width), vc_in.reshape(B, n_chunks, width),
                             w1k, w1v, pek, pev, w1kf, w1vf, w2k, w2v)
        ocw, sel = _nsa_cw(q, kcd, vct, kwd, vwt, gt, ovt, tq_nsa)
        o_nsa = _nsa_sel(q, ksd, vst, sel, gt, ocw, tq_nsa, tk_sel)
        o_mla = _mla(qm, km, vmt, tq_mla)
        x1 = _merge(o_nsa, o_mla, ga, gb, x, mod, w_o_nsa[layer].astype(BF16), w_o_mla[layer].astype(BF16),
                    w_out[layer].astype(BF16), tm_out)
        x = _mlp(x1, mod, g_mlp[layer][None], w_fc1[layer].astype(BF16), w_fc2[layer].astype(BF16),
                 g_final[None], tm_out, 1024, layer == depth - 1)
    return x
```

```python
import functools
import math

import jax
import jax.numpy as jnp
from jax import lax
from jax.experimental import pallas as pl
from jax.experimental.pallas import tpu as pltpu

F32 = jnp.float32
BF16 = jnp.bfloat16

D_MODEL = 1024
NSA_GROUPS = 2
NSA_HG = 4
NSA_DH = 64
CMP_LEN = 32
CMP_STRIDE = 16
CMP_HIDDEN = 256
SEL_BLOCK = 64
SEL_TOPK = 16
WINDOW = 512
MLA_HEADS = 8
MLA_NOPE = 64
MLA_ROPE = 32
MLA_V = 64
MLA_Q_RANK = 384
MLA_KV_RANK = 256
ROPE_THETA = 10000.0
D_FF = 4 * D_MODEL
EPS = 1e-6
FORCE_BONUS = 1000.0

LANE = 128
HALF = LANE // 2
NEG = -1e30
LOG2E = math.log2(math.e)
VMEM_LIMIT = 52 * 1024 * 1024
GATE_ROWS = 16
ONES_ROWS = 16

OFF_Q = 0
OFF_KC = 512
OFF_VC = 640
OFF_KS = 768
OFF_VS = 1024
OFF_KW = 1152
OFF_VW = 1408
OFF_G = 1536
OFF_QD = 1664
OFF_KVD = 2048
OFF_KRA = 2304
OFF_KRB = 2432
OFF_GA = 2560
OFF_GB = 3584
IN_COLS = 4608


def _dot(a, b):
    return jnp.dot(a, b, preferred_element_type=F32)


def _dot_nt(a, b):
    return lax.dot_general(a, b, (((1,), (1,)), ((), ())), preferred_element_type=F32)


def _rms(x, g):
    return x * lax.rsqrt(jnp.mean(x * x, axis=-1, keepdims=True) + EPS) * g


def _sigmoid(x):
    return 1.0 / (1.0 + jnp.exp(-x))


def _params(sem):
    return pltpu.CompilerParams(dimension_semantics=sem, vmem_limit_bytes=VMEM_LIMIT)


def _ada_body(c_ref, w_ref, b_ref, o_ref):
    o_ref[...] = jnp.dot(c_ref[...], w_ref[...], preferred_element_type=F32,
                         precision=lax.Precision.HIGHEST) + b_ref[...]


def _ada(c, w, b):
    nb, d = c.shape
    n = w.shape[1]
    rows = 8
    cp = jnp.zeros((rows, d), F32).at[:nb].set(c)
    tn = 1536
    out = pl.pallas_call(
        _ada_body,
        grid=(n // tn,),
        in_specs=[pl.BlockSpec((rows, d), lambda j: (0, 0)),
                  pl.BlockSpec((d, tn), lambda j: (0, j)),
                  pl.BlockSpec((1, tn), lambda j: (0, j))],
        out_specs=pl.BlockSpec((rows, tn), lambda j: (0, j)),
        out_shape=jax.ShapeDtypeStruct((rows, n), F32),
        compiler_params=_params(("arbitrary",)),
        name="ada",
    )(cp, w, b.reshape(1, n))
    return out[:nb].reshape(nb, 6, d)


def _inproj_body(x_ref, mod_ref, gmix_ref, w_ref, gcq_ref, wq2_ref, gckv_ref, wkv_ref,
                 cs_ref, sn_ref,
                 q_ref, kc_ref, vc_ref, ks_ref, vst_ref, kw_ref, vwt_ref, gt_ref,
                 qm_ref, km_ref, vmt_ref, ga_ref, gb_ref):
    x = x_ref[0]
    h = _rms(x, gmix_ref[...]) * (1.0 + mod_ref[0, 1:2, :]) + mod_ref[0, 0:1, :]
    hb = h.astype(BF16)

    def proj(lo, n):
        return _dot(hb, w_ref[:, lo:lo + n])

    q_ref[0] = (proj(OFF_Q, 512) * (NSA_DH ** -0.5 * LOG2E)).astype(BF16)
    kc_ref[0] = proj(OFF_KC, 128).astype(BF16)
    vc_ref[0] = proj(OFF_VC, 128).astype(BF16)
    ks_ref[0] = proj(OFF_KS, 256).astype(BF16)
    kw_ref[0] = proj(OFF_KW, 256).astype(BF16)
    vst_ref[0] = proj(OFF_VS, LANE).T.astype(BF16)
    vwt_ref[0] = proj(OFF_VW, LANE).T.astype(BF16)
    gt_ref[0] = _sigmoid(proj(OFF_G, LANE)).T[0:NSA_GROUPS * GATE_ROWS]
    ga_ref[0] = _sigmoid(proj(OFF_GA, 1024)).astype(BF16)
    gb_ref[0] = _sigmoid(proj(OFF_GB, 1024)).astype(BF16)

    cs = cs_ref[...]
    sn = sn_ref[...]
    nh = MLA_HEADS * LANE
    cq = _rms(proj(OFF_QD, MLA_Q_RANK), gcq_ref[...]).astype(BF16)
    q2 = _dot(cq, wq2_ref[...])
    scale = (MLA_NOPE + MLA_ROPE) ** -0.5 * LOG2E
    for hh in range(MLA_HEADS):
        a = q2[:, LANE * hh:LANE * (hh + 1)]
        b = q2[:, nh + LANE * hh:nh + LANE * (hh + 1)]
        qm_ref[0, :, LANE * hh:LANE * (hh + 1)] = ((a * cs + b * sn) * scale).astype(BF16)
    ckv = _rms(proj(OFF_KVD, MLA_KV_RANK), gckv_ref[...]).astype(BF16)
    kv = _dot(ckv, wkv_ref[...])
    kr = proj(OFF_KRA, LANE) * cs + proj(OFF_KRB, LANE) * sn
    for hh in range(MLA_HEADS):
        km_ref[0, :, LANE * hh:LANE * (hh + 1)] = (kv[:, LANE * hh:LANE * (hh + 1)] + kr).astype(BF16)
    vmt_ref[0] = kv[:, nh:].T.astype(BF16)


def _inproj(x, mod, g_mix, w_all, g_cq, wq2, g_ckv, wkv, cs, sn, tm):
    B, S, D = x.shape
    row = lambda n: pl.BlockSpec((1, tm, n), lambda b, i: (b, i, 0))
    col = lambda n: pl.BlockSpec((1, n, tm), lambda b, i: (b, 0, i))
    full = lambda a: pl.BlockSpec(a.shape, lambda b, i: (0,) * a.ndim)
    tok = lambda n, dt: (row(n), jax.ShapeDtypeStruct((B, S, n), dt))
    tr = lambda n, dt: (col(n), jax.ShapeDtypeStruct((B, n, S), dt))
    outs = [tok(512, BF16), tok(128, BF16), tok(128, BF16), tok(256, BF16), tr(LANE, BF16),
            tok(256, BF16), tr(LANE, BF16), tr(NSA_GROUPS * GATE_ROWS, F32),
            tok(1024, BF16), tok(1024, BF16), tr(MLA_HEADS * MLA_V, BF16), tok(1024, BF16), tok(1024, BF16)]
    return pl.pallas_call(
        _inproj_body,
        grid=(B, S // tm),
        in_specs=[row(D),
                  pl.BlockSpec((1, 6, D), lambda b, i: (b, 0, 0)),
                  full(g_mix), full(w_all), full(g_cq), full(wq2), full(g_ckv), full(wkv),
                  pl.BlockSpec((tm, LANE), lambda b, i: (i, 0)),
                  pl.BlockSpec((tm, LANE), lambda b, i: (i, 0))],
        out_specs=[o[0] for o in outs],
        out_shape=[o[1] for o in outs],
        compiler_params=_params(("parallel", "parallel")),
        name="inproj",
    )(x, mod, g_mix, w_all, g_cq, wq2, g_ckv, wkv, cs, sn)


def _compress_one(x, w1big, pe, w1f, w2d):
    hcat = _dot(x, w1big)
    const = jnp.dot(pe, w1f, preferred_element_type=F32, precision=lax.Precision.HIGHEST)[0:1]
    n = x.shape[0]
    rowid = lax.broadcasted_iota(jnp.int32, (n, LANE), 0)
    hid_w = CMP_HIDDEN
    outs = []
    for g in range(NSA_GROUPS):
        a = hcat[:, 2 * hid_w * g:2 * hid_w * g + hid_w]
        b = hcat[:, 2 * hid_w * g + hid_w:2 * hid_w * (g + 1)]
        hid = a + pltpu.roll(b, n - 1, 0) + const
        act = hid * _sigmoid(hid)
        out = _dot(act.astype(BF16), w2d)
        outs.append(jnp.where(rowid < n - 1, out, 0.0))
    return outs


def _compress_body(xk_ref, xv_ref, w1k_ref, w1v_ref, pek_ref, pev_ref, w1kf_ref, w1vf_ref,
                   w2k_ref, w2v_ref, kc_ref, vct_ref):
    kc = _compress_one(xk_ref[0], w1k_ref[...], pek_ref[...], w1kf_ref[...], w2k_ref[...])
    vc = _compress_one(xv_ref[0], w1v_ref[...], pev_ref[...], w1vf_ref[...], w2v_ref[...])
    for g in range(NSA_GROUPS):
        kc_ref[0, g] = kc[g].astype(BF16)
        vct_ref[0, g] = vc[g].T[0:NSA_DH].astype(BF16)


def _compress(xk, xv, w1k, w1v, pek, pev, w1kf, w1vf, w2k, w2v):
    B, n, width = xk.shape
    full = lambda a: pl.BlockSpec(a.shape, lambda b: (0,) * a.ndim)
    xs = pl.BlockSpec((1, n, width), lambda b: (b, 0, 0))
    return pl.pallas_call(
        _compress_body,
        grid=(B,),
        in_specs=[xs, xs, full(w1k), full(w1v), full(pek), full(pev), full(w1kf), full(w1vf),
                  full(w2k), full(w2v)],
        out_specs=[pl.BlockSpec((1, NSA_GROUPS, n, LANE), lambda b: (b, 0, 0, 0)),
                   pl.BlockSpec((1, NSA_GROUPS, NSA_DH, n), lambda b: (b, 0, 0, 0))],
        out_shape=[jax.ShapeDtypeStruct((B, NSA_GROUPS, n, LANE), BF16),
                   jax.ShapeDtypeStruct((B, NSA_GROUPS, NSA_DH, n), BF16)],
        compiler_params=_params(("parallel",)),
        name="compress",
    )(xk, xv, w1k, w1v, pek, pev, w1kf, w1vf, w2k, w2v)


def _stack_heads(q_ref, tq):
    upper = lax.broadcasted_iota(jnp.int32, (tq, LANE), 1) >= HALF
    parts = []
    for hh in range(NSA_HG):
        slab = q_ref[0, :, LANE * (hh // 2):LANE * (hh // 2 + 1)]
        keep = upper if hh % 2 else jnp.logical_not(upper)
        parts.append(jnp.where(keep, slab, jnp.zeros_like(slab)))
    return jnp.concatenate(parts, axis=0)


def _slope(g, hh):
    return jnp.where(g == 0, 1.0, 2.0 ** -NSA_HG).astype(F32) * (2.0 ** -(hh + 1) * LOG2E)


def _with_ones(vt):
    return jnp.concatenate([vt, jnp.ones((ONES_ROWS, vt.shape[1]), vt.dtype)], axis=0)


def _nsa_cw_body(q_ref, kc_ref, vct_ref, kw_ref, vwt_ref, gt_ref, ovt_ref, gm_ref,
                 ocw_ref, sel_ref, cnt_ref, bc_sc, bw_sc, *, tq, seq):
    g = pl.program_id(1)
    t0 = pl.program_id(2) * tq
    qst = _stack_heads(q_ref, tq)
    gates = gt_ref[0]
    kc = kc_ref[0, 0]
    ncp = kc.shape[0]
    wk = WINDOW + tq

    @pl.when(pl.program_id(2) == 0)
    def _():
        c_end_f = (lax.broadcasted_iota(jnp.int32, (ncp, tq), 0) * CMP_STRIDE + (CMP_LEN - 1)).astype(F32)
        key_f = lax.broadcasted_iota(jnp.int32, (wk, tq), 0).astype(F32)
        for hh in range(NSA_HG):
            bc_sc[hh] = _slope(g, hh) * c_end_f
            bw_sc[hh] = _slope(g, hh) * key_f

    vct = _with_ones(vct_ref[0, 0])
    st = _dot_nt(kc, qst)
    t_c = t0 + lax.broadcasted_iota(jnp.int32, (ncp, tq), 1)
    c_end = lax.broadcasted_iota(jnp.int32, (ncp, tq), 0) * CMP_STRIDE + (CMP_LEN - 1)
    madd_c = jnp.where(t_c >= c_end, 0.0, NEG)
    psum = jnp.zeros((ncp, tq), F32)
    o_cmp = []
    for hh in range(NSA_HG):
        z = st[:, hh * tq:(hh + 1) * tq] + bc_sc[hh] + madd_c
        m = jnp.max(z, axis=0, keepdims=True)
        e = jnp.exp2(z - m)
        o_aug = _dot(vct, e.astype(BF16))
        r = jnp.where(m > 0.5 * NEG, 1.0 / jnp.maximum(o_aug[NSA_DH:NSA_DH + 1], 1e-30), 0.0)
        psum = psum + e * r
        o_cmp.append(o_aug[0:NSA_DH] * r)

    ovt = ovt_ref[...]
    p1 = psum.astype(BF16)
    r1 = psum - p1.astype(F32)
    p2 = r1.astype(BF16)
    p3 = (r1 - p2.astype(F32)).astype(BF16)
    p_slc = _dot(ovt, p1) + _dot(ovt, p2) + _dot(ovt, p3)
    nsp = ovt.shape[0]
    blk = lax.broadcasted_iota(jnp.int32, (nsp, tq), 0)
    cur = (t0 + lax.broadcasted_iota(jnp.int32, (nsp, tq), 1)) // SEL_BLOCK
    valid = blk <= cur
    forced = (blk == 0) | (blk >= cur - 1)
    score = jnp.where(valid, p_slc + jnp.where(forced, FORCE_BONUS, 0.0), -1.0)
    for _ in range(min(SEL_TOPK, seq // SEL_BLOCK)):
        m = jnp.max(score, axis=0, keepdims=True)
        first = jnp.min(jnp.where(score == m, blk, nsp), axis=0, keepdims=True)
        score = jnp.where(blk == first, -1.0, score)
    sel = jnp.where(valid & (score < 0.0), 1.0, 0.0)
    sel_ref[0, 0] = sel
    per_blk = _dot_nt(jnp.ones((8, tq), BF16), sel.astype(BF16))
    cnt_ref[0, 0, 0] = _dot(per_blk.astype(BF16), gm_ref[...])

    k_start = pl.multiple_of(jnp.clip(t0 - WINDOW, 0, seq - wk), LANE)
    kw = kw_ref[0, pl.ds(k_start, wk), :]
    vwt = _with_ones(vwt_ref[0, :, pl.ds(k_start, wk)])
    st = _dot_nt(kw, qst)
    t_w = t0 + lax.broadcasted_iota(jnp.int32, (wk, tq), 1)
    dist_i = t_w - (k_start + lax.broadcasted_iota(jnp.int32, (wk, tq), 0))
    madd_w = jnp.where((dist_i >= 0) & (dist_i < WINDOW), 0.0, NEG)
    for hh in range(NSA_HG):
        z = st[:, hh * tq:(hh + 1) * tq] + bw_sc[hh] + madd_w
        e = jnp.exp2(z - jnp.max(z, axis=0, keepdims=True))
        o_aug = _dot(vwt, e.astype(BF16))
        o_win = o_aug[0:NSA_DH] * (1.0 / jnp.maximum(o_aug[NSA_DH:NSA_DH + 1], 1e-30))
        ocw_ref[0, 0, NSA_DH * hh:NSA_DH * (hh + 1), :] = (
            gates[hh:hh + 1] * o_cmp[hh] + gates[2 * NSA_HG + hh:2 * NSA_HG + hh + 1] * o_win)


def _nsa_cw(q, kcd, vct, kwd, vwt, gt, ovt, gmat, tq):
    B, S, _ = q.shape
    ncp = kcd.shape[2]
    nsp = ovt.shape[0]
    nq = S // tq
    return pl.pallas_call(
        functools.partial(_nsa_cw_body, tq=tq, seq=S),
        grid=(B, NSA_GROUPS, nq),
        in_specs=[pl.BlockSpec((1, tq, 2 * LANE), lambda b, g, i: (b, i, g)),
                  pl.BlockSpec((1, 1, ncp, LANE), lambda b, g, i: (b, g, 0, 0)),
                  pl.BlockSpec((1, 1, NSA_DH, ncp), lambda b, g, i: (b, g, 0, 0)),
                  pl.BlockSpec((1, S, LANE), lambda b, g, i: (b, 0, g)),
                  pl.BlockSpec((1, NSA_DH, S), lambda b, g, i: (b, g, 0)),
                  pl.BlockSpec((1, GATE_ROWS, tq), lambda b, g, i: (b, g, i)),
                  pl.BlockSpec(ovt.shape, lambda b, g, i: (0, 0)),
                  pl.BlockSpec(gmat.shape, lambda b, g, i: (0, 0))],
        out_specs=[pl.BlockSpec((1, 1, NSA_HG * NSA_DH, tq), lambda b, g, i: (b, g, 0, i)),
                   pl.BlockSpec((1, 1, nsp, tq), lambda b, g, i: (b, g, 0, i)),
                   pl.BlockSpec((1, 1, 1, 8, LANE), lambda b, g, i: (b, g, i, 0, 0))],
        out_shape=[jax.ShapeDtypeStruct((B, NSA_GROUPS, NSA_HG * NSA_DH, S), F32),
                   jax.ShapeDtypeStruct((B, NSA_GROUPS, nsp, S), F32),
                   jax.ShapeDtypeStruct((B, NSA_GROUPS, nq, 8, LANE), F32)],
        scratch_shapes=[pltpu.VMEM((NSA_HG, ncp, tq), F32), pltpu.VMEM((NSA_HG, WINDOW + tq, tq), F32)],
        compiler_params=_params(("arbitrary", "arbitrary", "arbitrary")),
        name="nsa_cw",
    )(q, kcd, vct, kwd, vwt, gt, ovt, gmat)


def _nsa_sel_body(flag_ref, q_ref, ks_ref, vst_ref, sel_ref, gt_ref, ocw_ref, o_ref,
                  m_sc, acc_sc, skb_sc, qst_sc, sa_sc, sb_sc, lst_sc, *, tq, tk, n_tiles):
    g = pl.program_id(1)
    i = pl.program_id(2)
    t0 = i * tq
    j_diag = t0 // tk

    @pl.when(i == 0)
    def _():
        key_f = lax.broadcasted_iota(jnp.int32, (tk, tq), 0).astype(F32)
        for hh in range(NSA_HG):
            skb_sc[hh] = _slope(g, hh) * key_f

    qst_sc[...] = _stack_heads(q_ref, tq)
    m_sc[...] = jnp.full(m_sc.shape, NEG, F32)
    acc_sc[...] = jnp.zeros(acc_sc.shape, F32)

    base = ((pl.program_id(0) * NSA_GROUPS + g) * pl.num_programs(2) + i) * n_tiles
    n_act = jnp.int32(0)
    for j in range(n_tiles):
        lst_sc[n_act] = j
        n_act = n_act + jnp.where((flag_ref[base + j] > 0) & (j < j_diag), 1, 0)
    lst_sc[n_act] = j_diag

    def scores(j, s_ref):
        k0 = pl.multiple_of(j * tk, tk)
        s_ref[...] = _dot_nt(ks_ref[0, pl.ds(k0, tk), :], qst_sc[...])

    def consume(j, s_ref, diagonal):
        k0 = pl.multiple_of(j * tk, tk)
        vt = _with_ones(vst_ref[0, :, pl.ds(k0, tk)])
        b0 = j * (tk // SEL_BLOCK)
        pieces = []
        for r in range(tk // SEL_BLOCK):
            chosen = sel_ref[0, 0, pl.ds(b0 + r, 1), :]
            pieces.append(jnp.broadcast_to(jnp.where(chosen > 0.5, 0.0, NEG), (SEL_BLOCK, tq)))
        madd = jnp.concatenate(pieces, axis=0)
        if diagonal:
            key_i = lax.broadcasted_iota(jnp.int32, (tk, tq), 0)
            qry_i = lax.broadcasted_iota(jnp.int32, (tk, tq), 1)
            madd = jnp.where(key_i + k0 <= qry_i + t0, madd, NEG)
        for hh in range(NSA_HG):
            c = _slope(g, hh) * (k0 - t0).astype(F32)
            z = s_ref[:, hh * tq:(hh + 1) * tq] + skb_sc[hh] + madd
            m_prev = m_sc[hh]
            m_new = jnp.maximum(m_prev, jnp.max(z, axis=0, keepdims=True) + c)
            alpha = jnp.exp2(m_prev - m_new)
            p = jnp.exp2(z - (m_new - c))
            acc_sc[hh] = alpha * acc_sc[hh] + _dot(vt, p.astype(BF16))
            m_sc[hh] = m_new

    scores(lst_sc[0], sa_sc)

    def pair(jj, carry):
        it = 2 * jj
        scores(lst_sc[it + 1], sb_sc)
        consume(lst_sc[it], sa_sc, False)
        scores(lst_sc[it + 2], sa_sc)
        consume(lst_sc[it + 1], sb_sc, False)
        return carry

    lax.fori_loop(0, n_act // 2, pair, 0)

    @pl.when(n_act % 2 == 0)
    def _():
        consume(j_diag, sa_sc, True)

    @pl.when(n_act % 2 == 1)
    def _():
        scores(j_diag, sb_sc)
        consume(lst_sc[n_act - 1], sa_sc, False)
        consume(j_diag, sb_sc, True)

    gates = gt_ref[0]
    for pair_i in range(NSA_HG // 2):
        halves = []
        for hh in (2 * pair_i, 2 * pair_i + 1):
            acc = acc_sc[hh]
            w = gates[NSA_HG + hh:NSA_HG + hh + 1] / jnp.maximum(acc[NSA_DH:NSA_DH + 1], 1e-30)
            halves.append(acc[0:NSA_DH] * w + ocw_ref[0, 0, NSA_DH * hh:NSA_DH * (hh + 1), :])
        o_ref[0, :, LANE * pair_i:LANE * (pair_i + 1)] = jnp.concatenate(halves, axis=0).T.astype(BF16)


def _nsa_sel(flags, q, ksd, vst, sel, gt, ocw, tq, tk):
    B, S, _ = q.shape
    nsp = sel.shape[2]
    n_tiles = S // tk
    qs = pl.BlockSpec((1, tq, 2 * LANE), lambda b, g, i, f: (b, i, g))
    return pl.pallas_call(
        functools.partial(_nsa_sel_body, tq=tq, tk=tk, n_tiles=n_tiles),
        grid_spec=pltpu.PrefetchScalarGridSpec(
            num_scalar_prefetch=1,
            grid=(B, NSA_GROUPS, S // tq),
            in_specs=[qs,
                      pl.BlockSpec((1, S, LANE), lambda b, g, i, f: (b, 0, g)),
                      pl.BlockSpec((1, NSA_DH, S), lambda b, g, i, f: (b, g, 0)),
                      pl.BlockSpec((1, 1, nsp, tq), lambda b, g, i, f: (b, g, 0, i)),
                      pl.BlockSpec((1, GATE_ROWS, tq), lambda b, g, i, f: (b, g, i)),
                      pl.BlockSpec((1, 1, NSA_HG * NSA_DH, tq), lambda b, g, i, f: (b, g, 0, i))],
            out_specs=qs,
            scratch_shapes=[pltpu.VMEM((NSA_HG, 1, tq), F32),
                            pltpu.VMEM((NSA_HG, NSA_DH + ONES_ROWS, tq), F32),
                            pltpu.VMEM((NSA_HG, tk, tq), F32),
                            pltpu.VMEM((NSA_HG * tq, LANE), BF16),
                            pltpu.VMEM((tk, NSA_HG * tq), F32),
                            pltpu.VMEM((tk, NSA_HG * tq), F32),
                            pltpu.SMEM((n_tiles + 1,), jnp.int32)]),
        out_shape=jax.ShapeDtypeStruct((B, S, 4 * LANE), BF16),
        compiler_params=_params(("arbitrary", "arbitrary", "arbitrary")),
        name="nsa_sel",
    )(flags, q, ksd, vst, sel, gt, ocw)


def _mla_body(q_ref, k_ref, vt_ref, o_ref, m_sc, acc_sc, sa_sc, sb_sc, *, tq, tc):
    n_full = pl.program_id(2)
    m_sc[...] = jnp.full(m_sc.shape, NEG, F32)
    acc_sc[...] = jnp.zeros(acc_sc.shape, F32)

    def scores(j, s_ref):
        k0 = pl.multiple_of(j * tq, tq)
        for hh in range(2):
            lanes = slice(LANE * hh, LANE * (hh + 1))
            s_ref[hh] = _dot_nt(k_ref[0, pl.ds(k0, tq), lanes], q_ref[0, :, lanes])

    def consume(j, s_ref, diagonal):
        k0 = pl.multiple_of(j * tq, tq)
        for hh in range(2):
            vt = _with_ones(vt_ref[0, MLA_V * hh:MLA_V * (hh + 1), pl.ds(k0, tq)])
            for c in range(tq // tc):
                cols = slice(c * tc, (c + 1) * tc)
                st = s_ref[hh, :, cols]
                if diagonal:
                    causal = (lax.broadcasted_iota(jnp.int32, (tq, tc), 0)
                              <= lax.broadcasted_iota(jnp.int32, (tq, tc), 1) + c * tc)
                    st = jnp.where(causal, st, NEG)
                m_prev = m_sc[hh, :, cols]
                m_new = jnp.maximum(m_prev, jnp.max(st, axis=0, keepdims=True))
                alpha = jnp.exp2(m_prev - m_new)
                p = jnp.exp2(st - m_new)
                acc_sc[hh, :, cols] = alpha * acc_sc[hh, :, cols] + _dot(vt, p.astype(BF16))
                m_sc[hh, :, cols] = m_new

    scores(0, sa_sc)

    def pair(jj, carry):
        j = 2 * jj
        scores(j + 1, sb_sc)
        consume(j, sa_sc, False)
        scores(j + 2, sa_sc)
        consume(j + 1, sb_sc, False)
        return carry

    lax.fori_loop(0, n_full // 2, pair, 0)

    @pl.when(n_full % 2 == 0)
    def _():
        consume(n_full, sa_sc, True)

    @pl.when(n_full % 2 == 1)
    def _():
        scores(n_full, sb_sc)
        consume(n_full - 1, sa_sc, False)
        consume(n_full, sb_sc, True)

    halves = []
    for hh in range(2):
        acc = acc_sc[hh]
        halves.append(acc[0:MLA_V] * (1.0 / jnp.maximum(acc[MLA_V:MLA_V + 1], 1e-30)))
    o_ref[0] = jnp.concatenate(halves, axis=0).T.astype(BF16)


def _mla(qm, km, vmt, tq):
    B, S, _ = qm.shape
    return pl.pallas_call(
        functools.partial(_mla_body, tq=tq, tc=min(256, tq)),
        grid=(B, MLA_HEADS // 2, S // tq),
        in_specs=[pl.BlockSpec((1, tq, 2 * LANE), lambda b, p, i: (b, i, p)),
                  pl.BlockSpec((1, S, 2 * LANE), lambda b, p, i: (b, 0, p)),
                  pl.BlockSpec((1, 2 * MLA_V, S), lambda b, p, i: (b, p, 0))],
        out_specs=pl.BlockSpec((1, tq, LANE), lambda b, p, i: (b, i, p)),
        out_shape=jax.ShapeDtypeStruct((B, S, MLA_HEADS * MLA_V), BF16),
        scratch_shapes=[pltpu.VMEM((2, 1, tq), F32), pltpu.VMEM((2, MLA_V + ONES_ROWS, tq), F32),
                        pltpu.VMEM((2, tq, tq), F32), pltpu.VMEM((2, tq, tq), F32)],
        compiler_params=_params(("parallel", "parallel", "arbitrary")),
        name="mla",
    )(qm, km, vmt)


def _merge_body(on_ref, om_ref, ga_ref, gb_ref, x_ref, mod_ref, won_ref, wom_ref, wout_ref, o_ref):
    y = (ga_ref[0].astype(F32) * _dot(on_ref[0], won_ref[...])
         + gb_ref[0].astype(F32) * _dot(om_ref[0], wom_ref[...]))
    o_ref[0] = x_ref[0] + mod_ref[0, 2:3, :] * _dot(y.astype(BF16), wout_ref[...])


def _merge(o_nsa, o_mla, ga, gb, x, mod, w_on, w_om, w_out, tm):
    B, S, D = x.shape
    row = lambda n: pl.BlockSpec((1, tm, n), lambda b, i: (b, i, 0))
    full = lambda a: pl.BlockSpec(a.shape, lambda b, i: (0,) * a.ndim)
    return pl.pallas_call(
        _merge_body,
        grid=(B, S // tm),
        in_specs=[row(o_nsa.shape[2]), row(o_mla.shape[2]), row(D), row(D), row(D),
                  pl.BlockSpec((1, 6, D), lambda b, i: (b, 0, 0)), full(w_on), full(w_om), full(w_out)],
        out_specs=row(D),
        out_shape=jax.ShapeDtypeStruct((B, S, D), F32),
        compiler_params=_params(("parallel", "parallel")),
        name="merge",
    )(o_nsa, o_mla, ga, gb, x, mod, w_on, w_om, w_out)


def _mlp_body(x_ref, mod_ref, gmlp_ref, w1_ref, w2_ref, gfin_ref, o_ref, *, fc, final):
    x1 = x_ref[0]
    h = _rms(x1, gmlp_ref[...]) * (1.0 + mod_ref[0, 4:5, :]) + mod_ref[0, 3:4, :]
    hb = h.astype(BF16)
    acc = jnp.zeros(x1.shape, F32)
    for c in range(w1_ref.shape[1] // fc):
        a = jnp.maximum(_dot(hb, w1_ref[:, c * fc:(c + 1) * fc]), 0.0)
        acc = acc + _dot((a * a).astype(BF16), w2_ref[c * fc:(c + 1) * fc, :])
    x2 = x1 + mod_ref[0, 5:6, :] * acc
    o_ref[0] = _rms(x2, gfin_ref[...]) if final else x2


def _mlp(x1, mod, g_mlp, w1, w2, g_fin, tm, fc, final):
    B, S, D = x1.shape
    row = pl.BlockSpec((1, tm, D), lambda b, i: (b, i, 0))
    full = lambda a: pl.BlockSpec(a.shape, lambda b, i: (0,) * a.ndim)
    return pl.pallas_call(
        functools.partial(_mlp_body, fc=fc, final=final),
        grid=(B, S // tm),
        in_specs=[row, pl.BlockSpec((1, 6, D), lambda b, i: (b, 0, 0)), full(g_mlp), full(w1), full(w2),
                  full(g_fin)],
        out_specs=row,
        out_shape=jax.ShapeDtypeStruct((B, S, D), F32),
        compiler_params=_params(("parallel", "parallel")),
        name="mlp",
    )(x1, mod, g_mlp, w1, w2, g_fin)


def _dup_groups(w):
    g0, g1 = w[:, :NSA_DH], w[:, NSA_DH:]
    return jnp.concatenate([g0, g0, g1, g1], axis=1)


def _inproj_weight(w_in):
    d = w_in.shape[0]
    o = 0
    parts = {}
    for name, n in (("q", 512), ("kc", 128), ("vc", 128), ("ks", 128), ("vs", 128), ("kw", 128),
                    ("vw", 128), ("g", 24), ("qd", MLA_Q_RANK), ("kvd", MLA_KV_RANK), ("kr", MLA_ROPE),
                    ("ga", D_MODEL), ("gb", D_MODEL)):
        parts[name] = w_in[:, o:o + n]
        o += n
    wg = parts["g"].reshape(d, 3, NSA_GROUPS, NSA_HG).transpose(0, 2, 1, 3).reshape(d, NSA_GROUPS, 3 * NSA_HG)
    wg = jnp.pad(wg, ((0, 0), (0, 0), (0, GATE_ROWS - 3 * NSA_HG))).reshape(d, NSA_GROUPS * GATE_ROWS)
    wg = jnp.pad(wg, ((0, 0), (0, LANE - NSA_GROUPS * GATE_ROWS)))
    half = MLA_ROPE // 2
    x1, x2 = parts["kr"][:, :half], parts["kr"][:, half:]
    z_lo = jnp.zeros((d, MLA_NOPE), F32)
    z_hi = jnp.zeros((d, LANE - MLA_NOPE - MLA_ROPE), F32)
    kra = jnp.concatenate([z_lo, x1, x2, z_hi], axis=1)
    krb = jnp.concatenate([z_lo, -x2, x1, z_hi], axis=1)
    w_all = jnp.concatenate(
        [parts["q"], parts["kc"], parts["vc"], _dup_groups(parts["ks"]), parts["vs"],
         _dup_groups(parts["kw"]), parts["vw"], wg, parts["qd"], parts["kvd"], kra, krb,
         parts["ga"], parts["gb"]], axis=1)
    assert w_all.shape[1] == IN_COLS
    return w_all.astype(BF16)


def _mla_q_weight(w_uq):
    r = w_uq.shape[0]
    w = w_uq.reshape(r, MLA_HEADS, MLA_NOPE + MLA_ROPE)
    half = MLA_ROPE // 2
    nope, x1, x2 = w[..., :MLA_NOPE], w[..., MLA_NOPE:MLA_NOPE + half], w[..., MLA_NOPE + half:]
    z_hi = jnp.zeros((r, MLA_HEADS, LANE - MLA_NOPE - MLA_ROPE), F32)
    a = jnp.concatenate([nope, x1, x2, z_hi], axis=-1).reshape(r, MLA_HEADS * LANE)
    b = jnp.concatenate([jnp.zeros_like(nope), -x2, x1, z_hi], axis=-1).reshape(r, MLA_HEADS * LANE)
    return jnp.concatenate([a, b], axis=1).astype(BF16)


def _mla_kv_weight(w_uk, w_uv):
    r = w_uk.shape[0]
    k = w_uk.reshape(r, MLA_HEADS, MLA_NOPE)
    k = jnp.pad(k, ((0, 0), (0, 0), (0, LANE - MLA_NOPE))).reshape(r, MLA_HEADS * LANE)
    return jnp.concatenate([k, w_uv], axis=1).astype(BF16)


def _rope_tables(seq):
    half = MLA_ROPE // 2
    pos = jnp.arange(seq, dtype=F32)
    inv_freq = ROPE_THETA ** (-jnp.arange(half, dtype=F32) / half)
    ang = pos[:, None] * inv_freq[None, :]
    cos, sin = jnp.cos(ang), jnp.sin(ang)
    ones = jnp.ones((seq, MLA_NOPE), F32)
    z_lo = jnp.zeros((seq, MLA_NOPE), F32)
    z_hi = jnp.zeros((seq, LANE - MLA_NOPE - MLA_ROPE), F32)
    return (jnp.concatenate([ones, cos, cos, z_hi], axis=1),
            jnp.concatenate([z_lo, sin, sin, z_hi], axis=1))


def _compress_weights(pe, w1, w2):
    hid = w1.shape[1]
    wa = w1[:CMP_STRIDE * NSA_DH].reshape(CMP_STRIDE, NSA_DH, hid)
    wb = w1[CMP_STRIDE * NSA_DH:].reshape(CMP_STRIDE, NSA_DH, hid)
    z = jnp.zeros_like(wa)
    cols = []
    for g in range(NSA_GROUPS):
        for w in (wa, wb):
            slots = [z, z]
            slots[g] = w
            cols.append(jnp.stack(slots, axis=1).reshape(CMP_STRIDE * NSA_GROUPS * NSA_DH, hid))
    w1big = jnp.concatenate(cols, axis=1).astype(BF16)
    pe8 = jnp.zeros((8, CMP_LEN * NSA_DH), F32).at[0].set(pe.reshape(-1))
    w2d = jnp.concatenate([w2, w2], axis=1).astype(BF16)
    return w1big, pe8, w1, w2d


def _overlap_table(seq):
    ns = seq // SEL_BLOCK
    nsp = -(-ns // LANE) * LANE
    ncp = seq // CMP_STRIDE
    s_id = jnp.arange(nsp)[:, None]
    c_id = jnp.arange(ncp)[None, :]
    c_start = c_id * CMP_STRIDE
    return ((c_start < s_id * SEL_BLOCK + SEL_BLOCK) & (c_start + CMP_LEN - 1 >= s_id * SEL_BLOCK)
            & (c_id < ncp - 1)).astype(BF16)


def kernel(x, c, w_ada, b_ada, g_mix, w_in, pe_ck, w_ck1, w_ck2, pe_cv, w_cv1, w_cv2, g_cq, w_uq, g_ckv,
           w_uk, w_uv, w_o_nsa, w_o_mla, w_out, g_mlp, w_fc1, w_fc2, g_final):
    B, S, D = x.shape
    depth = w_ada.shape[0]
    tm_in = min(256, S)
    tm_out = min(512, S)
    tq_nsa = min(256, S)
    tk_sel = min(512, S)
    tq_mla = min(512, S)
    cs, sn = _rope_tables(S)
    ovt = _overlap_table(S)
    gmat = (jnp.arange(ovt.shape[0])[:, None] // (tk_sel // SEL_BLOCK) == jnp.arange(LANE)[None, :]).astype(BF16)
    n_chunks = S // CMP_STRIDE
    for layer in range(depth):
        mod = _ada(c, w_ada[layer], b_ada[layer])
        (q, kc_in, vc_in, ksd, vst, kwd, vwt, gt, qm, km, vmt, ga, gb) = _inproj(
            x, mod, g_mix[layer][None], _inproj_weight(w_in[layer]), g_cq[layer][None],
            _mla_q_weight(w_uq[layer]), g_ckv[layer][None], _mla_kv_weight(w_uk[layer], w_uv[layer]),
            cs, sn, tm_in)
        w1k, pek, w1kf, w2k = _compress_weights(pe_ck[layer], w_ck1[layer], w_ck2[layer])
        w1v, pev, w1vf, w2v = _compress_weights(pe_cv[layer], w_cv1[layer], w_cv2[layer])
        width = CMP_STRIDE * NSA_GROUPS * NSA_DH
        kcd, vct = _compress(kc_in.reshape(B, n_chunks, width), vc_in.reshape(B, n_chunks, width),
                             w1k, w1v, pek, pev, w1kf, w1vf, w2k, w2v)
        ocw, sel, cnt = _nsa_cw(q, kcd, vct, kwd, vwt, gt, ovt, gmat, tq_nsa)
        flags = (cnt[:, :, :, 0, :S // tk_sel] > 0.5).astype(jnp.int32).reshape(-1)
        o_nsa = _nsa_sel(flags, q, ksd, vst, sel, gt, ocw, tq_nsa, tk_sel)
        o_mla = _mla(qm, km, vmt, tq_mla)
        x1 = _merge(o_nsa, o_mla, ga, gb, x, mod, w_o_nsa[layer].astype(BF16), w_o_mla[layer].astype(BF16),
                    w_out[layer].astype(BF16), tm_out)
        x = _mlp(x1, mod, g_mlp[layer][None], w_fc1[layer].astype(BF16), w_fc2[layer].astype(BF16),
                 g_final[None], tm_out, 1024, layer == depth - 1)
    return x
```

```python
import functools
import math

import jax
import jax.numpy as jnp
from jax import lax
from jax.experimental import pallas as pl
from jax.experimental.pallas import tpu as pltpu

F32 = jnp.float32
BF16 = jnp.bfloat16

D_MODEL = 1024
NSA_GROUPS = 2
NSA_HG = 4
NSA_DH = 64
CMP_LEN = 32
CMP_STRIDE = 16
CMP_HIDDEN = 256
SEL_BLOCK = 64
SEL_TOPK = 16
WINDOW = 512
MLA_HEADS = 8
MLA_NOPE = 64
MLA_ROPE = 32
MLA_V = 64
MLA_Q_RANK = 384
MLA_KV_RANK = 256
ROPE_THETA = 10000.0
D_FF = 4 * D_MODEL
EPS = 1e-6
FORCE_BONUS = 1000.0

LANE = 128
HALF = LANE // 2
NEG = -1e30
LOG2E = math.log2(math.e)
VMEM_LIMIT = 52 * 1024 * 1024
GATE_ROWS = 16
ONES_ROWS = 16

OFF_Q = 0
OFF_KC = 512
OFF_VC = 640
OFF_KS = 768
OFF_VS = 1024
OFF_KW = 1152
OFF_VW = 1408
OFF_G = 1536
OFF_QD = 1664
OFF_KVD = 2048
OFF_KRA = 2304
OFF_KRB = 2432
OFF_GA = 2560
OFF_GB = 3584
IN_COLS = 4608


def _dot(a, b):
    return jnp.dot(a, b, preferred_element_type=F32)


def _dot_nt(a, b):
    return lax.dot_general(a, b, (((1,), (1,)), ((), ())), preferred_element_type=F32)


def _rms(x, g):
    return x * lax.rsqrt(jnp.mean(x * x, axis=-1, keepdims=True) + EPS) * g


def _sigmoid(x):
    return 1.0 / (1.0 + jnp.exp(-x))


def _params(sem):
    return pltpu.CompilerParams(dimension_semantics=sem, vmem_limit_bytes=VMEM_LIMIT)


def _ada_body(c_ref, w_ref, b_ref, o_ref):
    o_ref[...] = jnp.dot(c_ref[...], w_ref[...], preferred_element_type=F32,
                         precision=lax.Precision.HIGHEST) + b_ref[...]


def _ada(c, w, b):
    nb, d = c.shape
    n = w.shape[1]
    rows = 8
    cp = jnp.zeros((rows, d), F32).at[:nb].set(c)
    tn = 1536
    out = pl.pallas_call(
        _ada_body,
        grid=(n // tn,),
        in_specs=[pl.BlockSpec((rows, d), lambda j: (0, 0)),
                  pl.BlockSpec((d, tn), lambda j: (0, j)),
                  pl.BlockSpec((1, tn), lambda j: (0, j))],
        out_specs=pl.BlockSpec((rows, tn), lambda j: (0, j)),
        out_shape=jax.ShapeDtypeStruct((rows, n), F32),
        compiler_params=_params(("arbitrary",)),
        name="ada",
    )(cp, w, b.reshape(1, n))
    return out[:nb].reshape(nb, 6, d)


def _inproj_body(x_ref, mod_ref, gmix_ref, w_ref, gcq_ref, wq2_ref, gckv_ref, wkv_ref,
                 cs_ref, sn_ref,
                 q_ref, kc_ref, vc_ref, ks_ref, vst_ref, kw_ref, vwt_ref, gt_ref,
                 qm_ref, km_ref, vmt_ref, ga_ref, gb_ref):
    x = x_ref[0]
    h = _rms(x, gmix_ref[...]) * (1.0 + mod_ref[0, 1:2, :]) + mod_ref[0, 0:1, :]
    hb = h.astype(BF16)

    def proj(lo, n):
        return _dot(hb, w_ref[:, lo:lo + n])

    q_ref[0] = (proj(OFF_Q, 512) * (NSA_DH ** -0.5 * LOG2E)).astype(BF16)
    kc_ref[0] = proj(OFF_KC, 128).astype(BF16)
    vc_ref[0] = proj(OFF_VC, 128).astype(BF16)
    ks_ref[0] = proj(OFF_KS, 256).astype(BF16)
    kw_ref[0] = proj(OFF_KW, 256).astype(BF16)
    vst_ref[0] = proj(OFF_VS, LANE).T.astype(BF16)
    vwt_ref[0] = proj(OFF_VW, LANE).T.astype(BF16)
    gt_ref[0] = _sigmoid(proj(OFF_G, LANE)).T[0:NSA_GROUPS * GATE_ROWS]
    ga_ref[0] = _sigmoid(proj(OFF_GA, 1024)).astype(BF16)
    gb_ref[0] = _sigmoid(proj(OFF_GB, 1024)).astype(BF16)

    cs = cs_ref[...]
    sn = sn_ref[...]
    nh = MLA_HEADS * LANE
    cq = _rms(proj(OFF_QD, MLA_Q_RANK), gcq_ref[...]).astype(BF16)
    q2 = _dot(cq, wq2_ref[...])
    scale = (MLA_NOPE + MLA_ROPE) ** -0.5 * LOG2E
    for hh in range(MLA_HEADS):
        a = q2[:, LANE * hh:LANE * (hh + 1)]
        b = q2[:, nh + LANE * hh:nh + LANE * (hh + 1)]
        qm_ref[0, :, LANE * hh:LANE * (hh + 1)] = ((a * cs + b * sn) * scale).astype(BF16)
    ckv = _rms(proj(OFF_KVD, MLA_KV_RANK), gckv_ref[...]).astype(BF16)
    kv = _dot(ckv, wkv_ref[...])
    kr = proj(OFF_KRA, LANE) * cs + proj(OFF_KRB, LANE) * sn
    for hh in range(MLA_HEADS):
        km_ref[0, :, LANE * hh:LANE * (hh + 1)] = (kv[:, LANE * hh:LANE * (hh + 1)] + kr).astype(BF16)
    vmt_ref[0] = kv[:, nh:].T.astype(BF16)


def _inproj(x, mod, g_mix, w_all, g_cq, wq2, g_ckv, wkv, cs, sn, tm):
    B, S, D = x.shape
    row = lambda n: pl.BlockSpec((1, tm, n), lambda b, i: (b, i, 0))
    col = lambda n: pl.BlockSpec((1, n, tm), lambda b, i: (b, 0, i))
    full = lambda a: pl.BlockSpec(a.shape, lambda b, i: (0,) * a.ndim)
    tok = lambda n, dt: (row(n), jax.ShapeDtypeStruct((B, S, n), dt))
    tr = lambda n, dt: (col(n), jax.ShapeDtypeStruct((B, n, S), dt))
    outs = [tok(512, BF16), tok(128, BF16), tok(128, BF16), tok(256, BF16), tr(LANE, BF16),
            tok(256, BF16), tr(LANE, BF16), tr(NSA_GROUPS * GATE_ROWS, F32),
            tok(1024, BF16), tok(1024, BF16), tr(MLA_HEADS * MLA_V, BF16), tok(1024, BF16), tok(1024, BF16)]
    return pl.pallas_call(
        _inproj_body,
        grid=(B, S // tm),
        in_specs=[row(D),
                  pl.BlockSpec((1, 6, D), lambda b, i: (b, 0, 0)),
                  full(g_mix), full(w_all), full(g_cq), full(wq2), full(g_ckv), full(wkv),
                  pl.BlockSpec((tm, LANE), lambda b, i: (i, 0)),
                  pl.BlockSpec((tm, LANE), lambda b, i: (i, 0))],
        out_specs=[o[0] for o in outs],
        out_shape=[o[1] for o in outs],
        compiler_params=_params(("parallel", "parallel")),
        name="inproj",
    )(x, mod, g_mix, w_all, g_cq, wq2, g_ckv, wkv, cs, sn)


def _compress_one(x, w1big, pe, w1f, w2d):
    hcat = _dot(x, w1big)
    const = jnp.dot(pe, w1f, preferred_element_type=F32, precision=lax.Precision.HIGHEST)[0:1]
    n = x.shape[0]
    rowid = lax.broadcasted_iota(jnp.int32, (n, LANE), 0)
    hid_w = CMP_HIDDEN
    outs = []
    for g in range(NSA_GROUPS):
        a = hcat[:, 2 * hid_w * g:2 * hid_w * g + hid_w]
        b = hcat[:, 2 * hid_w * g + hid_w:2 * hid_w * (g + 1)]
        hid = a + pltpu.roll(b, n - 1, 0) + const
        act = hid * _sigmoid(hid)
        out = _dot(act.astype(BF16), w2d)
        outs.append(jnp.where(rowid < n - 1, out, 0.0))
    return outs


def _compress_body(xk_ref, xv_ref, w1k_ref, w1v_ref, pek_ref, pev_ref, w1kf_ref, w1vf_ref,
                   w2k_ref, w2v_ref, kc_ref, vct_ref):
    kc = _compress_one(xk_ref[0], w1k_ref[...], pek_ref[...], w1kf_ref[...], w2k_ref[...])
    vc = _compress_one(xv_ref[0], w1v_ref[...], pev_ref[...], w1vf_ref[...], w2v_ref[...])
    for g in range(NSA_GROUPS):
        kc_ref[0, g] = kc[g].astype(BF16)
        vct_ref[0, g] = vc[g].T[0:NSA_DH].astype(BF16)


def _compress(xk, xv, w1k, w1v, pek, pev, w1kf, w1vf, w2k, w2v):
    B, n, width = xk.shape
    full = lambda a: pl.BlockSpec(a.shape, lambda b: (0,) * a.ndim)
    xs = pl.BlockSpec((1, n, width), lambda b: (b, 0, 0))
    return pl.pallas_call(
        _compress_body,
        grid=(B,),
        in_specs=[xs, xs, full(w1k), full(w1v), full(pek), full(pev), full(w1kf), full(w1vf),
                  full(w2k), full(w2v)],
        out_specs=[pl.BlockSpec((1, NSA_GROUPS, n, LANE), lambda b: (b, 0, 0, 0)),
                   pl.BlockSpec((1, NSA_GROUPS, NSA_DH, n), lambda b: (b, 0, 0, 0))],
        out_shape=[jax.ShapeDtypeStruct((B, NSA_GROUPS, n, LANE), BF16),
                   jax.ShapeDtypeStruct((B, NSA_GROUPS, NSA_DH, n), BF16)],
        compiler_params=_params(("parallel",)),
        name="compress",
    )(xk, xv, w1k, w1v, pek, pev, w1kf, w1vf, w2k, w2v)


def _stack_heads(q_ref, tq):
    upper = lax.broadcasted_iota(jnp.int32, (tq, LANE), 1) >= HALF
    parts = []
    for hh in range(NSA_HG):
        slab = q_ref[0, :, LANE * (hh // 2):LANE * (hh // 2 + 1)]
        keep = upper if hh % 2 else jnp.logical_not(upper)
        parts.append(jnp.where(keep, slab, jnp.zeros_like(slab)))
    return jnp.concatenate(parts, axis=0)


def _slope(g, hh):
    return jnp.where(g == 0, 1.0, 2.0 ** -NSA_HG).astype(F32) * (2.0 ** -(hh + 1) * LOG2E)


def _with_ones(vt):
    return jnp.concatenate([vt, jnp.ones((ONES_ROWS, vt.shape[1]), vt.dtype)], axis=0)


def _nsa_cw_body(q_ref, kc_ref, vct_ref, kw_ref, vwt_ref, gt_ref, ovt_ref, gm_ref,
                 ocw_ref, sel_ref, cnt_ref, bc_sc, bw_sc, qst_sc, *, tq, tw, seq, n_parts):
    g = pl.program_id(1)
    i = pl.program_id(2)
    t0 = i * tq
    qst_sc[...] = _stack_heads(q_ref, tq)
    gates = gt_ref[0]
    ncp = kc_ref.shape[2]
    nsp = ovt_ref.shape[0]
    wk = WINDOW + tw

    @pl.when(i == 0)
    def _():
        c_end_f = (lax.broadcasted_iota(jnp.int32, (ncp, tq), 0) * CMP_STRIDE + (CMP_LEN - 1)).astype(F32)
        key_f = lax.broadcasted_iota(jnp.int32, (wk, tw), 0).astype(F32)
        for hh in range(NSA_HG):
            bc_sc[hh] = _slope(g, hh) * c_end_f
            bw_sc[hh] = _slope(g, hh) * key_f

    def tile_body(ncp_e, nsp_e):
        vct = _with_ones(vct_ref[0, 0, :, 0:ncp_e])
        st = _dot_nt(kc_ref[0, 0, 0:ncp_e, :], qst_sc[...])
        t_c = t0 + lax.broadcasted_iota(jnp.int32, (ncp_e, tq), 1)
        c_end = lax.broadcasted_iota(jnp.int32, (ncp_e, tq), 0) * CMP_STRIDE + (CMP_LEN - 1)
        madd_c = jnp.where(t_c >= c_end, 0.0, NEG)
        psum = jnp.zeros((ncp_e, tq), F32)
        o_cmp = []
        for hh in range(NSA_HG):
            z = st[:, hh * tq:(hh + 1) * tq] + bc_sc[hh, 0:ncp_e] + madd_c
            m = jnp.max(z, axis=0, keepdims=True)
            e = jnp.exp2(z - m)
            o_aug = _dot(vct, e.astype(BF16))
            r = jnp.where(m > 0.5 * NEG, 1.0 / jnp.maximum(o_aug[NSA_DH:NSA_DH + 1], 1e-30), 0.0)
            psum = psum + e * r
            o_cmp.append(o_aug[0:NSA_DH] * r)

        ovt = ovt_ref[0:nsp_e, 0:ncp_e]
        p1 = psum.astype(BF16)
        r1 = psum - p1.astype(F32)
        p2 = r1.astype(BF16)
        p3 = (r1 - p2.astype(F32)).astype(BF16)
        p_slc = _dot(ovt, p1) + _dot(ovt, p2) + _dot(ovt, p3)

        n_forced = 3
        sels = []
        for c in range(tq // tw):
            blk = lax.broadcasted_iota(jnp.int32, (nsp_e, tw), 0)
            cur = (t0 + c * tw + lax.broadcasted_iota(jnp.int32, (nsp_e, tw), 1)) // SEL_BLOCK
            valid = blk <= cur
            forced = (blk == 0) | (blk >= cur - 1)
            score = jnp.where(valid & jnp.logical_not(forced), p_slc[:, c * tw:(c + 1) * tw], -1.0)
            for _ in range(max(min(SEL_TOPK, seq // SEL_BLOCK) - n_forced, 0)):
                m = jnp.max(score, axis=0, keepdims=True)
                first = jnp.min(jnp.where(score == m, blk, nsp), axis=0, keepdims=True)
                score = jnp.where(blk == first, -1.0, score)
            sels.append(jnp.where(valid & (score < 0.0), 1.0, 0.0))
        sel = jnp.concatenate(sels, axis=1)
        sel_ref[0, 0, 0:nsp_e] = sel
        if nsp_e < nsp:
            sel_ref[0, 0, nsp_e:nsp] = jnp.zeros((nsp - nsp_e, tq), F32)
        per_blk = _dot_nt(jnp.ones((8, tq), BF16), sel.astype(BF16))
        cnt_ref[0, 0, 0] = _dot(per_blk.astype(BF16), gm_ref[0:nsp_e])

        for sub in range(tq // tw):
            k_start = pl.multiple_of(jnp.clip(t0 + sub * tw - WINDOW, 0, seq - wk), LANE)
            kw = kw_ref[0, pl.ds(k_start, wk), :]
            vwt = _with_ones(vwt_ref[0, :, pl.ds(k_start, wk)])
            qsub = jnp.concatenate(
                [qst_sc[hh * tq + sub * tw:hh * tq + (sub + 1) * tw] for hh in range(NSA_HG)], axis=0)
            st = _dot_nt(kw, qsub)
            t_w = t0 + sub * tw + lax.broadcasted_iota(jnp.int32, (wk, tw), 1)
            dist_i = t_w - (k_start + lax.broadcasted_iota(jnp.int32, (wk, tw), 0))
            madd_w = jnp.where((dist_i >= 0) & (dist_i < WINDOW), 0.0, NEG)
            cols = slice(sub * tw, (sub + 1) * tw)
            for hh in range(NSA_HG):
                z = st[:, hh * tw:(hh + 1) * tw] + bw_sc[hh] + madd_w
                e = jnp.exp2(z - jnp.max(z, axis=0, keepdims=True))
                o_aug = _dot(vwt, e.astype(BF16))
                o_win = o_aug[0:NSA_DH] * (1.0 / jnp.maximum(o_aug[NSA_DH:NSA_DH + 1], 1e-30))
                ocw_ref[0, 0, NSA_DH * hh:NSA_DH * (hh + 1), cols] = (
                    gates[hh:hh + 1, cols] * o_cmp[hh][:, cols]
                    + gates[2 * NSA_HG + hh:2 * NSA_HG + hh + 1, cols] * o_win)

    part = (i * n_parts) // pl.num_programs(2)
    for v in range(n_parts):
        pl.when(part == v)(functools.partial(tile_body, ncp * (v + 1) // n_parts, nsp * (v + 1) // n_parts))


def _cw_parts(nq, ncp, nsp):
    for n in (4, 2):
        if nq % n == 0 and (ncp // n) % LANE == 0 and (nsp // n) % 8 == 0:
            return n
    return 1


def _nsa_cw(q, kcd, vct, kwd, vwt, gt, ovt, gmat, tq):
    B, S, _ = q.shape
    ncp = kcd.shape[2]
    nsp = ovt.shape[0]
    nq = S // tq
    tw = min(LANE, tq)
    return pl.pallas_call(
        functools.partial(_nsa_cw_body, tq=tq, tw=tw, seq=S, n_parts=_cw_parts(nq, ncp, nsp)),
        grid=(B, NSA_GROUPS, nq),
        in_specs=[pl.BlockSpec((1, tq, 2 * LANE), lambda b, g, i: (b, i, g)),
                  pl.BlockSpec((1, 1, ncp, LANE), lambda b, g, i: (b, g, 0, 0)),
                  pl.BlockSpec((1, 1, NSA_DH, ncp), lambda b, g, i: (b, g, 0, 0)),
                  pl.BlockSpec((1, S, LANE), lambda b, g, i: (b, 0, g)),
                  pl.BlockSpec((1, NSA_DH, S), lambda b, g, i: (b, g, 0)),
                  pl.BlockSpec((1, GATE_ROWS, tq), lambda b, g, i: (b, g, i)),
                  pl.BlockSpec(ovt.shape, lambda b, g, i: (0, 0)),
                  pl.BlockSpec(gmat.shape, lambda b, g, i: (0, 0))],
        out_specs=[pl.BlockSpec((1, 1, NSA_HG * NSA_DH, tq), lambda b, g, i: (b, g, 0, i)),
                   pl.BlockSpec((1, 1, nsp, tq), lambda b, g, i: (b, g, 0, i)),
                   pl.BlockSpec((1, 1, 1, 8, LANE), lambda b, g, i: (b, g, i, 0, 0))],
        out_shape=[jax.ShapeDtypeStruct((B, NSA_GROUPS, NSA_HG * NSA_DH, S), F32),
                   jax.ShapeDtypeStruct((B, NSA_GROUPS, nsp, S), F32),
                   jax.ShapeDtypeStruct((B, NSA_GROUPS, nq, 8, LANE), F32)],
        scratch_shapes=[pltpu.VMEM((NSA_HG, ncp, tq), F32), pltpu.VMEM((NSA_HG, WINDOW + tw, tw), F32),
                        pltpu.VMEM((NSA_HG * tq, LANE), BF16)],
        compiler_params=_params(("arbitrary", "arbitrary", "arbitrary")),
        name="nsa_cw",
    )(q, kcd, vct, kwd, vwt, gt, ovt, gmat)


def _nsa_sel_body(flag_ref, q_ref, ks_ref, vst_ref, sel_ref, gt_ref, ocw_ref, o_ref,
                  m_sc, acc_sc, skb_sc, qst_sc, sa_sc, sb_sc, lst_sc, *, tq, tk, n_tiles):
    g = pl.program_id(1)
    i = pl.program_id(2)
    t0 = i * tq
    j_diag = t0 // tk

    @pl.when(i == 0)
    def _():
        key_f = lax.broadcasted_iota(jnp.int32, (tk, tq), 0).astype(F32)
        for hh in range(NSA_HG):
            skb_sc[hh] = _slope(g, hh) * key_f

    qst_sc[...] = _stack_heads(q_ref, tq)
    m_sc[...] = jnp.full(m_sc.shape, NEG, F32)
    acc_sc[...] = jnp.zeros(acc_sc.shape, F32)

    base = ((pl.program_id(0) * NSA_GROUPS + g) * pl.num_programs(2) + i) * n_tiles
    n_act = jnp.int32(0)
    for j in range(n_tiles):
        lst_sc[n_act] = j
        n_act = n_act + jnp.where((flag_ref[base + j] > 0) & (j < j_diag), 1, 0)
    lst_sc[n_act] = j_diag

    def scores(j, s_ref):
        k0 = pl.multiple_of(j * tk, tk)
        s_ref[...] = _dot_nt(ks_ref[0, pl.ds(k0, tk), :], qst_sc[...])

    def consume(j, s_ref, diagonal):
        k0 = pl.multiple_of(j * tk, tk)
        vt = _with_ones(vst_ref[0, :, pl.ds(k0, tk)])
        b0 = j * (tk // SEL_BLOCK)
        pieces = []
        for r in range(tk // SEL_BLOCK):
            chosen = sel_ref[0, 0, pl.ds(b0 + r, 1), :]
            pieces.append(jnp.broadcast_to(jnp.where(chosen > 0.5, 0.0, NEG), (SEL_BLOCK, tq)))
        madd = jnp.concatenate(pieces, axis=0)
        if diagonal:
            key_i = lax.broadcasted_iota(jnp.int32, (tk, tq), 0)
            qry_i = lax.broadcasted_iota(jnp.int32, (tk, tq), 1)
            madd = jnp.where(key_i + k0 <= qry_i + t0, madd, NEG)
        for hh in range(NSA_HG):
            c = _slope(g, hh) * (k0 - t0).astype(F32)
            z = s_ref[:, hh * tq:(hh + 1) * tq] + skb_sc[hh] + madd
            m_prev = m_sc[hh]
            m_new = jnp.maximum(m_prev, jnp.max(z, axis=0, keepdims=True) + c)
            alpha = jnp.exp2(m_prev - m_new)
            p = jnp.exp2(z - (m_new - c))
            acc_sc[hh] = alpha * acc_sc[hh] + _dot(vt, p.astype(BF16))
            m_sc[hh] = m_new

    scores(lst_sc[0], sa_sc)

    def pair(jj, carry):
        it = 2 * jj
        scores(lst_sc[it + 1], sb_sc)
        consume(lst_sc[it], sa_sc, False)
        scores(lst_sc[it + 2], sa_sc)
        consume(lst_sc[it + 1], sb_sc, False)
        return carry

    lax.fori_loop(0, n_act // 2, pair, 0)

    @pl.when(n_act % 2 == 0)
    def _():
        consume(j_diag, sa_sc, True)

    @pl.when(n_act % 2 == 1)
    def _():
        scores(j_diag, sb_sc)
        consume(lst_sc[n_act - 1], sa_sc, False)
        consume(j_diag, sb_sc, True)

    gates = gt_ref[0]
    for pair_i in range(NSA_HG // 2):
        halves = []
        for hh in (2 * pair_i, 2 * pair_i + 1):
            acc = acc_sc[hh]
            w = gates[NSA_HG + hh:NSA_HG + hh + 1] / jnp.maximum(acc[NSA_DH:NSA_DH + 1], 1e-30)
            halves.append(acc[0:NSA_DH] * w + ocw_ref[0, 0, NSA_DH * hh:NSA_DH * (hh + 1), :])
        o_ref[0, :, LANE * pair_i:LANE * (pair_i + 1)] = jnp.concatenate(halves, axis=0).T.astype(BF16)


def _nsa_sel(flags, q, ksd, vst, sel, gt, ocw, tq, tk):
    B, S, _ = q.shape
    nsp = sel.shape[2]
    n_tiles = S // tk
    qs = pl.BlockSpec((1, tq, 2 * LANE), lambda b, g, i, f: (b, i, g))
    return pl.pallas_call(
        functools.partial(_nsa_sel_body, tq=tq, tk=tk, n_tiles=n_tiles),
        grid_spec=pltpu.PrefetchScalarGridSpec(
            num_scalar_prefetch=1,
            grid=(B, NSA_GROUPS, S // tq),
            in_specs=[qs,
                      pl.BlockSpec((1, S, LANE), lambda b, g, i, f: (b, 0, g)),
                      pl.BlockSpec((1, NSA_DH, S), lambda b, g, i, f: (b, g, 0)),
                      pl.BlockSpec((1, 1, nsp, tq), lambda b, g, i, f: (b, g, 0, i)),
                      pl.BlockSpec((1, GATE_ROWS, tq), lambda b, g, i, f: (b, g, i)),
                      pl.BlockSpec((1, 1, NSA_HG * NSA_DH, tq), lambda b, g, i, f: (b, g, 0, i))],
            out_specs=qs,
            scratch_shapes=[pltpu.VMEM((NSA_HG, 1, tq), F32),
                            pltpu.VMEM((NSA_HG, NSA_DH + ONES_ROWS, tq), F32),
                            pltpu.VMEM((NSA_HG, tk, tq), F32),
                            pltpu.VMEM((NSA_HG * tq, LANE), BF16),
                            pltpu.VMEM((tk, NSA_HG * tq), F32),
                            pltpu.VMEM((tk, NSA_HG * tq), F32),
                            pltpu.SMEM((n_tiles + 1,), jnp.int32)]),
        out_shape=jax.ShapeDtypeStruct((B, S, 4 * LANE), BF16),
        compiler_params=_params(("arbitrary", "arbitrary", "arbitrary")),
        name="nsa_sel",
    )(flags, q, ksd, vst, sel, gt, ocw)


def _mla_body(q_ref, k_ref, vt_ref, o_ref, m_sc, acc_sc, sa_sc, sb_sc, *, tq, tc):
    n_full = pl.program_id(2)
    m_sc[...] = jnp.full(m_sc.shape, NEG, F32)
    acc_sc[...] = jnp.zeros(acc_sc.shape, F32)

    def scores(j, s_ref):
        k0 = pl.multiple_of(j * tq, tq)
        for hh in range(2):
            lanes = slice(LANE * hh, LANE * (hh + 1))
            s_ref[hh] = _dot_nt(k_ref[0, pl.ds(k0, tq), lanes], q_ref[0, :, lanes])

    def consume(j, s_ref, diagonal):
        k0 = pl.multiple_of(j * tq, tq)
        for hh in range(2):
            vt = _with_ones(vt_ref[0, MLA_V * hh:MLA_V * (hh + 1), pl.ds(k0, tq)])
            for c in range(tq // tc):
                cols = slice(c * tc, (c + 1) * tc)
                st = s_ref[hh, :, cols]
                if diagonal:
                    causal = (lax.broadcasted_iota(jnp.int32, (tq, tc), 0)
                              <= lax.broadcasted_iota(jnp.int32, (tq, tc), 1) + c * tc)
                    st = jnp.where(causal, st, NEG)
                m_prev = m_sc[hh, :, cols]
                m_new = jnp.maximum(m_prev, jnp.max(st, axis=0, keepdims=True))
                alpha = jnp.exp2(m_prev - m_new)
                p = jnp.exp2(st - m_new)
                acc_sc[hh, :, cols] = alpha * acc_sc[hh, :, cols] + _dot(vt, p.astype(BF16))
                m_sc[hh, :, cols] = m_new

    scores(0, sa_sc)

    def pair(jj, carry):
        j = 2 * jj
        scores(j + 1, sb_sc)
        consume(j, sa_sc, False)
        scores(j + 2, sa_sc)
        consume(j + 1, sb_sc, False)
        return carry

    lax.fori_loop(0, n_full // 2, pair, 0)

    @pl.when(n_full % 2 == 0)
    def _():
        consume(n_full, sa_sc, True)

    @pl.when(n_full % 2 == 1)
    def _():
        scores(n_full, sb_sc)
        consume(n_full - 1, sa_sc, False)
        consume(n_full, sb_sc, True)

    halves = []
    for hh in range(2):
        acc = acc_sc[hh]
        halves.append(acc[0:MLA_V] * (1.0 / jnp.maximum(acc[MLA_V:MLA_V + 1], 1e-30)))
    o_ref[0] = jnp.concatenate(halves, axis=0).T.astype(BF16)


def _mla(qm, km, vmt, tq):
    B, S, _ = qm.shape
    return pl.pallas_call(
        functools.partial(_mla_body, tq=tq, tc=min(256, tq)),
        grid=(B, MLA_HEADS // 2, S // tq),
        in_specs=[pl.BlockSpec((1, tq, 2 * LANE), lambda b, p, i: (b, i, p)),
                  pl.BlockSpec((1, S, 2 * LANE), lambda b, p, i: (b, 0, p)),
                  pl.BlockSpec((1, 2 * MLA_V, S), lambda b, p, i: (b, p, 0))],
        out_specs=pl.BlockSpec((1, tq, LANE), lambda b, p, i: (b, i, p)),
        out_shape=jax.ShapeDtypeStruct((B, S, MLA_HEADS * MLA_V), BF16),
        scratch_shapes=[pltpu.VMEM((2, 1, tq), F32), pltpu.VMEM((2, MLA_V + ONES_ROWS, tq), F32),
                        pltpu.VMEM((2, tq, tq), F32), pltpu.VMEM((2, tq, tq), F32)],
        compiler_params=_params(("parallel", "parallel", "arbitrary")),
        name="mla",
    )(qm, km, vmt)


def _merge_body(on_ref, om_ref, ga_ref, gb_ref, x_ref, mod_ref, won_ref, wom_ref, wout_ref, o_ref):
    y = (ga_ref[0].astype(F32) * _dot(on_ref[0], won_ref[...])
         + gb_ref[0].astype(F32) * _dot(om_ref[0], wom_ref[...]))
    o_ref[0] = x_ref[0] + mod_ref[0, 2:3, :] * _dot(y.astype(BF16), wout_ref[...])


def _merge(o_nsa, o_mla, ga, gb, x, mod, w_on, w_om, w_out, tm):
    B, S, D = x.shape
    row = lambda n: pl.BlockSpec((1, tm, n), lambda b, i: (b, i, 0))
    full = lambda a: pl.BlockSpec(a.shape, lambda b, i: (0,) * a.ndim)
    return pl.pallas_call(
        _merge_body,
        grid=(B, S // tm),
        in_specs=[row(o_nsa.shape[2]), row(o_mla.shape[2]), row(D), row(D), row(D),
                  pl.BlockSpec((1, 6, D), lambda b, i: (b, 0, 0)), full(w_on), full(w_om), full(w_out)],
        out_specs=row(D),
        out_shape=jax.ShapeDtypeStruct((B, S, D), F32),
        compiler_params=_params(("parallel", "parallel")),
        name="merge",
    )(o_nsa, o_mla, ga, gb, x, mod, w_on, w_om, w_out)


def _mlp_body(x_ref, mod_ref, gmlp_ref, w1_ref, w2_ref, gfin_ref, o_ref, *, fc, final):
    x1 = x_ref[0]
    h = _rms(x1, gmlp_ref[...]) * (1.0 + mod_ref[0, 4:5, :]) + mod_ref[0, 3:4, :]
    hb = h.astype(BF16)
    acc = jnp.zeros(x1.shape, F32)
    for c in range(w1_ref.shape[1] // fc):
        a = jnp.maximum(_dot(hb, w1_ref[:, c * fc:(c + 1) * fc]), 0.0)
        acc = acc + _dot((a * a).astype(BF16), w2_ref[c * fc:(c + 1) * fc, :])
    x2 = x1 + mod_ref[0, 5:6, :] * acc
    o_ref[0] = _rms(x2, gfin_ref[...]) if final else x2


def _mlp(x1, mod, g_mlp, w1, w2, g_fin, tm, fc, final):
    B, S, D = x1.shape
    row = pl.BlockSpec((1, tm, D), lambda b, i: (b, i, 0))
    full = lambda a: pl.BlockSpec(a.shape, lambda b, i: (0,) * a.ndim)
    return pl.pallas_call(
        functools.partial(_mlp_body, fc=fc, final=final),
        grid=(B, S // tm),
        in_specs=[row, pl.BlockSpec((1, 6, D), lambda b, i: (b, 0, 0)), full(g_mlp), full(w1), full(w2),
                  full(g_fin)],
        out_specs=row,
        out_shape=jax.ShapeDtypeStruct((B, S, D), F32),
        compiler_params=_params(("parallel", "parallel")),
        name="mlp",
    )(x1, mod, g_mlp, w1, w2, g_fin)


def _dup_groups(w):
    g0, g1 = w[:, :NSA_DH], w[:, NSA_DH:]
    return jnp.concatenate([g0, g0, g1, g1], axis=1)


def _inproj_weight(w_in):
    d = w_in.shape[0]
    o = 0
    parts = {}
    for name, n in (("q", 512), ("kc", 128), ("vc", 128), ("ks", 128), ("vs", 128), ("kw", 128),
                    ("vw", 128), ("g", 24), ("qd", MLA_Q_RANK), ("kvd", MLA_KV_RANK), ("kr", MLA_ROPE),
                    ("ga", D_MODEL), ("gb", D_MODEL)):
        parts[name] = w_in[:, o:o + n]
        o += n
    wg = parts["g"].reshape(d, 3, NSA_GROUPS, NSA_HG).transpose(0, 2, 1, 3).reshape(d, NSA_GROUPS, 3 * NSA_HG)
    wg = jnp.pad(wg, ((0, 0), (0, 0), (0, GATE_ROWS - 3 * NSA_HG))).reshape(d, NSA_GROUPS * GATE_ROWS)
    wg = jnp.pad(wg, ((0, 0), (0, LANE - NSA_GROUPS * GATE_ROWS)))
    half = MLA_ROPE // 2
    x1, x2 = parts["kr"][:, :half], parts["kr"][:, half:]
    z_lo = jnp.zeros((d, MLA_NOPE), F32)
    z_hi = jnp.zeros((d, LANE - MLA_NOPE - MLA_ROPE), F32)
    kra = jnp.concatenate([z_lo, x1, x2, z_hi], axis=1)
    krb = jnp.concatenate([z_lo, -x2, x1, z_hi], axis=1)
    w_all = jnp.concatenate(
        [parts["q"], parts["kc"], parts["vc"], _dup_groups(parts["ks"]), parts["vs"],
         _dup_groups(parts["kw"]), parts["vw"], wg, parts["qd"], parts["kvd"], kra, krb,
         parts["ga"], parts["gb"]], axis=1)
    assert w_all.shape[1] == IN_COLS
    return w_all.astype(BF16)


def _mla_q_weight(w_uq):
    r = w_uq.shape[0]
    w = w_uq.reshape(r, MLA_HEADS, MLA_NOPE + MLA_ROPE)
    half = MLA_ROPE // 2
    nope, x1, x2 = w[..., :MLA_NOPE], w[..., MLA_NOPE:MLA_NOPE + half], w[..., MLA_NOPE + half:]
    z_hi = jnp.zeros((r, MLA_HEADS, LANE - MLA_NOPE - MLA_ROPE), F32)
    a = jnp.concatenate([nope, x1, x2, z_hi], axis=-1).reshape(r, MLA_HEADS * LANE)
    b = jnp.concatenate([jnp.zeros_like(nope), -x2, x1, z_hi], axis=-1).reshape(r, MLA_HEADS * LANE)
    return jnp.concatenate([a, b], axis=1).astype(BF16)


def _mla_kv_weight(w_uk, w_uv):
    r = w_uk.shape[0]
    k = w_uk.reshape(r, MLA_HEADS, MLA_NOPE)
    k = jnp.pad(k, ((0, 0), (0, 0), (0, LANE - MLA_NOPE))).reshape(r, MLA_HEADS * LANE)
    return jnp.concatenate([k, w_uv], axis=1).astype(BF16)


def _rope_tables(seq):
    half = MLA_ROPE // 2
    pos = jnp.arange(seq, dtype=F32)
    inv_freq = ROPE_THETA ** (-jnp.arange(half, dtype=F32) / half)
    ang = pos[:, None] * inv_freq[None, :]
    cos, sin = jnp.cos(ang), jnp.sin(ang)
    ones = jnp.ones((seq, MLA_NOPE), F32)
    z_lo = jnp.zeros((seq, MLA_NOPE), F32)
    z_hi = jnp.zeros((seq, LANE - MLA_NOPE - MLA_ROPE), F32)
    return (jnp.concatenate([ones, cos, cos, z_hi], axis=1),
            jnp.concatenate([z_lo, sin, sin, z_hi], axis=1))


def _compress_weights(pe, w1, w2):
    hid = w1.shape[1]
    wa = w1[:CMP_STRIDE * NSA_DH].reshape(CMP_STRIDE, NSA_DH, hid)
    wb = w1[CMP_STRIDE * NSA_DH:].reshape(CMP_STRIDE, NSA_DH, hid)
    z = jnp.zeros_like(wa)
    cols = []
    for g in range(NSA_GROUPS):
        for w in (wa, wb):
            slots = [z, z]
            slots[g] = w
            cols.append(jnp.stack(slots, axis=1).reshape(CMP_STRIDE * NSA_GROUPS * NSA_DH, hid))
    w1big = jnp.concatenate(cols, axis=1).astype(BF16)
    pe8 = jnp.zeros((8, CMP_LEN * NSA_DH), F32).at[0].set(pe.reshape(-1))
    w2d = jnp.concatenate([w2, w2], axis=1).astype(BF16)
    return w1big, pe8, w1, w2d


def _overlap_table(seq):
    ns = seq // SEL_BLOCK
    nsp = -(-ns // LANE) * LANE
    ncp = seq // CMP_STRIDE
    s_id = jnp.arange(nsp)[:, None]
    c_id = jnp.arange(ncp)[None, :]
    c_start = c_id * CMP_STRIDE
    return ((c_start < s_id * SEL_BLOCK + SEL_BLOCK) & (c_start + CMP_LEN - 1 >= s_id * SEL_BLOCK)
            & (c_id < ncp - 1)).astype(BF16)


def kernel(x, c, w_ada, b_ada, g_mix, w_in, pe_ck, w_ck1, w_ck2, pe_cv, w_cv1, w_cv2, g_cq, w_uq, g_ckv,
           w_uk, w_uv, w_o_nsa, w_o_mla, w_out, g_mlp, w_fc1, w_fc2, g_final):
    B, S, D = x.shape
    depth = w_ada.shape[0]
    tm_in = min(256, S)
    tm_out = min(512, S)
    tq_nsa = min(256, S)
    tk_sel = min(512, S)
    tq_mla = min(512, S)
    cs, sn = _rope_tables(S)
    ovt = _overlap_table(S)
    gmat = (jnp.arange(ovt.shape[0])[:, None] // (tk_sel // SEL_BLOCK) == jnp.arange(LANE)[None, :]).astype(BF16)
    n_chunks = S // CMP_STRIDE
    for layer in range(depth):
        mod = _ada(c, w_ada[layer], b_ada[layer])
        (q, kc_in, vc_in, ksd, vst, kwd, vwt, gt, qm, km, vmt, ga, gb) = _inproj(
            x, mod, g_mix[layer][None], _inproj_weight(w_in[layer]), g_cq[layer][None],
            _mla_q_weight(w_uq[layer]), g_ckv[layer][None], _mla_kv_weight(w_uk[layer], w_uv[layer]),
            cs, sn, tm_in)
        w1k, pek, w1kf, w2k = _compress_weights(pe_ck[layer], w_ck1[layer], w_ck2[layer])
        w1v, pev, w1vf, w2v = _compress_weights(pe_cv[layer], w_cv1[layer], w_cv2[layer])
        width = CMP_STRIDE * NSA_GROUPS * NSA_DH
        kcd, vct = _compress(kc_in.reshape(B, n_chunks, width), vc_in.reshape(B, n_chunks, width),
                             w1k, w1v, pek, pev, w1kf, w1vf, w2k, w2v)
        ocw, sel, cnt = _nsa_cw(q, kcd, vct, kwd, vwt, gt, ovt, gmat, tq_nsa)
        flags = (cnt[:, :, :, 0, :S // tk_sel] > 0.5).astype(jnp.int32).reshape(-1)
        o_nsa = _nsa_sel(flags, q, ksd, vst, sel, gt, ocw, tq_nsa, tk_sel)
        o_mla = _mla(qm, km, vmt, tq_mla)
        x1 = _merge(o_nsa, o_mla, ga, gb, x, mod, w_o_nsa[layer].astype(BF16), w_o_mla[layer].astype(BF16),
                    w_out[layer].astype(BF16), tm_out)
        x = _mlp(x1, mod, g_mlp[layer][None], w_fc1[layer].astype(BF16), w_fc2[layer].astype(BF16),
                 g_final[None], tm_out, 1024, layer == depth - 1)
    return x
```

```python
import functools
import math

import jax
import jax.numpy as jnp
from jax import lax
from jax.experimental import pallas as pl
from jax.experimental.pallas import tpu as pltpu

F32 = jnp.float32
BF16 = jnp.bfloat16

D_MODEL = 1024
NSA_GROUPS = 2
NSA_HG = 4
NSA_DH = 64
CMP_LEN = 32
CMP_STRIDE = 16
CMP_HIDDEN = 256
SEL_BLOCK = 64
SEL_TOPK = 16
WINDOW = 512
MLA_HEADS = 8
MLA_NOPE = 64
MLA_ROPE = 32
MLA_V = 64
MLA_Q_RANK = 384
MLA_KV_RANK = 256
ROPE_THETA = 10000.0
D_FF = 4 * D_MODEL
EPS = 1e-6
FORCE_BONUS = 1000.0

LANE = 128
HALF = LANE // 2
NEG = -1e30
LOG2E = math.log2(math.e)
VMEM_LIMIT = 52 * 1024 * 1024
GATE_ROWS = 16
ONES_ROWS = 16

OFF_Q = 0
OFF_KC = 512
OFF_VC = 640
OFF_KS = 768
OFF_VS = 1024
OFF_KW = 1152
OFF_VW = 1408
OFF_G = 1536
OFF_QD = 1664
OFF_KVD = 2048
OFF_KRA = 2304
OFF_KRB = 2432
OFF_GA = 2560
OFF_GB = 3584
IN_COLS = 4608


def _dot(a, b):
    return jnp.dot(a, b, preferred_element_type=F32)


def _dot_nt(a, b):
    return lax.dot_general(a, b, (((1,), (1,)), ((), ())), preferred_element_type=F32)


def _rms(x, g):
    return x * lax.rsqrt(jnp.mean(x * x, axis=-1, keepdims=True) + EPS) * g


def _sigmoid(x):
    return 1.0 / (1.0 + jnp.exp(-x))


def _params(sem):
    return pltpu.CompilerParams(dimension_semantics=sem, vmem_limit_bytes=VMEM_LIMIT)


def _ada_body(c_ref, w_ref, b_ref, o_ref):
    o_ref[...] = jnp.dot(c_ref[...], w_ref[...], preferred_element_type=F32,
                         precision=lax.Precision.HIGHEST) + b_ref[...]


def _ada(c, w, b):
    nb, d = c.shape
    n = w.shape[1]
    rows = 8
    cp = jnp.zeros((rows, d), F32).at[:nb].set(c)
    tn = 1536
    out = pl.pallas_call(
        _ada_body,
        grid=(n // tn,),
        in_specs=[pl.BlockSpec((rows, d), lambda j: (0, 0)),
                  pl.BlockSpec((d, tn), lambda j: (0, j)),
                  pl.BlockSpec((1, tn), lambda j: (0, j))],
        out_specs=pl.BlockSpec((rows, tn), lambda j: (0, j)),
        out_shape=jax.ShapeDtypeStruct((rows, n), F32),
        compiler_params=_params(("arbitrary",)),
        name="ada",
    )(cp, w, b.reshape(1, n))
    return out[:nb].reshape(nb, 6, d)


def _inproj_body(x_ref, mod_ref, gmix_ref, w_ref, gcq_ref, wq2_ref, gckv_ref, wkv_ref,
                 cs_ref, sn_ref,
                 q_ref, kc_ref, vc_ref, ks_ref, vst_ref, kw_ref, vwt_ref, gt_ref,
                 qm_ref, km_ref, vmt_ref, ga_ref, gb_ref):
    x = x_ref[0]
    h = _rms(x, gmix_ref[...]) * (1.0 + mod_ref[0, 1:2, :]) + mod_ref[0, 0:1, :]
    hb = h.astype(BF16)

    def proj(lo, n):
        return _dot(hb, w_ref[:, lo:lo + n])

    q_ref[0] = (proj(OFF_Q, 512) * (NSA_DH ** -0.5 * LOG2E)).astype(BF16)
    kc_ref[0] = proj(OFF_KC, 128).astype(BF16)
    vc_ref[0] = proj(OFF_VC, 128).astype(BF16)
    ks_ref[0] = proj(OFF_KS, 256).astype(BF16)
    kw_ref[0] = proj(OFF_KW, 256).astype(BF16)
    vst_ref[0] = proj(OFF_VS, LANE).T.astype(BF16)
    vwt_ref[0] = proj(OFF_VW, LANE).T.astype(BF16)
    gt_ref[0] = _sigmoid(proj(OFF_G, LANE)).T[0:NSA_GROUPS * GATE_ROWS]
    ga_ref[0] = _sigmoid(proj(OFF_GA, 1024)).astype(BF16)
    gb_ref[0] = _sigmoid(proj(OFF_GB, 1024)).astype(BF16)

    cs = cs_ref[...]
    sn = sn_ref[...]
    nh = MLA_HEADS * LANE
    cq = _rms(proj(OFF_QD, MLA_Q_RANK), gcq_ref[...]).astype(BF16)
    q2 = _dot(cq, wq2_ref[...])
    scale = (MLA_NOPE + MLA_ROPE) ** -0.5 * LOG2E
    for hh in range(MLA_HEADS):
        a = q2[:, LANE * hh:LANE * (hh + 1)]
        b = q2[:, nh + LANE * hh:nh + LANE * (hh + 1)]
        qm_ref[0, :, LANE * hh:LANE * (hh + 1)] = ((a * cs + b * sn) * scale).astype(BF16)
    ckv = _rms(proj(OFF_KVD, MLA_KV_RANK), gckv_ref[...]).astype(BF16)
    kv = _dot(ckv, wkv_ref[...])
    kr = proj(OFF_KRA, LANE) * cs + proj(OFF_KRB, LANE) * sn
    for hh in range(MLA_HEADS):
        km_ref[0, :, LANE * hh:LANE * (hh + 1)] = (kv[:, LANE * hh:LANE * (hh + 1)] + kr).astype(BF16)
    vmt_ref[0] = kv[:, nh:].T.astype(BF16)


def _inproj(x, mod, g_mix, w_all, g_cq, wq2, g_ckv, wkv, cs, sn, tm):
    B, S, D = x.shape
    row = lambda n: pl.BlockSpec((1, tm, n), lambda b, i: (b, i, 0))
    col = lambda n: pl.BlockSpec((1, n, tm), lambda b, i: (b, 0, i))
    full = lambda a: pl.BlockSpec(a.shape, lambda b, i: (0,) * a.ndim)
    tok = lambda n, dt: (row(n), jax.ShapeDtypeStruct((B, S, n), dt))
    tr = lambda n, dt: (col(n), jax.ShapeDtypeStruct((B, n, S), dt))
    outs = [tok(512, BF16), tok(128, BF16), tok(128, BF16), tok(256, BF16), tr(LANE, BF16),
            tok(256, BF16), tr(LANE, BF16), tr(NSA_GROUPS * GATE_ROWS, F32),
            tok(1024, BF16), tok(1024, BF16), tr(MLA_HEADS * MLA_V, BF16), tok(1024, BF16), tok(1024, BF16)]
    return pl.pallas_call(
        _inproj_body,
        grid=(B, S // tm),
        in_specs=[row(D),
                  pl.BlockSpec((1, 6, D), lambda b, i: (b, 0, 0)),
                  full(g_mix), full(w_all), full(g_cq), full(wq2), full(g_ckv), full(wkv),
                  pl.BlockSpec((tm, LANE), lambda b, i: (i, 0)),
                  pl.BlockSpec((tm, LANE), lambda b, i: (i, 0))],
        out_specs=[o[0] for o in outs],
        out_shape=[o[1] for o in outs],
        compiler_params=_params(("parallel", "parallel")),
        name="inproj",
    )(x, mod, g_mix, w_all, g_cq, wq2, g_ckv, wkv, cs, sn)


def _compress_one(x, w1big, pe, w1f, w2d):
    hcat = _dot(x, w1big)
    const = jnp.dot(pe, w1f, preferred_element_type=F32, precision=lax.Precision.HIGHEST)[0:1]
    n = x.shape[0]
    rowid = lax.broadcasted_iota(jnp.int32, (n, LANE), 0)
    hid_w = CMP_HIDDEN
    outs = []
    for g in range(NSA_GROUPS):
        a = hcat[:, 2 * hid_w * g:2 * hid_w * g + hid_w]
        b = hcat[:, 2 * hid_w * g + hid_w:2 * hid_w * (g + 1)]
        hid = a + pltpu.roll(b, n - 1, 0) + const
        act = hid * _sigmoid(hid)
        out = _dot(act.astype(BF16), w2d)
        outs.append(jnp.where(rowid < n - 1, out, 0.0))
    return outs


def _compress_body(xk_ref, xv_ref, w1k_ref, w1v_ref, pek_ref, pev_ref, w1kf_ref, w1vf_ref,
                   w2k_ref, w2v_ref, kc_ref, vct_ref):
    kc = _compress_one(xk_ref[0], w1k_ref[...], pek_ref[...], w1kf_ref[...], w2k_ref[...])
    vc = _compress_one(xv_ref[0], w1v_ref[...], pev_ref[...], w1vf_ref[...], w2v_ref[...])
    for g in range(NSA_GROUPS):
        kc_ref[0, g] = kc[g].astype(BF16)
        vct_ref[0, g] = vc[g].T[0:NSA_DH].astype(BF16)


def _compress(xk, xv, w1k, w1v, pek, pev, w1kf, w1vf, w2k, w2v):
    B, n, width = xk.shape
    full = lambda a: pl.BlockSpec(a.shape, lambda b: (0,) * a.ndim)
    xs = pl.BlockSpec((1, n, width), lambda b: (b, 0, 0))
    return pl.pallas_call(
        _compress_body,
        grid=(B,),
        in_specs=[xs, xs, full(w1k), full(w1v), full(pek), full(pev), full(w1kf), full(w1vf),
                  full(w2k), full(w2v)],
        out_specs=[pl.BlockSpec((1, NSA_GROUPS, n, LANE), lambda b: (b, 0, 0, 0)),
                   pl.BlockSpec((1, NSA_GROUPS, NSA_DH, n), lambda b: (b, 0, 0, 0))],
        out_shape=[jax.ShapeDtypeStruct((B, NSA_GROUPS, n, LANE), BF16),
                   jax.ShapeDtypeStruct((B, NSA_GROUPS, NSA_DH, n), BF16)],
        compiler_params=_params(("parallel",)),
        name="compress",
    )(xk, xv, w1k, w1v, pek, pev, w1kf, w1vf, w2k, w2v)


def _stack_heads(q_ref, tq):
    upper = lax.broadcasted_iota(jnp.int32, (tq, LANE), 1) >= HALF
    parts = []
    for hh in range(NSA_HG):
        slab = q_ref[0, :, LANE * (hh // 2):LANE * (hh // 2 + 1)]
        keep = upper if hh % 2 else jnp.logical_not(upper)
        parts.append(jnp.where(keep, slab, jnp.zeros_like(slab)))
    return jnp.concatenate(parts, axis=0)


def _slope(g, hh):
    return jnp.where(g == 0, 1.0, 2.0 ** -NSA_HG).astype(F32) * (2.0 ** -(hh + 1) * LOG2E)


def _with_ones(vt):
    return jnp.concatenate([vt, jnp.ones((ONES_ROWS, vt.shape[1]), vt.dtype)], axis=0)


def _nsa_cw_body(q_ref, kc_ref, vct_ref, kw_ref, vwt_ref, gt_ref, ovt_ref, gm_ref,
                 ocw_ref, sel_ref, cnt_ref, bc_sc, bw_sc, qst_sc, *, tq, tw, seq, n_parts):
    g = pl.program_id(1)
    i = pl.program_id(2)
    t0 = i * tq
    qst_sc[...] = _stack_heads(q_ref, tq)
    gates = gt_ref[0]
    ncp = kc_ref.shape[2]
    nsp = ovt_ref.shape[0]
    wk = WINDOW + tw

    @pl.when(i == 0)
    def _():
        c_end_f = (lax.broadcasted_iota(jnp.int32, (ncp, tq), 0) * CMP_STRIDE + (CMP_LEN - 1)).astype(F32)
        key_f = lax.broadcasted_iota(jnp.int32, (wk, tw), 0).astype(F32)
        for hh in range(NSA_HG):
            bc_sc[hh] = _slope(g, hh) * c_end_f
            bw_sc[hh] = _slope(g, hh) * key_f

    def tile_body(ncp_e, nsp_e):
        vct = _with_ones(vct_ref[0, 0, :, 0:ncp_e])
        st = _dot_nt(kc_ref[0, 0, 0:ncp_e, :], qst_sc[...])
        t_c = t0 + lax.broadcasted_iota(jnp.int32, (ncp_e, tq), 1)
        c_end = lax.broadcasted_iota(jnp.int32, (ncp_e, tq), 0) * CMP_STRIDE + (CMP_LEN - 1)
        madd_c = jnp.where(t_c >= c_end, 0.0, NEG)
        psum = jnp.zeros((ncp_e, tq), F32)
        o_cmp = []
        for hh in range(NSA_HG):
            z = st[:, hh * tq:(hh + 1) * tq] + bc_sc[hh, 0:ncp_e] + madd_c
            m = jnp.max(z, axis=0, keepdims=True)
            e = jnp.exp2(z - m)
            o_aug = _dot(vct, e.astype(BF16))
            r = jnp.where(m > 0.5 * NEG, 1.0 / jnp.maximum(o_aug[NSA_DH:NSA_DH + 1], 1e-30), 0.0)
            psum = psum + e * r
            o_cmp.append(o_aug[0:NSA_DH] * r)

        ovt = ovt_ref[0:nsp_e, 0:ncp_e]
        p1 = psum.astype(BF16)
        r1 = psum - p1.astype(F32)
        p2 = r1.astype(BF16)
        p3 = (r1 - p2.astype(F32)).astype(BF16)
        p_slc = _dot(ovt, p1) + _dot(ovt, p2) + _dot(ovt, p3)

        n_forced = 3
        sels = []
        for c in range(tq // tw):
            blk = lax.broadcasted_iota(jnp.int32, (nsp_e, tw), 0)
            cur = (t0 + c * tw + lax.broadcasted_iota(jnp.int32, (nsp_e, tw), 1)) // SEL_BLOCK
            valid = blk <= cur
            forced = (blk == 0) | (blk >= cur - 1)
            score = jnp.where(valid & jnp.logical_not(forced), p_slc[:, c * tw:(c + 1) * tw], -1.0)
            for _ in range(max(min(SEL_TOPK, seq // SEL_BLOCK) - n_forced, 0)):
                m = jnp.max(score, axis=0, keepdims=True)
                first = jnp.min(jnp.where(score == m, blk, nsp), axis=0, keepdims=True)
                score = jnp.where(blk == first, -1.0, score)
            sels.append(jnp.where(valid & (score < 0.0), 1.0, 0.0))
        sel = jnp.concatenate(sels, axis=1)
        sel_ref[0, 0, 0:nsp_e] = sel
        if nsp_e < nsp:
            sel_ref[0, 0, nsp_e:nsp] = jnp.zeros((nsp - nsp_e, tq), F32)
        per_blk = _dot_nt(jnp.ones((8, tq), BF16), sel.astype(BF16))
        cnt_ref[0, 0, 0] = _dot(per_blk.astype(BF16), gm_ref[0:nsp_e])

        for sub in range(tq // tw):
            k_start = pl.multiple_of(jnp.clip(t0 + sub * tw - WINDOW, 0, seq - wk), LANE)
            kw = kw_ref[0, pl.ds(k_start, wk), :]
            vwt = _with_ones(vwt_ref[0, :, pl.ds(k_start, wk)])
            qsub = jnp.concatenate(
                [qst_sc[hh * tq + sub * tw:hh * tq + (sub + 1) * tw] for hh in range(NSA_HG)], axis=0)
            st = _dot_nt(kw, qsub)
            t_w = t0 + sub * tw + lax.broadcasted_iota(jnp.int32, (wk, tw), 1)
            dist_i = t_w - (k_start + lax.broadcasted_iota(jnp.int32, (wk, tw), 0))
            madd_w = jnp.where((dist_i >= 0) & (dist_i < WINDOW), 0.0, NEG)
            cols = slice(sub * tw, (sub + 1) * tw)
            for hh in range(NSA_HG):
                z = st[:, hh * tw:(hh + 1) * tw] + bw_sc[hh] + madd_w
                e = jnp.exp2(z - jnp.max(z, axis=0, keepdims=True))
                o_aug = _dot(vwt, e.astype(BF16))
                o_win = o_aug[0:NSA_DH] * (1.0 / jnp.maximum(o_aug[NSA_DH:NSA_DH + 1], 1e-30))
                ocw_ref[0, 0, NSA_DH * hh:NSA_DH * (hh + 1), cols] = (
                    gates[hh:hh + 1, cols] * o_cmp[hh][:, cols]
                    + gates[2 * NSA_HG + hh:2 * NSA_HG + hh + 1, cols] * o_win)

    part = (i * n_parts) // pl.num_programs(2)
    for v in range(n_parts):
        pl.when(part == v)(functools.partial(tile_body, ncp * (v + 1) // n_parts, nsp * (v + 1) // n_parts))


def _cw_parts(nq, ncp, nsp):
    for n in (4, 2):
        if nq % n == 0 and (ncp // n) % LANE == 0 and (nsp // n) % 8 == 0:
            return n
    return 1


def _nsa_cw(q, kcd, vct, kwd, vwt, gt, ovt, gmat, tq):
    B, S, _ = q.shape
    ncp = kcd.shape[2]
    nsp = ovt.shape[0]
    nq = S // tq
    tw = min(LANE, tq)
    return pl.pallas_call(
        functools.partial(_nsa_cw_body, tq=tq, tw=tw, seq=S, n_parts=_cw_parts(nq, ncp, nsp)),
        grid=(B, NSA_GROUPS, nq),
        in_specs=[pl.BlockSpec((1, tq, 2 * LANE), lambda b, g, i: (b, i, g)),
                  pl.BlockSpec((1, 1, ncp, LANE), lambda b, g, i: (b, g, 0, 0)),
                  pl.BlockSpec((1, 1, NSA_DH, ncp), lambda b, g, i: (b, g, 0, 0)),
                  pl.BlockSpec((1, S, LANE), lambda b, g, i: (b, 0, g)),
                  pl.BlockSpec((1, NSA_DH, S), lambda b, g, i: (b, g, 0)),
                  pl.BlockSpec((1, GATE_ROWS, tq), lambda b, g, i: (b, g, i)),
                  pl.BlockSpec(ovt.shape, lambda b, g, i: (0, 0)),
                  pl.BlockSpec(gmat.shape, lambda b, g, i: (0, 0))],
        out_specs=[pl.BlockSpec((1, 1, NSA_HG * NSA_DH, tq), lambda b, g, i: (b, g, 0, i)),
                   pl.BlockSpec((1, 1, nsp, tq), lambda b, g, i: (b, g, 0, i)),
                   pl.BlockSpec((1, 1, 1, 8, LANE), lambda b, g, i: (b, g, i, 0, 0))],
        out_shape=[jax.ShapeDtypeStruct((B, NSA_GROUPS, NSA_HG * NSA_DH, S), F32),
                   jax.ShapeDtypeStruct((B, NSA_GROUPS, nsp, S), F32),
                   jax.ShapeDtypeStruct((B, NSA_GROUPS, nq, 8, LANE), F32)],
        scratch_shapes=[pltpu.VMEM((NSA_HG, ncp, tq), F32), pltpu.VMEM((NSA_HG, WINDOW + tw, tw), F32),
                        pltpu.VMEM((NSA_HG * tq, LANE), BF16)],
        compiler_params=_params(("arbitrary", "arbitrary", "arbitrary")),
        name="nsa_cw",
    )(q, kcd, vct, kwd, vwt, gt, ovt, gmat)


def _nsa_sel_body(flag_ref, q_ref, ks_ref, vst_ref, sel_ref, gt_ref, ocw_ref, o_ref,
                  m_sc, acc_sc, skb_sc, qst_sc, sa_sc, sb_sc, lst_sc, *, tq, tk, n_tiles):
    g = pl.program_id(1)
    i = pl.program_id(2)
    t0 = i * tq
    j_diag = t0 // tk

    @pl.when(i == 0)
    def _():
        key_f = lax.broadcasted_iota(jnp.int32, (tk, tq), 0).astype(F32)
        for hh in range(NSA_HG):
            skb_sc[hh] = _slope(g, hh) * key_f

    qst_sc[...] = _stack_heads(q_ref, tq)
    m_sc[...] = jnp.full(m_sc.shape, NEG, F32)
    acc_sc[...] = jnp.zeros(acc_sc.shape, F32)

    base = ((pl.program_id(0) * NSA_GROUPS + g) * pl.num_programs(2) + i) * n_tiles
    n_act = jnp.int32(0)
    for j in range(n_tiles):
        lst_sc[n_act] = j
        n_act = n_act + jnp.where((flag_ref[base + j] > 0) & (j < j_diag), 1, 0)
    lst_sc[n_act] = j_diag

    def scores(j, s_ref):
        k0 = pl.multiple_of(j * tk, tk)
        s_ref[...] = _dot_nt(ks_ref[0, pl.ds(k0, tk), :], qst_sc[...])

    def consume(j, s_ref, diagonal):
        k0 = pl.multiple_of(j * tk, tk)
        vt = _with_ones(vst_ref[0, :, pl.ds(k0, tk)])
        b0 = j * (tk // SEL_BLOCK)
        pieces = []
        for r in range(tk // SEL_BLOCK):
            chosen = sel_ref[0, 0, pl.ds(b0 + r, 1), :]
            pieces.append(jnp.broadcast_to(jnp.where(chosen > 0.5, 0.0, NEG), (SEL_BLOCK, tq)))
        madd = jnp.concatenate(pieces, axis=0)
        if diagonal:
            key_i = lax.broadcasted_iota(jnp.int32, (tk, tq), 0)
            qry_i = lax.broadcasted_iota(jnp.int32, (tk, tq), 1)
            madd = jnp.where(key_i + k0 <= qry_i + t0, madd, NEG)
        for hh in range(NSA_HG):
            c = _slope(g, hh) * (k0 - t0).astype(F32)
            z = s_ref[:, hh * tq:(hh + 1) * tq] + skb_sc[hh] + madd
            m_prev = m_sc[hh]
            m_new = jnp.maximum(m_prev, jnp.max(z, axis=0, keepdims=True) + c)
            alpha = jnp.exp2(m_prev - m_new)
            p = jnp.exp2(z - (m_new - c))
            acc_sc[hh] = alpha * acc_sc[hh] + _dot(vt, p.astype(BF16))
            m_sc[hh] = m_new

    scores(lst_sc[0], sa_sc)

    def pair(jj, carry):
        it = 2 * jj
        scores(lst_sc[it + 1], sb_sc)
        consume(lst_sc[it], sa_sc, False)
        scores(lst_sc[it + 2], sa_sc)
        consume(lst_sc[it + 1], sb_sc, False)
        return carry

    lax.fori_loop(0, n_act // 2, pair, 0)

    @pl.when(n_act % 2 == 0)
    def _():
        consume(j_diag, sa_sc, True)

    @pl.when(n_act % 2 == 1)
    def _():
        scores(j_diag, sb_sc)
        consume(lst_sc[n_act - 1], sa_sc, False)
        consume(j_diag, sb_sc, True)

    gates = gt_ref[0]
    for pair_i in range(NSA_HG // 2):
        halves = []
        for hh in (2 * pair_i, 2 * pair_i + 1):
            acc = acc_sc[hh]
            w = gates[NSA_HG + hh:NSA_HG + hh + 1] / jnp.maximum(acc[NSA_DH:NSA_DH + 1], 1e-30)
            halves.append(acc[0:NSA_DH] * w + ocw_ref[0, 0, NSA_DH * hh:NSA_DH * (hh + 1), :])
        o_ref[0, :, LANE * pair_i:LANE * (pair_i + 1)] = jnp.concatenate(halves, axis=0).T.astype(BF16)


def _nsa_sel(flags, q, ksd, vst, sel, gt, ocw, tq, tk):
    B, S, _ = q.shape
    nsp = sel.shape[2]
    n_tiles = S // tk
    qs = pl.BlockSpec((1, tq, 2 * LANE), lambda b, g, i, f: (b, i, g))
    return pl.pallas_call(
        functools.partial(_nsa_sel_body, tq=tq, tk=tk, n_tiles=n_tiles),
        grid_spec=pltpu.PrefetchScalarGridSpec(
            num_scalar_prefetch=1,
            grid=(B, NSA_GROUPS, S // tq),
            in_specs=[qs,
                      pl.BlockSpec((1, S, LANE), lambda b, g, i, f: (b, 0, g)),
                      pl.BlockSpec((1, NSA_DH, S), lambda b, g, i, f: (b, g, 0)),
                      pl.BlockSpec((1, 1, nsp, tq), lambda b, g, i, f: (b, g, 0, i)),
                      pl.BlockSpec((1, GATE_ROWS, tq), lambda b, g, i, f: (b, g, i)),
                      pl.BlockSpec((1, 1, NSA_HG * NSA_DH, tq), lambda b, g, i, f: (b, g, 0, i))],
            out_specs=qs,
            scratch_shapes=[pltpu.VMEM((NSA_HG, 1, tq), F32),
                            pltpu.VMEM((NSA_HG, NSA_DH + ONES_ROWS, tq), F32),
                            pltpu.VMEM((NSA_HG, tk, tq), F32),
                            pltpu.VMEM((NSA_HG * tq, LANE), BF16),
                            pltpu.VMEM((tk, NSA_HG * tq), F32),
                            pltpu.VMEM((tk, NSA_HG * tq), F32),
                            pltpu.SMEM((n_tiles + 1,), jnp.int32)]),
        out_shape=jax.ShapeDtypeStruct((B, S, 4 * LANE), BF16),
        compiler_params=_params(("arbitrary", "arbitrary", "arbitrary")),
        name="nsa_sel",
    )(flags, q, ksd, vst, sel, gt, ocw)


def _mla_body(q_ref, k_ref, vt_ref, o_ref, m_sc, acc_sc, sa_sc, sb_sc, *, tq, tc, nh):
    n_full = pl.program_id(2)
    m_sc[...] = jnp.full(m_sc.shape, NEG, F32)
    acc_sc[...] = jnp.zeros(acc_sc.shape, F32)

    def scores(j, s_ref):
        k0 = pl.multiple_of(j * tq, tq)
        for hh in range(nh):
            lanes = slice(LANE * hh, LANE * (hh + 1))
            s_ref[hh] = _dot_nt(k_ref[0, pl.ds(k0, tq), lanes], q_ref[0, :, lanes])

    def consume(j, s_ref, diagonal):
        k0 = pl.multiple_of(j * tq, tq)
        for hh in range(nh):
            vt = _with_ones(vt_ref[0, MLA_V * hh:MLA_V * (hh + 1), pl.ds(k0, tq)])
            for c in range(tq // tc):
                cols = slice(c * tc, (c + 1) * tc)
                st = s_ref[hh, :, cols]
                if diagonal:
                    causal = (lax.broadcasted_iota(jnp.int32, (tq, tc), 0)
                              <= lax.broadcasted_iota(jnp.int32, (tq, tc), 1) + c * tc)
                    st = jnp.where(causal, st, NEG)
                m_prev = m_sc[hh, :, cols]
                m_new = jnp.maximum(m_prev, jnp.max(st, axis=0, keepdims=True))
                alpha = jnp.exp2(m_prev - m_new)
                p = jnp.exp2(st - m_new)
                acc_sc[hh, :, cols] = alpha * acc_sc[hh, :, cols] + _dot(vt, p.astype(BF16))
                m_sc[hh, :, cols] = m_new

    scores(0, sa_sc)

    def pair(jj, carry):
        j = 2 * jj
        scores(j + 1, sb_sc)
        consume(j, sa_sc, False)
        scores(j + 2, sa_sc)
        consume(j + 1, sb_sc, False)
        return carry

    lax.fori_loop(0, n_full // 2, pair, 0)

    @pl.when(n_full % 2 == 0)
    def _():
        consume(n_full, sa_sc, True)

    @pl.when(n_full % 2 == 1)
    def _():
        scores(n_full, sb_sc)
        consume(n_full - 1, sa_sc, False)
        consume(n_full, sb_sc, True)

    for pair_i in range(nh // 2):
        halves = []
        for hh in (2 * pair_i, 2 * pair_i + 1):
            acc = acc_sc[hh]
            halves.append(acc[0:MLA_V] * (1.0 / jnp.maximum(acc[MLA_V:MLA_V + 1], 1e-30)))
        o_ref[0, :, LANE * pair_i:LANE * (pair_i + 1)] = jnp.concatenate(halves, axis=0).T.astype(BF16)


def _mla(qm, km, vmt, tq, nh):
    B, S, _ = qm.shape
    return pl.pallas_call(
        functools.partial(_mla_body, tq=tq, tc=min(256, tq), nh=nh),
        grid=(B, MLA_HEADS // nh, S // tq),
        in_specs=[pl.BlockSpec((1, tq, nh * LANE), lambda b, p, i: (b, i, p)),
                  pl.BlockSpec((1, S, nh * LANE), lambda b, p, i: (b, 0, p)),
                  pl.BlockSpec((1, nh * MLA_V, S), lambda b, p, i: (b, p, 0))],
        out_specs=pl.BlockSpec((1, tq, nh * MLA_V), lambda b, p, i: (b, i, p)),
        out_shape=jax.ShapeDtypeStruct((B, S, MLA_HEADS * MLA_V), BF16),
        scratch_shapes=[pltpu.VMEM((nh, 1, tq), F32), pltpu.VMEM((nh, MLA_V + ONES_ROWS, tq), F32),
                        pltpu.VMEM((nh, tq, tq), F32), pltpu.VMEM((nh, tq, tq), F32)],
        compiler_params=_params(("parallel", "parallel", "arbitrary")),
        name="mla",
    )(qm, km, vmt)


def _merge_body(on_ref, om_ref, ga_ref, gb_ref, x_ref, mod_ref, won_ref, wom_ref, wout_ref, o_ref):
    y = (ga_ref[0].astype(F32) * _dot(on_ref[0], won_ref[...])
         + gb_ref[0].astype(F32) * _dot(om_ref[0], wom_ref[...]))
    o_ref[0] = x_ref[0] + mod_ref[0, 2:3, :] * _dot(y.astype(BF16), wout_ref[...])


def _merge(o_nsa, o_mla, ga, gb, x, mod, w_on, w_om, w_out, tm):
    B, S, D = x.shape
    row = lambda n: pl.BlockSpec((1, tm, n), lambda b, i: (b, i, 0))
    full = lambda a: pl.BlockSpec(a.shape, lambda b, i: (0,) * a.ndim)
    return pl.pallas_call(
        _merge_body,
        grid=(B, S // tm),
        in_specs=[row(o_nsa.shape[2]), row(o_mla.shape[2]), row(D), row(D), row(D),
                  pl.BlockSpec((1, 6, D), lambda b, i: (b, 0, 0)), full(w_on), full(w_om), full(w_out)],
        out_specs=row(D),
        out_shape=jax.ShapeDtypeStruct((B, S, D), F32),
        compiler_params=_params(("parallel", "parallel")),
        name="merge",
    )(o_nsa, o_mla, ga, gb, x, mod, w_on, w_om, w_out)


def _mlp_body(x_ref, mod_ref, gmlp_ref, w1_ref, w2_ref, gfin_ref, o_ref, *, fc, final):
    x1 = x_ref[0]
    h = _rms(x1, gmlp_ref[...]) * (1.0 + mod_ref[0, 4:5, :]) + mod_ref[0, 3:4, :]
    hb = h.astype(BF16)
    acc = jnp.zeros(x1.shape, F32)
    for c in range(w1_ref.shape[1] // fc):
        a = jnp.maximum(_dot(hb, w1_ref[:, c * fc:(c + 1) * fc]), 0.0)
        acc = acc + _dot((a * a).astype(BF16), w2_ref[c * fc:(c + 1) * fc, :])
    x2 = x1 + mod_ref[0, 5:6, :] * acc
    o_ref[0] = _rms(x2, gfin_ref[...]) if final else x2


def _mlp(x1, mod, g_mlp, w1, w2, g_fin, tm, fc, final):
    B, S, D = x1.shape
    row = pl.BlockSpec((1, tm, D), lambda b, i: (b, i, 0))
    full = lambda a: pl.BlockSpec(a.shape, lambda b, i: (0,) * a.ndim)
    return pl.pallas_call(
        functools.partial(_mlp_body, fc=fc, final=final),
        grid=(B, S // tm),
        in_specs=[row, pl.BlockSpec((1, 6, D), lambda b, i: (b, 0, 0)), full(g_mlp), full(w1), full(w2),
                  full(g_fin)],
        out_specs=row,
        out_shape=jax.ShapeDtypeStruct((B, S, D), F32),
        compiler_params=_params(("parallel", "parallel")),
        name="mlp",
    )(x1, mod, g_mlp, w1, w2, g_fin)


def _dup_groups(w):
    g0, g1 = w[:, :NSA_DH], w[:, NSA_DH:]
    return jnp.concatenate([g0, g0, g1, g1], axis=1)


def _inproj_weight(w_in):
    d = w_in.shape[0]
    o = 0
    parts = {}
    for name, n in (("q", 512), ("kc", 128), ("vc", 128), ("ks", 128), ("vs", 128), ("kw", 128),
                    ("vw", 128), ("g", 24), ("qd", MLA_Q_RANK), ("kvd", MLA_KV_RANK), ("kr", MLA_ROPE),
                    ("ga", D_MODEL), ("gb", D_MODEL)):
        parts[name] = w_in[:, o:o + n]
        o += n
    wg = parts["g"].reshape(d, 3, NSA_GROUPS, NSA_HG).transpose(0, 2, 1, 3).reshape(d, NSA_GROUPS, 3 * NSA_HG)
    wg = jnp.pad(wg, ((0, 0), (0, 0), (0, GATE_ROWS - 3 * NSA_HG))).reshape(d, NSA_GROUPS * GATE_ROWS)
    wg = jnp.pad(wg, ((0, 0), (0, LANE - NSA_GROUPS * GATE_ROWS)))
    half = MLA_ROPE // 2
    x1, x2 = parts["kr"][:, :half], parts["kr"][:, half:]
    z_lo = jnp.zeros((d, MLA_NOPE), F32)
    z_hi = jnp.zeros((d, LANE - MLA_NOPE - MLA_ROPE), F32)
    kra = jnp.concatenate([z_lo, x1, x2, z_hi], axis=1)
    krb = jnp.concatenate([z_lo, -x2, x1, z_hi], axis=1)
    w_all = jnp.concatenate(
        [parts["q"], parts["kc"], parts["vc"], _dup_groups(parts["ks"]), parts["vs"],
         _dup_groups(parts["kw"]), parts["vw"], wg, parts["qd"], parts["kvd"], kra, krb,
         parts["ga"], parts["gb"]], axis=1)
    assert w_all.shape[1] == IN_COLS
    return w_all.astype(BF16)


def _mla_q_weight(w_uq):
    r = w_uq.shape[0]
    w = w_uq.reshape(r, MLA_HEADS, MLA_NOPE + MLA_ROPE)
    half = MLA_ROPE // 2
    nope, x1, x2 = w[..., :MLA_NOPE], w[..., MLA_NOPE:MLA_NOPE + half], w[..., MLA_NOPE + half:]
    z_hi = jnp.zeros((r, MLA_HEADS, LANE - MLA_NOPE - MLA_ROPE), F32)
    a = jnp.concatenate([nope, x1, x2, z_hi], axis=-1).reshape(r, MLA_HEADS * LANE)
    b = jnp.concatenate([jnp.zeros_like(nope), -x2, x1, z_hi], axis=-1).reshape(r, MLA_HEADS * LANE)
    return jnp.concatenate([a, b], axis=1).astype(BF16)


def _mla_kv_weight(w_uk, w_uv):
    r = w_uk.shape[0]
    k = w_uk.reshape(r, MLA_HEADS, MLA_NOPE)
    k = jnp.pad(k, ((0, 0), (0, 0), (0, LANE - MLA_NOPE))).reshape(r, MLA_HEADS * LANE)
    return jnp.concatenate([k, w_uv], axis=1).astype(BF16)


def _rope_tables(seq):
    half = MLA_ROPE // 2
    pos = jnp.arange(seq, dtype=F32)
    inv_freq = ROPE_THETA ** (-jnp.arange(half, dtype=F32) / half)
    ang = pos[:, None] * inv_freq[None, :]
    cos, sin = jnp.cos(ang), jnp.sin(ang)
    ones = jnp.ones((seq, MLA_NOPE), F32)
    z_lo = jnp.zeros((seq, MLA_NOPE), F32)
    z_hi = jnp.zeros((seq, LANE - MLA_NOPE - MLA_ROPE), F32)
    return (jnp.concatenate([ones, cos, cos, z_hi], axis=1),
            jnp.concatenate([z_lo, sin, sin, z_hi], axis=1))


def _compress_weights(pe, w1, w2):
    hid = w1.shape[1]
    wa = w1[:CMP_STRIDE * NSA_DH].reshape(CMP_STRIDE, NSA_DH, hid)
    wb = w1[CMP_STRIDE * NSA_DH:].reshape(CMP_STRIDE, NSA_DH, hid)
    z = jnp.zeros_like(wa)
    cols = []
    for g in range(NSA_GROUPS):
        for w in (wa, wb):
            slots = [z, z]
            slots[g] = w
            cols.append(jnp.stack(slots, axis=1).reshape(CMP_STRIDE * NSA_GROUPS * NSA_DH, hid))
    w1big = jnp.concatenate(cols, axis=1).astype(BF16)
    pe8 = jnp.zeros((8, CMP_LEN * NSA_DH), F32).at[0].set(pe.reshape(-1))
    w2d = jnp.concatenate([w2, w2], axis=1).astype(BF16)
    return w1big, pe8, w1, w2d


def _overlap_table(seq):
    ns = seq // SEL_BLOCK
    nsp = -(-ns // LANE) * LANE
    ncp = seq // CMP_STRIDE
    s_id = jnp.arange(nsp)[:, None]
    c_id = jnp.arange(ncp)[None, :]
    c_start = c_id * CMP_STRIDE
    return ((c_start < s_id * SEL_BLOCK + SEL_BLOCK) & (c_start + CMP_LEN - 1 >= s_id * SEL_BLOCK)
            & (c_id < ncp - 1)).astype(BF16)


def kernel(x, c, w_ada, b_ada, g_mix, w_in, pe_ck, w_ck1, w_ck2, pe_cv, w_cv1, w_cv2, g_cq, w_uq, g_ckv,
           w_uk, w_uv, w_o_nsa, w_o_mla, w_out, g_mlp, w_fc1, w_fc2, g_final):
    B, S, D = x.shape
    depth = w_ada.shape[0]
    tm_in = min(512, S)
    tm_out = min(512, S)
    tq_nsa = min(256, S)
    tk_sel = min(256, S)
    tq_mla = min(512, S)
    cs, sn = _rope_tables(S)
    ovt = _overlap_table(S)
    gmat = (jnp.arange(ovt.shape[0])[:, None] // (tk_sel // SEL_BLOCK) == jnp.arange(LANE)[None, :]).astype(BF16)
    n_chunks = S // CMP_STRIDE
    for layer in range(depth):
        mod = _ada(c, w_ada[layer], b_ada[layer])
        (q, kc_in, vc_in, ksd, vst, kwd, vwt, gt, qm, km, vmt, ga, gb) = _inproj(
            x, mod, g_mix[layer][None], _inproj_weight(w_in[layer]), g_cq[layer][None],
            _mla_q_weight(w_uq[layer]), g_ckv[layer][None], _mla_kv_weight(w_uk[layer], w_uv[layer]),
            cs, sn, tm_in)
        w1k, pek, w1kf, w2k = _compress_weights(pe_ck[layer], w_ck1[layer], w_ck2[layer])
        w1v, pev, w1vf, w2v = _compress_weights(pe_cv[layer], w_cv1[layer], w_cv2[layer])
        width = CMP_STRIDE * NSA_GROUPS * NSA_DH
        kcd, vct = _compress(kc_in.reshape(B, n_chunks, width), vc_in.reshape(B, n_chunks, width),
                             w1k, w1v, pek, pev, w1kf, w1vf, w2k, w2v)
        ocw, sel, cnt = _nsa_cw(q, kcd, vct, kwd, vwt, gt, ovt, gmat, tq_nsa)
        flags = (cnt[:, :, :, 0, :S // tk_sel] > 0.5).astype(jnp.int32).reshape(-1)
        o_nsa = _nsa_sel(flags, q, ksd, vst, sel, gt, ocw, tq_nsa, tk_sel)
        o_mla = _mla(qm, km, vmt, tq_mla, 4)
        x1 = _merge(o_nsa, o_mla, ga, gb, x, mod, w_o_nsa[layer].astype(BF16), w_o_mla[layer].astype(BF16),
                    w_out[layer].astype(BF16), tm_out)
        x = _mlp(x1, mod, g_mlp[layer][None], w_fc1[layer].astype(BF16), w_fc2[layer].astype(BF16),
                 g_final[None], tm_out, 1024, layer == depth - 1)
    return x
```

```python
import functools
import math

import jax
import jax.numpy as jnp
from jax import lax
from jax.experimental import pallas as pl
from jax.experimental.pallas import tpu as pltpu

F32 = jnp.float32
BF16 = jnp.bfloat16

D_MODEL = 1024
NSA_GROUPS = 2
NSA_HG = 4
NSA_DH = 64
CMP_LEN = 32
CMP_STRIDE = 16
CMP_HIDDEN = 256
SEL_BLOCK = 64
SEL_TOPK = 16
WINDOW = 512
MLA_HEADS = 8
MLA_NOPE = 64
MLA_ROPE = 32
MLA_V = 64
MLA_Q_RANK = 384
MLA_KV_RANK = 256
ROPE_THETA = 10000.0
D_FF = 4 * D_MODEL
EPS = 1e-6
FORCE_BONUS = 1000.0

LANE = 128
HALF = LANE // 2
NEG = -1e30
LOG2E = math.log2(math.e)
VMEM_LIMIT = 52 * 1024 * 1024
GATE_ROWS = 16
ONES_ROWS = 16
MXU_COLS = 256
OFF_Q = 0
OFF_KC_VC = 512
OFF_KS_KW = 768
OFF_VS_VW = 1024
OFF_G_KR = 1280
OFF_QD = 1536
OFF_KVD = 1920
OFF_GA = 2176
OFF_GB = 3200
IN_COLS = 4224


def _dot(a, b):
    return jnp.dot(a, b, preferred_element_type=F32)


def _dot_nt(a, b):
    return lax.dot_general(a, b, (((1,), (1,)), ((), ())), preferred_element_type=F32)


def _rms(x, g):
    return x * lax.rsqrt(jnp.mean(x * x, axis=-1, keepdims=True) + EPS) * g


def _sigmoid(x):
    return 1.0 / (1.0 + jnp.exp(-x))


def _params(sem):
    return pltpu.CompilerParams(dimension_semantics=sem, vmem_limit_bytes=VMEM_LIMIT)


def _ada_body(c_ref, w_ref, b_ref, o_ref):
    o_ref[...] = jnp.dot(c_ref[...], w_ref[...], preferred_element_type=F32,
                         precision=lax.Precision.HIGHEST) + b_ref[...]


def _ada(c, w, b):
    nb, d = c.shape
    n = w.shape[1]
    rows = 8
    cp = jnp.zeros((rows, d), F32).at[:nb].set(c)
    tn = 1536
    out = pl.pallas_call(
        _ada_body,
        grid=(n // tn,),
        in_specs=[pl.BlockSpec((rows, d), lambda j: (0, 0)),
                  pl.BlockSpec((d, tn), lambda j: (0, j)),
                  pl.BlockSpec((1, tn), lambda j: (0, j))],
        out_specs=pl.BlockSpec((rows, tn), lambda j: (0, j)),
        out_shape=jax.ShapeDtypeStruct((rows, n), F32),
        compiler_params=_params(("arbitrary",)),
        name="ada",
    )(cp, w, b.reshape(1, n))
    return out[:nb].reshape(nb, 6, d)


def _inproj_body(x_ref, mod_ref, gmix_ref, w_ref, gcq_ref, wq2_ref, gckv_ref, wkv_ref,
                 cs_ref, sn_ref,
                 q_ref, kc_ref, vc_ref, ks_ref, vst_ref, kw_ref, vwt_ref, gt_ref,
                 qm_ref, km_ref, vmt_ref, ga_ref, gb_ref):
    x = x_ref[0]
    h = _rms(x, gmix_ref[...]) * (1.0 + mod_ref[0, 1:2, :]) + mod_ref[0, 0:1, :]
    hb = h.astype(BF16)

    def proj(lo, n):
        return _dot(hb, w_ref[:, lo:lo + n])

    lane = lax.broadcasted_iota(jnp.int32, (x.shape[0], LANE), 1)
    half = MLA_ROPE // 2

    def dup_groups(z):
        swapped = pltpu.roll(z, HALF, 1)
        lower = lane < HALF
        return jnp.concatenate([jnp.where(lower, z, swapped), jnp.where(lower, swapped, z)], axis=1)

    def rotary(a):
        partner = jnp.where(lane < MLA_NOPE + half, -pltpu.roll(a, LANE - half, 1), pltpu.roll(a, half, 1))
        return a * cs_ref[...] + partner * sn_ref[...]

    q_ref[0] = (proj(OFF_Q, 512) * (NSA_DH ** -0.5 * LOG2E)).astype(BF16)
    z = proj(OFF_KC_VC, MXU_COLS)
    kc_ref[0] = z[:, 0:LANE].astype(BF16)
    vc_ref[0] = z[:, LANE:].astype(BF16)
    z = proj(OFF_KS_KW, MXU_COLS)
    ks_ref[0] = dup_groups(z[:, 0:LANE]).astype(BF16)
    kw_ref[0] = dup_groups(z[:, LANE:]).astype(BF16)
    z = proj(OFF_VS_VW, MXU_COLS)
    vst_ref[0] = z[:, 0:LANE].T.astype(BF16)
    vwt_ref[0] = z[:, LANE:].T.astype(BF16)
    z = proj(OFF_G_KR, MXU_COLS)
    gt_ref[0] = _sigmoid(z[:, 0:LANE]).T[0:NSA_GROUPS * GATE_ROWS]
    kr = rotary(z[:, LANE:])
    ga_ref[0] = _sigmoid(proj(OFF_GA, 1024)).astype(BF16)
    gb_ref[0] = _sigmoid(proj(OFF_GB, 1024)).astype(BF16)

    nh = MLA_HEADS * LANE
    cq = _rms(proj(OFF_QD, MLA_Q_RANK), gcq_ref[...]).astype(BF16)
    q2 = _dot(cq, wq2_ref[...])
    scale = (MLA_NOPE + MLA_ROPE) ** -0.5 * LOG2E
    for hh in range(MLA_HEADS):
        qm_ref[0, :, LANE * hh:LANE * (hh + 1)] = (rotary(q2[:, LANE * hh:LANE * (hh + 1)]) * scale).astype(BF16)
    ckv = _rms(proj(OFF_KVD, MLA_KV_RANK), gckv_ref[...]).astype(BF16)
    kv = _dot(ckv, wkv_ref[...])
    for hh in range(MLA_HEADS):
        km_ref[0, :, LANE * hh:LANE * (hh + 1)] = (kv[:, LANE * hh:LANE * (hh + 1)] + kr).astype(BF16)
    vmt_ref[0] = kv[:, nh:].T.astype(BF16)


def _inproj(x, mod, g_mix, w_all, g_cq, wq2, g_ckv, wkv, cs, sn, tm):
    B, S, D = x.shape
    row = lambda n: pl.BlockSpec((1, tm, n), lambda b, i: (b, i, 0))
    col = lambda n: pl.BlockSpec((1, n, tm), lambda b, i: (b, 0, i))
    full = lambda a: pl.BlockSpec(a.shape, lambda b, i: (0,) * a.ndim)
    tok = lambda n, dt: (row(n), jax.ShapeDtypeStruct((B, S, n), dt))
    tr = lambda n, dt: (col(n), jax.ShapeDtypeStruct((B, n, S), dt))
    outs = [tok(512, BF16), tok(128, BF16), tok(128, BF16), tok(256, BF16), tr(LANE, BF16),
            tok(256, BF16), tr(LANE, BF16), tr(NSA_GROUPS * GATE_ROWS, F32),
            tok(1024, BF16), tok(1024, BF16), tr(MLA_HEADS * MLA_V, BF16), tok(1024, BF16), tok(1024, BF16)]
    return pl.pallas_call(
        _inproj_body,
        grid=(B, S // tm),
        in_specs=[row(D),
                  pl.BlockSpec((1, 6, D), lambda b, i: (b, 0, 0)),
                  full(g_mix), full(w_all), full(g_cq), full(wq2), full(g_ckv), full(wkv),
                  pl.BlockSpec((tm, LANE), lambda b, i: (i, 0)),
                  pl.BlockSpec((tm, LANE), lambda b, i: (i, 0))],
        out_specs=[o[0] for o in outs],
        out_shape=[o[1] for o in outs],
        compiler_params=_params(("parallel", "parallel")),
        name="inproj",
    )(x, mod, g_mix, w_all, g_cq, wq2, g_ckv, wkv, cs, sn)


def _compress_one(x, w1big, pe, w1f, w2d):
    hcat = _dot(x, w1big)
    const = jnp.dot(pe, w1f, preferred_element_type=F32, precision=lax.Precision.HIGHEST)[0:1]
    n = x.shape[0]
    rowid = lax.broadcasted_iota(jnp.int32, (n, LANE), 0)
    hid_w = CMP_HIDDEN
    outs = []
    for g in range(NSA_GROUPS):
        a = hcat[:, 2 * hid_w * g:2 * hid_w * g + hid_w]
        b = hcat[:, 2 * hid_w * g + hid_w:2 * hid_w * (g + 1)]
        hid = a + pltpu.roll(b, n - 1, 0) + const
        act = hid * _sigmoid(hid)
        out = _dot(act.astype(BF16), w2d)
        outs.append(jnp.where(rowid < n - 1, out, 0.0))
    return outs


def _compress_body(xk_ref, xv_ref, w1k_ref, w1v_ref, pek_ref, pev_ref, w1kf_ref, w1vf_ref,
                   w2k_ref, w2v_ref, kc_ref, vct_ref):
    kc = _compress_one(xk_ref[0], w1k_ref[...], pek_ref[...], w1kf_ref[...], w2k_ref[...])
    vc = _compress_one(xv_ref[0], w1v_ref[...], pev_ref[...], w1vf_ref[...], w2v_ref[...])
    for g in range(NSA_GROUPS):
        kc_ref[0, g] = kc[g].astype(BF16)
        vct_ref[0, g] = vc[g].T[0:NSA_DH].astype(BF16)


def _compress(xk, xv, w1k, w1v, pek, pev, w1kf, w1vf, w2k, w2v):
    B, n, width = xk.shape
    full = lambda a: pl.BlockSpec(a.shape, lambda b: (0,) * a.ndim)
    xs = pl.BlockSpec((1, n, width), lambda b: (b, 0, 0))
    return pl.pallas_call(
        _compress_body,
        grid=(B,),
        in_specs=[xs, xs, full(w1k), full(w1v), full(pek), full(pev), full(w1kf), full(w1vf),
                  full(w2k), full(w2v)],
        out_specs=[pl.BlockSpec((1, NSA_GROUPS, n, LANE), lambda b: (b, 0, 0, 0)),
                   pl.BlockSpec((1, NSA_GROUPS, NSA_DH, n), lambda b: (b, 0, 0, 0))],
        out_shape=[jax.ShapeDtypeStruct((B, NSA_GROUPS, n, LANE), BF16),
                   jax.ShapeDtypeStruct((B, NSA_GROUPS, NSA_DH, n), BF16)],
        compiler_params=_params(("parallel",)),
        name="compress",
    )(xk, xv, w1k, w1v, pek, pev, w1kf, w1vf, w2k, w2v)


def _stack_heads(q_ref, tq):
    upper = lax.broadcasted_iota(jnp.int32, (tq, LANE), 1) >= HALF
    parts = []
    for hh in range(NSA_HG):
        slab = q_ref[0, :, LANE * (hh // 2):LANE * (hh // 2 + 1)]
        keep = upper if hh % 2 else jnp.logical_not(upper)
        parts.append(jnp.where(keep, slab, jnp.zeros_like(slab)))
    return jnp.concatenate(parts, axis=0)


def _slope(g, hh):
    return jnp.where(g == 0, 1.0, 2.0 ** -NSA_HG).astype(F32) * (2.0 ** -(hh + 1) * LOG2E)


def _with_ones(vt):
    return jnp.concatenate([vt, jnp.ones((ONES_ROWS, vt.shape[1]), vt.dtype)], axis=0)


def _nsa_cw_body(q_ref, kc_ref, vct_ref, kw_ref, vwt_ref, gt_ref, ovt_ref, gm_ref,
                 ocw_ref, sel_ref, cnt_ref, bc_sc, bw_sc, qst_sc, *, tq, tw, seq, n_parts):
    g = pl.program_id(1)
    i = pl.program_id(2)
    t0 = i * tq
    qst_sc[...] = _stack_heads(q_ref, tq)
    gates = gt_ref[0]
    ncp = kc_ref.shape[2]
    nsp = ovt_ref.shape[0]
    wk = WINDOW + tw

    @pl.when(i == 0)
    def _():
        c_end_f = (lax.broadcasted_iota(jnp.int32, (ncp, tq), 0) * CMP_STRIDE + (CMP_LEN - 1)).astype(F32)
        key_f = lax.broadcasted_iota(jnp.int32, (wk, tw), 0).astype(F32)
        for hh in range(NSA_HG):
            bc_sc[hh] = _slope(g, hh) * c_end_f
            bw_sc[hh] = _slope(g, hh) * key_f

    def tile_body(ncp_e, nsp_e):
        vct = _with_ones(vct_ref[0, 0, :, 0:ncp_e])
        st = _dot_nt(kc_ref[0, 0, 0:ncp_e, :], qst_sc[...])
        t_c = t0 + lax.broadcasted_iota(jnp.int32, (ncp_e, tq), 1)
        c_end = lax.broadcasted_iota(jnp.int32, (ncp_e, tq), 0) * CMP_STRIDE + (CMP_LEN - 1)
        madd_c = jnp.where(t_c >= c_end, 0.0, NEG)
        psum = jnp.zeros((ncp_e, tq), F32)
        o_cmp = []
        for hh in range(NSA_HG):
            z = st[:, hh * tq:(hh + 1) * tq] + bc_sc[hh, 0:ncp_e] + madd_c
            m = jnp.max(z, axis=0, keepdims=True)
            e = jnp.exp2(z - m)
            o_aug = _dot(vct, e.astype(BF16))
            r = jnp.where(m > 0.5 * NEG, 1.0 / jnp.maximum(o_aug[NSA_DH:NSA_DH + 1], 1e-30), 0.0)
            psum = psum + e * r
            o_cmp.append(o_aug[0:NSA_DH] * r)

        ovt = ovt_ref[0:nsp_e, 0:ncp_e]
        p1 = psum.astype(BF16)
        r1 = psum - p1.astype(F32)
        p2 = r1.astype(BF16)
        p3 = (r1 - p2.astype(F32)).astype(BF16)
        p_slc = _dot(ovt, p1) + _dot(ovt, p2) + _dot(ovt, p3)

        n_forced = 3
        sels = []
        for c in range(tq // tw):
            blk = lax.broadcasted_iota(jnp.int32, (nsp_e, tw), 0)
            cur = (t0 + c * tw + lax.broadcasted_iota(jnp.int32, (nsp_e, tw), 1)) // SEL_BLOCK
            valid = blk <= cur
            forced = (blk == 0) | (blk >= cur - 1)
            score = jnp.where(valid & jnp.logical_not(forced), p_slc[:, c * tw:(c + 1) * tw], -1.0)
            for _ in range(max(min(SEL_TOPK, seq // SEL_BLOCK) - n_forced, 0)):
                m = jnp.max(score, axis=0, keepdims=True)
                first = jnp.min(jnp.where(score == m, blk, nsp), axis=0, keepdims=True)
                score = jnp.where(blk == first, -1.0, score)
            sels.append(jnp.where(valid & (score < 0.0), 1.0, 0.0))
        sel = jnp.concatenate(sels, axis=1)
        sel_ref[0, 0, 0:nsp_e] = sel
        if nsp_e < nsp:
            sel_ref[0, 0, nsp_e:nsp] = jnp.zeros((nsp - nsp_e, tq), F32)
        per_blk = _dot_nt(jnp.ones((8, tq), BF16), sel.astype(BF16))
        cnt_ref[0, 0, 0] = _dot(per_blk.astype(BF16), gm_ref[0:nsp_e])

        for sub in range(tq // tw):
            k_start = pl.multiple_of(jnp.clip(t0 + sub * tw - WINDOW, 0, seq - wk), LANE)
            kw = kw_ref[0, pl.ds(k_start, wk), :]
            vwt = _with_ones(vwt_ref[0, :, pl.ds(k_start, wk)])
            qsub = jnp.concatenate(
                [qst_sc[hh * tq + sub * tw:hh * tq + (sub + 1) * tw] for hh in range(NSA_HG)], axis=0)
            st = _dot_nt(kw, qsub)
            t_w = t0 + sub * tw + lax.broadcasted_iota(jnp.int32, (wk, tw), 1)
            dist_i = t_w - (k_start + lax.broadcasted_iota(jnp.int32, (wk, tw), 0))
            madd_w = jnp.where((dist_i >= 0) & (dist_i < WINDOW), 0.0, NEG)
            cols = slice(sub * tw, (sub + 1) * tw)
            for hh in range(NSA_HG):
                z = st[:, hh * tw:(hh + 1) * tw] + bw_sc[hh] + madd_w
                e = jnp.exp2(z - jnp.max(z, axis=0, keepdims=True))
                o_aug = _dot(vwt, e.astype(BF16))
                o_win = o_aug[0:NSA_DH] * (1.0 / jnp.maximum(o_aug[NSA_DH:NSA_DH + 1], 1e-30))
                ocw_ref[0, 0, NSA_DH * hh:NSA_DH * (hh + 1), cols] = (
                    gates[hh:hh + 1, cols] * o_cmp[hh][:, cols]
                    + gates[2 * NSA_HG + hh:2 * NSA_HG + hh + 1, cols] * o_win)

    part = (i * n_parts) // pl.num_programs(2)
    for v in range(n_parts):
        pl.when(part == v)(functools.partial(tile_body, ncp * (v + 1) // n_parts, nsp * (v + 1) // n_parts))


def _cw_parts(nq, ncp, nsp):
    for n in (4, 2):
        if nq % n == 0 and (ncp // n) % LANE == 0 and (nsp // n) % 8 == 0:
            return n
    return 1


def _nsa_cw(q, kcd, vct, kwd, vwt, gt, ovt, gmat, tq):
    B, S, _ = q.shape
    ncp = kcd.shape[2]
    nsp = ovt.shape[0]
    nq = S // tq
    tw = min(LANE, tq)
    return pl.pallas_call(
        functools.partial(_nsa_cw_body, tq=tq, tw=tw, seq=S, n_parts=_cw_parts(nq, ncp, nsp)),
        grid=(B, NSA_GROUPS, nq),
        in_specs=[pl.BlockSpec((1, tq, 2 * LANE), lambda b, g, i: (b, i, g)),
                  pl.BlockSpec((1, 1, ncp, LANE), lambda b, g, i: (b, g, 0, 0)),
                  pl.BlockSpec((1, 1, NSA_DH, ncp), lambda b, g, i: (b, g, 0, 0)),
                  pl.BlockSpec((1, S, LANE), lambda b, g, i: (b, 0, g)),
                  pl.BlockSpec((1, NSA_DH, S), lambda b, g, i: (b, g, 0)),
                  pl.BlockSpec((1, GATE_ROWS, tq), lambda b, g, i: (b, g, i)),
                  pl.BlockSpec(ovt.shape, lambda b, g, i: (0, 0)),
                  pl.BlockSpec(gmat.shape, lambda b, g, i: (0, 0))],
        out_specs=[pl.BlockSpec((1, 1, NSA_HG * NSA_DH, tq), lambda b, g, i: (b, g, 0, i)),
                   pl.BlockSpec((1, 1, nsp, tq), lambda b, g, i: (b, g, 0, i)),
                   pl.BlockSpec((1, 1, 1, 8, LANE), lambda b, g, i: (b, g, i, 0, 0))],
        out_shape=[jax.ShapeDtypeStruct((B, NSA_GROUPS, NSA_HG * NSA_DH, S), F32),
                   jax.ShapeDtypeStruct((B, NSA_GROUPS, nsp, S), F32),
                   jax.ShapeDtypeStruct((B, NSA_GROUPS, nq, 8, LANE), F32)],
        scratch_shapes=[pltpu.VMEM((NSA_HG, ncp, tq), F32), pltpu.VMEM((NSA_HG, WINDOW + tw, tw), F32),
                        pltpu.VMEM((NSA_HG * tq, LANE), BF16)],
        compiler_params=_params(("arbitrary", "arbitrary", "arbitrary")),
        name="nsa_cw",
    )(q, kcd, vct, kwd, vwt, gt, ovt, gmat)


def _nsa_sel_body(flag_ref, q_ref, ks_ref, vst_ref, sel_ref, gt_ref, ocw_ref, o_ref,
                  m_sc, acc_sc, skb_sc, qst_sc, sa_sc, sb_sc, lst_sc, *, tq, tk, n_tiles):
    g = pl.program_id(1)
    i = pl.program_id(2)
    t0 = i * tq
    j_diag = t0 // tk

    @pl.when(i == 0)
    def _():
        key_f = lax.broadcasted_iota(jnp.int32, (tk, tq), 0).astype(F32)
        for hh in range(NSA_HG):
            skb_sc[hh] = _slope(g, hh) * key_f

    qst_sc[...] = _stack_heads(q_ref, tq)
    m_sc[...] = jnp.full(m_sc.shape, NEG, F32)
    acc_sc[...] = jnp.zeros(acc_sc.shape, F32)

    base = ((pl.program_id(0) * NSA_GROUPS + g) * pl.num_programs(2) + i) * n_tiles
    n_act = jnp.int32(0)
    for j in range(n_tiles):
        lst_sc[n_act] = j
        n_act = n_act + jnp.where((flag_ref[base + j] > 0) & (j < j_diag), 1, 0)
    lst_sc[n_act] = j_diag

    def scores(j, s_ref):
        k0 = pl.multiple_of(j * tk, tk)
        s_ref[...] = _dot_nt(ks_ref[0, pl.ds(k0, tk), :], qst_sc[...])

    def consume(j, s_ref, diagonal):
        k0 = pl.multiple_of(j * tk, tk)
        vt = _with_ones(vst_ref[0, :, pl.ds(k0, tk)])
        b0 = j * (tk // SEL_BLOCK)
        pieces = []
        for r in range(tk // SEL_BLOCK):
            chosen = sel_ref[0, 0, pl.ds(b0 + r, 1), :]
            pieces.append(jnp.broadcast_to(jnp.where(chosen > 0.5, 0.0, NEG), (SEL_BLOCK, tq)))
        madd = jnp.concatenate(pieces, axis=0)
        if diagonal:
            key_i = lax.broadcasted_iota(jnp.int32, (tk, tq), 0)
            qry_i = lax.broadcasted_iota(jnp.int32, (tk, tq), 1)
            madd = jnp.where(key_i + k0 <= qry_i + t0, madd, NEG)
        for hh in range(NSA_HG):
            c = _slope(g, hh) * (k0 - t0).astype(F32)
            z = s_ref[:, hh * tq:(hh + 1) * tq] + skb_sc[hh] + madd
            m_prev = m_sc[hh]
            m_new = jnp.maximum(m_prev, jnp.max(z, axis=0, keepdims=True) + c)
            alpha = jnp.exp2(m_prev - m_new)
            p = jnp.exp2(z - (m_new - c))
            acc_sc[hh] = alpha * acc_sc[hh] + _dot(vt, p.astype(BF16))
            m_sc[hh] = m_new

    scores(lst_sc[0], sa_sc)

    def pair(jj, carry):
        it = 2 * jj
        scores(lst_sc[it + 1], sb_sc)
        consume(lst_sc[it], sa_sc, False)
        scores(lst_sc[it + 2], sa_sc)
        consume(lst_sc[it + 1], sb_sc, False)
        return carry

    lax.fori_loop(0, n_act // 2, pair, 0)

    @pl.when(n_act % 2 == 0)
    def _():
        consume(j_diag, sa_sc, True)

    @pl.when(n_act % 2 == 1)
    def _():
        scores(j_diag, sb_sc)
        consume(lst_sc[n_act - 1], sa_sc, False)
        consume(j_diag, sb_sc, True)

    gates = gt_ref[0]
    for pair_i in range(NSA_HG // 2):
        halves = []
        for hh in (2 * pair_i, 2 * pair_i + 1):
            acc = acc_sc[hh]
            w = gates[NSA_HG + hh:NSA_HG + hh + 1] / jnp.maximum(acc[NSA_DH:NSA_DH + 1], 1e-30)
            halves.append(acc[0:NSA_DH] * w + ocw_ref[0, 0, NSA_DH * hh:NSA_DH * (hh + 1), :])
        o_ref[0, :, LANE * pair_i:LANE * (pair_i + 1)] = jnp.concatenate(halves, axis=0).T.astype(BF16)


def _nsa_sel(flags, q, ksd, vst, sel, gt, ocw, tq, tk):
    B, S, _ = q.shape
    nsp = sel.shape[2]
    n_tiles = S // tk
    qs = pl.BlockSpec((1, tq, 2 * LANE), lambda b, g, i, f: (b, i, g))
    return pl.pallas_call(
        functools.partial(_nsa_sel_body, tq=tq, tk=tk, n_tiles=n_tiles),
        grid_spec=pltpu.PrefetchScalarGridSpec(
            num_scalar_prefetch=1,
            grid=(B, NSA_GROUPS, S // tq),
            in_specs=[qs,
                      pl.BlockSpec((1, S, LANE), lambda b, g, i, f: (b, 0, g)),
                      pl.BlockSpec((1, NSA_DH, S), lambda b, g, i, f: (b, g, 0)),
                      pl.BlockSpec((1, 1, nsp, tq), lambda b, g, i, f: (b, g, 0, i)),
                      pl.BlockSpec((1, GATE_ROWS, tq), lambda b, g, i, f: (b, g, i)),
                      pl.BlockSpec((1, 1, NSA_HG * NSA_DH, tq), lambda b, g, i, f: (b, g, 0, i))],
            out_specs=qs,
            scratch_shapes=[pltpu.VMEM((NSA_HG, 1, tq), F32),
                            pltpu.VMEM((NSA_HG, NSA_DH + ONES_ROWS, tq), F32),
                            pltpu.VMEM((NSA_HG, tk, tq), F32),
                            pltpu.VMEM((NSA_HG * tq, LANE), BF16),
                            pltpu.VMEM((tk, NSA_HG * tq), F32),
                            pltpu.VMEM((tk, NSA_HG * tq), F32),
                            pltpu.SMEM((n_tiles + 1,), jnp.int32)]),
        out_shape=jax.ShapeDtypeStruct((B, S, 4 * LANE), BF16),
        compiler_params=_params(("arbitrary", "arbitrary", "arbitrary")),
        name="nsa_sel",
    )(flags, q, ksd, vst, sel, gt, ocw)


def _mla_body(q_ref, k_ref, vt_ref, o_ref, m_sc, acc_sc, sa_sc, sb_sc, *, tq, tc, nh):
    n_full = pl.program_id(2)
    m_sc[...] = jnp.full(m_sc.shape, NEG, F32)
    acc_sc[...] = jnp.zeros(acc_sc.shape, F32)

    def scores(j, s_ref):
        k0 = pl.multiple_of(j * tq, tq)
        for hh in range(nh):
            lanes = slice(LANE * hh, LANE * (hh + 1))
            s_ref[hh] = _dot_nt(k_ref[0, pl.ds(k0, tq), lanes], q_ref[0, :, lanes])

    def consume(j, s_ref, diagonal):
        k0 = pl.multiple_of(j * tq, tq)
        for hh in range(nh):
            vt = _with_ones(vt_ref[0, MLA_V * hh:MLA_V * (hh + 1), pl.ds(k0, tq)])
            for c in range(tq // tc):
                cols = slice(c * tc, (c + 1) * tc)
                st = s_ref[hh, :, cols]
                if diagonal:
                    causal = (lax.broadcasted_iota(jnp.int32, (tq, tc), 0)
                              <= lax.broadcasted_iota(jnp.int32, (tq, tc), 1) + c * tc)
                    st = jnp.where(causal, st, NEG)
                m_prev = m_sc[hh, :, cols]
                m_new = jnp.maximum(m_prev, jnp.max(st, axis=0, keepdims=True))
                alpha = jnp.exp2(m_prev - m_new)
                p = jnp.exp2(st - m_new)
                acc_sc[hh, :, cols] = alpha * acc_sc[hh, :, cols] + _dot(vt, p.astype(BF16))
                m_sc[hh, :, cols] = m_new

    scores(0, sa_sc)

    def pair(jj, carry):
        j = 2 * jj
        scores(j + 1, sb_sc)
        consume(j, sa_sc, False)
        scores(j + 2, sa_sc)
        consume(j + 1, sb_sc, False)
        return carry

    lax.fori_loop(0, n_full // 2, pair, 0)

    @pl.when(n_full % 2 == 0)
    def _():
        consume(n_full, sa_sc, True)

    @pl.when(n_full % 2 == 1)
    def _():
        scores(n_full, sb_sc)
        consume(n_full - 1, sa_sc, False)
        consume(n_full, sb_sc, True)

    for pair_i in range(nh // 2):
        halves = []
        for hh in (2 * pair_i, 2 * pair_i + 1):
            acc = acc_sc[hh]
            halves.append(acc[0:MLA_V] * (1.0 / jnp.maximum(acc[MLA_V:MLA_V + 1], 1e-30)))
        o_ref[0, :, LANE * pair_i:LANE * (pair_i + 1)] = jnp.concatenate(halves, axis=0).T.astype(BF16)


def _mla(qm, km, vmt, tq, nh):
    B, S, _ = qm.shape
    return pl.pallas_call(
        functools.partial(_mla_body, tq=tq, tc=min(256, tq), nh=nh),
        grid=(B, MLA_HEADS // nh, S // tq),
        in_specs=[pl.BlockSpec((1, tq, nh * LANE), lambda b, p, i: (b, i, p)),
                  pl.BlockSpec((1, S, nh * LANE), lambda b, p, i: (b, 0, p)),
                  pl.BlockSpec((1, nh * MLA_V, S), lambda b, p, i: (b, p, 0))],
        out_specs=pl.BlockSpec((1, tq, nh * MLA_V), lambda b, p, i: (b, i, p)),
        out_shape=jax.ShapeDtypeStruct((B, S, MLA_HEADS * MLA_V), BF16),
        scratch_shapes=[pltpu.VMEM((nh, 1, tq), F32), pltpu.VMEM((nh, MLA_V + ONES_ROWS, tq), F32),
                        pltpu.VMEM((nh, tq, tq), F32), pltpu.VMEM((nh, tq, tq), F32)],
        compiler_params=_params(("parallel", "parallel", "arbitrary")),
        name="mla",
    )(qm, km, vmt)


def _merge_body(on_ref, om_ref, ga_ref, gb_ref, x_ref, mod_ref, won_ref, wom_ref, wout_ref, o_ref):
    y = (ga_ref[0].astype(F32) * _dot(on_ref[0], won_ref[...])
         + gb_ref[0].astype(F32) * _dot(om_ref[0], wom_ref[...]))
    o_ref[0] = x_ref[0] + mod_ref[0, 2:3, :] * _dot(y.astype(BF16), wout_ref[...])


def _merge(o_nsa, o_mla, ga, gb, x, mod, w_on, w_om, w_out, tm):
    B, S, D = x.shape
    row = lambda n: pl.BlockSpec((1, tm, n), lambda b, i: (b, i, 0))
    full = lambda a: pl.BlockSpec(a.shape, lambda b, i: (0,) * a.ndim)
    return pl.pallas_call(
        _merge_body,
        grid=(B, S // tm),
        in_specs=[row(o_nsa.shape[2]), row(o_mla.shape[2]), row(D), row(D), row(D),
                  pl.BlockSpec((1, 6, D), lambda b, i: (b, 0, 0)), full(w_on), full(w_om), full(w_out)],
        out_specs=row(D),
        out_shape=jax.ShapeDtypeStruct((B, S, D), F32),
        compiler_params=_params(("parallel", "parallel")),
        name="merge",
    )(o_nsa, o_mla, ga, gb, x, mod, w_on, w_om, w_out)


def _mlp_body(x_ref, mod_ref, gmlp_ref, w1_ref, w2_ref, gfin_ref, o_ref, *, fc, final):
    x1 = x_ref[0]
    h = _rms(x1, gmlp_ref[...]) * (1.0 + mod_ref[0, 4:5, :]) + mod_ref[0, 3:4, :]
    hb = h.astype(BF16)
    acc = jnp.zeros(x1.shape, F32)
    for c in range(w1_ref.shape[1] // fc):
        a = jnp.maximum(_dot(hb, w1_ref[:, c * fc:(c + 1) * fc]), 0.0)
        acc = acc + _dot((a * a).astype(BF16), w2_ref[c * fc:(c + 1) * fc, :])
    x2 = x1 + mod_ref[0, 5:6, :] * acc
    o_ref[0] = _rms(x2, gfin_ref[...]) if final else x2


def _mlp(x1, mod, g_mlp, w1, w2, g_fin, tm, fc, final):
    B, S, D = x1.shape
    row = pl.BlockSpec((1, tm, D), lambda b, i: (b, i, 0))
    full = lambda a: pl.BlockSpec(a.shape, lambda b, i: (0,) * a.ndim)
    return pl.pallas_call(
        functools.partial(_mlp_body, fc=fc, final=final),
        grid=(B, S // tm),
        in_specs=[row, pl.BlockSpec((1, 6, D), lambda b, i: (b, 0, 0)), full(g_mlp), full(w1), full(w2),
                  full(g_fin)],
        out_specs=row,
        out_shape=jax.ShapeDtypeStruct((B, S, D), F32),
        compiler_params=_params(("parallel", "parallel")),
        name="mlp",
    )(x1, mod, g_mlp, w1, w2, g_fin)


def _inproj_weight(w_in):
    d = w_in.shape[0]
    o = 0
    parts = {}
    for name, n in (("q", 512), ("kc", 128), ("vc", 128), ("ks", 128), ("vs", 128), ("kw", 128),
                    ("vw", 128), ("g", 24), ("qd", MLA_Q_RANK), ("kvd", MLA_KV_RANK), ("kr", MLA_ROPE),
                    ("ga", D_MODEL), ("gb", D_MODEL)):
        parts[name] = w_in[:, o:o + n]
        o += n
    wg = parts["g"].reshape(d, 3, NSA_GROUPS, NSA_HG).transpose(0, 2, 1, 3).reshape(d, NSA_GROUPS, 3 * NSA_HG)
    wg = jnp.pad(wg, ((0, 0), (0, 0), (0, GATE_ROWS - 3 * NSA_HG))).reshape(d, NSA_GROUPS * GATE_ROWS)
    wg = jnp.pad(wg, ((0, 0), (0, LANE - NSA_GROUPS * GATE_ROWS)))
    kr = jnp.pad(parts["kr"], ((0, 0), (MLA_NOPE, LANE - MLA_NOPE - MLA_ROPE)))
    w_all = jnp.concatenate(
        [p.astype(BF16) for p in
         (parts["q"], parts["kc"], parts["vc"], parts["ks"], parts["kw"], parts["vs"], parts["vw"], wg, kr,
          parts["qd"], parts["kvd"], parts["ga"], parts["gb"])], axis=1)
    assert w_all.shape[1] == IN_COLS
    return w_all


def _mla_q_weight(w_uq):
    r = w_uq.shape[0]
    w = w_uq.reshape(r, MLA_HEADS, MLA_NOPE + MLA_ROPE)
    w = jnp.pad(w, ((0, 0), (0, 0), (0, LANE - MLA_NOPE - MLA_ROPE)))
    return w.reshape(r, MLA_HEADS * LANE).astype(BF16)


def _mla_kv_weight(w_uk, w_uv):
    r = w_uk.shape[0]
    k = w_uk.reshape(r, MLA_HEADS, MLA_NOPE)
    k = jnp.pad(k, ((0, 0), (0, 0), (0, LANE - MLA_NOPE))).reshape(r, MLA_HEADS * LANE)
    return jnp.concatenate([k, w_uv], axis=1).astype(BF16)


def _rope_tables(seq):
    half = MLA_ROPE // 2
    pos = jnp.arange(seq, dtype=F32)
    inv_freq = ROPE_THETA ** (-jnp.arange(half, dtype=F32) / half)
    ang = pos[:, None] * inv_freq[None, :]
    cos, sin = jnp.cos(ang), jnp.sin(ang)
    ones = jnp.ones((seq, MLA_NOPE), F32)
    z_lo = jnp.zeros((seq, MLA_NOPE), F32)
    z_hi = jnp.zeros((seq, LANE - MLA_NOPE - MLA_ROPE), F32)
    return (jnp.concatenate([ones, cos, cos, z_hi], axis=1),
            jnp.concatenate([z_lo, sin, sin, z_hi], axis=1))


def _compress_weights(pe, w1, w2):
    hid = w1.shape[1]
    wa = w1[:CMP_STRIDE * NSA_DH].reshape(CMP_STRIDE, NSA_DH, hid)
    wb = w1[CMP_STRIDE * NSA_DH:].reshape(CMP_STRIDE, NSA_DH, hid)
    z = jnp.zeros_like(wa)
    cols = []
    for g in range(NSA_GROUPS):
        for w in (wa, wb):
            slots = [z, z]
            slots[g] = w
            cols.append(jnp.stack(slots, axis=1).reshape(CMP_STRIDE * NSA_GROUPS * NSA_DH, hid))
    w1big = jnp.concatenate(cols, axis=1).astype(BF16)
    pe8 = jnp.zeros((8, CMP_LEN * NSA_DH), F32).at[0].set(pe.reshape(-1))
    w2d = jnp.concatenate([w2, w2], axis=1).astype(BF16)
    return w1big, pe8, w1, w2d


def _overlap_table(seq):
    ns = seq // SEL_BLOCK
    nsp = -(-ns // LANE) * LANE
    ncp = seq // CMP_STRIDE
    s_id = jnp.arange(nsp)[:, None]
    c_id = jnp.arange(ncp)[None, :]
    c_start = c_id * CMP_STRIDE
    return ((c_start < s_id * SEL_BLOCK + SEL_BLOCK) & (c_start + CMP_LEN - 1 >= s_id * SEL_BLOCK)
            & (c_id < ncp - 1)).astype(BF16)


def kernel(x, c, w_ada, b_ada, g_mix, w_in, pe_ck, w_ck1, w_ck2, pe_cv, w_cv1, w_cv2, g_cq, w_uq, g_ckv,
           w_uk, w_uv, w_o_nsa, w_o_mla, w_out, g_mlp, w_fc1, w_fc2, g_final):
    B, S, D = x.shape
    depth = w_ada.shape[0]
    tm_in = min(512, S)
    tm_out = min(512, S)
    tq_nsa = min(256, S)
    tk_sel = min(256, S)
    tq_mla = min(512, S)
    cs, sn = _rope_tables(S)
    ovt = _overlap_table(S)
    gmat = (jnp.arange(ovt.shape[0])[:, None] // (tk_sel // SEL_BLOCK) == jnp.arange(LANE)[None, :]).astype(BF16)
    n_chunks = S // CMP_STRIDE
    for layer in range(depth):
        mod = _ada(c, w_ada[layer], b_ada[layer])
        (q, kc_in, vc_in, ksd, vst, kwd, vwt, gt, qm, km, vmt, ga, gb) = _inproj(
            x, mod, g_mix[layer][None], _inproj_weight(w_in[layer]), g_cq[layer][None],
            _mla_q_weight(w_uq[layer]), g_ckv[layer][None], _mla_kv_weight(w_uk[layer], w_uv[layer]),
            cs, sn, tm_in)
        w1k, pek, w1kf, w2k = _compress_weights(pe_ck[layer], w_ck1[layer], w_ck2[layer])
        w1v, pev, w1vf, w2v = _compress_weights(pe_cv[layer], w_cv1[layer], w_cv2[layer])
        width = CMP_STRIDE * NSA_GROUPS * NSA_DH
        kcd, vct = _compress(kc_in.reshape(B, n_chunks, width), vc_in.reshape(B, n_chunks, width),
                             w1k, w1v, pek, pev, w1kf, w1vf, w2k, w2v)
        ocw, sel, cnt = _nsa_cw(q, kcd, vct, kwd, vwt, gt, ovt, gmat, tq_nsa)
        flags = (cnt[:, :, :, 0, :S // tk_sel] > 0.5).astype(jnp.int32).reshape(-1)
        o_nsa = _nsa_sel(flags, q, ksd, vst, sel, gt, ocw, tq_nsa, tk_sel)
        o_mla = _mla(qm, km, vmt, tq_mla, 4)
        x1 = _merge(o_nsa, o_mla, ga, gb, x, mod, w_o_nsa[layer].astype(BF16), w_o_mla[layer].astype(BF16),
                    w_out[layer].astype(BF16), tm_out)
        x = _mlp(x1, mod, g_mlp[layer][None], w_fc1[layer].astype(BF16), w_fc2[layer].astype(BF16),
                 g_final[None], tm_out, 1024, layer == depth - 1)
    return x
```

```python
import functools
import math

import jax
import jax.numpy as jnp
from jax import lax
from jax.experimental import pallas as pl
from jax.experimental.pallas import tpu as pltpu

F32 = jnp.float32
BF16 = jnp.bfloat16

D_MODEL = 1024
NSA_GROUPS = 2
NSA_HG = 4
NSA_DH = 64
CMP_LEN = 32
CMP_STRIDE = 16
CMP_HIDDEN = 256
SEL_BLOCK = 64
SEL_TOPK = 16
WINDOW = 512
MLA_HEADS = 8
MLA_NOPE = 64
MLA_ROPE = 32
MLA_V = 64
MLA_Q_RANK = 384
MLA_KV_RANK = 256
ROPE_THETA = 10000.0
D_FF = 4 * D_MODEL
EPS = 1e-6
FORCE_BONUS = 1000.0

LANE = 128
HALF = LANE // 2
NEG = -1e30
LOG2E = math.log2(math.e)
VMEM_LIMIT = 52 * 1024 * 1024
GATE_ROWS = 16
ONES_ROWS = 16
MXU_COLS = 256
OFF_Q = 0
OFF_KC_VC = 512
OFF_KS_KW = 768
OFF_VS_VW = 1024
OFF_G_KR = 1280
OFF_QD = 1536
OFF_KVD = 1920
OFF_GA = 2176
OFF_GB = 3200
IN_COLS = 4224


def _dot(a, b):
    return jnp.dot(a, b, preferred_element_type=F32)


def _dot_nt(a, b):
    return lax.dot_general(a, b, (((1,), (1,)), ((), ())), preferred_element_type=F32)


def _rms(x, g):
    return x * lax.rsqrt(jnp.mean(x * x, axis=-1, keepdims=True) + EPS) * g


def _sigmoid(x):
    return 1.0 / (1.0 + jnp.exp(-x))


def _params(sem):
    return pltpu.CompilerParams(dimension_semantics=sem, vmem_limit_bytes=VMEM_LIMIT)


def _ada_body(c_ref, w_ref, b_ref, o_ref):
    o_ref[...] = jnp.dot(c_ref[...], w_ref[...], preferred_element_type=F32,
                         precision=lax.Precision.HIGHEST) + b_ref[...]


def _ada(c, w, b):
    nb, d = c.shape
    n = w.shape[1]
    rows = 8
    cp = jnp.zeros((rows, d), F32).at[:nb].set(c)
    tn = 1536
    out = pl.pallas_call(
        _ada_body,
        grid=(n // tn,),
        in_specs=[pl.BlockSpec((rows, d), lambda j: (0, 0)),
                  pl.BlockSpec((d, tn), lambda j: (0, j)),
                  pl.BlockSpec((1, tn), lambda j: (0, j))],
        out_specs=pl.BlockSpec((rows, tn), lambda j: (0, j)),
        out_shape=jax.ShapeDtypeStruct((rows, n), F32),
        compiler_params=_params(("arbitrary",)),
        name="ada",
    )(cp, w, b.reshape(1, n))
    return out[:nb].reshape(nb, 6, d)


def _inproj_body(x_ref, mod_ref, gmix_ref, w_ref, gcq_ref, wq2_ref, gckv_ref, wkv_ref,
                 cs_ref, sn_ref,
                 q_ref, kc_ref, vc_ref, ks_ref, vst_ref, kw_ref, vwt_ref, gt_ref,
                 qm_ref, km_ref, vmt_ref, ga_ref, gb_ref):
    x = x_ref[0]
    h = _rms(x, gmix_ref[...]) * (1.0 + mod_ref[0, 1:2, :]) + mod_ref[0, 0:1, :]
    hb = h.astype(BF16)

    def proj(lo, n):
        return _dot(hb, w_ref[:, lo:lo + n])

    lane = lax.broadcasted_iota(jnp.int32, (x.shape[0], LANE), 1)
    half = MLA_ROPE // 2

    def dup_groups(z):
        swapped = pltpu.roll(z, HALF, 1)
        lower = lane < HALF
        return jnp.concatenate([jnp.where(lower, z, swapped), jnp.where(lower, swapped, z)], axis=1)

    def rotary(a):
        partner = jnp.where(lane < MLA_NOPE + half, -pltpu.roll(a, LANE - half, 1), pltpu.roll(a, half, 1))
        return a * cs_ref[...] + partner * sn_ref[...]

    q_ref[0] = (proj(OFF_Q, 512) * (NSA_DH ** -0.5 * LOG2E)).astype(BF16)
    z = proj(OFF_KC_VC, MXU_COLS)
    kc_ref[0] = z[:, 0:LANE].astype(BF16)
    vc_ref[0] = z[:, LANE:].astype(BF16)
    z = proj(OFF_KS_KW, MXU_COLS)
    ks_ref[0] = dup_groups(z[:, 0:LANE]).astype(BF16)
    kw_ref[0] = dup_groups(z[:, LANE:]).astype(BF16)
    z = proj(OFF_VS_VW, MXU_COLS)
    vst_ref[0] = z[:, 0:LANE].T.astype(BF16)
    vwt_ref[0] = z[:, LANE:].T.astype(BF16)
    z = proj(OFF_G_KR, MXU_COLS)
    gt_ref[0] = _sigmoid(z[:, 0:LANE]).T[0:NSA_GROUPS * GATE_ROWS]
    kr = rotary(z[:, LANE:])
    ga_ref[0] = _sigmoid(proj(OFF_GA, 1024)).astype(BF16)
    gb_ref[0] = _sigmoid(proj(OFF_GB, 1024)).astype(BF16)

    nh = MLA_HEADS * LANE
    cq = _rms(proj(OFF_QD, MLA_Q_RANK), gcq_ref[...]).astype(BF16)
    q2 = _dot(cq, wq2_ref[...])
    scale = (MLA_NOPE + MLA_ROPE) ** -0.5 * LOG2E
    for hh in range(MLA_HEADS):
        qm_ref[0, :, LANE * hh:LANE * (hh + 1)] = (rotary(q2[:, LANE * hh:LANE * (hh + 1)]) * scale).astype(BF16)
    ckv = _rms(proj(OFF_KVD, MLA_KV_RANK), gckv_ref[...]).astype(BF16)
    kv = _dot(ckv, wkv_ref[...])
    for hh in range(MLA_HEADS):
        km_ref[0, :, LANE * hh:LANE * (hh + 1)] = (kv[:, LANE * hh:LANE * (hh + 1)] + kr).astype(BF16)
    vmt_ref[0] = kv[:, nh:].T.astype(BF16)


def _inproj(x, mod, g_mix, w_all, g_cq, wq2, g_ckv, wkv, cs, sn, tm):
    B, S, D = x.shape
    row = lambda n: pl.BlockSpec((1, tm, n), lambda b, i: (b, i, 0))
    col = lambda n: pl.BlockSpec((1, n, tm), lambda b, i: (b, 0, i))
    full = lambda a: pl.BlockSpec(a.shape, lambda b, i: (0,) * a.ndim)
    tok = lambda n, dt: (row(n), jax.ShapeDtypeStruct((B, S, n), dt))
    tr = lambda n, dt: (col(n), jax.ShapeDtypeStruct((B, n, S), dt))
    outs = [tok(512, BF16), tok(128, BF16), tok(128, BF16), tok(256, BF16), tr(LANE, BF16),
            tok(256, BF16), tr(LANE, BF16), tr(NSA_GROUPS * GATE_ROWS, F32),
            tok(1024, BF16), tok(1024, BF16), tr(MLA_HEADS * MLA_V, BF16), tok(1024, BF16), tok(1024, BF16)]
    return pl.pallas_call(
        _inproj_body,
        grid=(B, S // tm),
        in_specs=[row(D),
                  pl.BlockSpec((1, 6, D), lambda b, i: (b, 0, 0)),
                  full(g_mix), full(w_all), full(g_cq), full(wq2), full(g_ckv), full(wkv),
                  pl.BlockSpec((tm, LANE), lambda b, i: (i, 0)),
                  pl.BlockSpec((tm, LANE), lambda b, i: (i, 0))],
        out_specs=[o[0] for o in outs],
        out_shape=[o[1] for o in outs],
        compiler_params=_params(("parallel", "parallel")),
        name="inproj",
    )(x, mod, g_mix, w_all, g_cq, wq2, g_ckv, wkv, cs, sn)


def _compress_one(x, w1big, pe, w1f, w2d):
    hcat = _dot(x, w1big)
    const = jnp.dot(pe, w1f, preferred_element_type=F32, precision=lax.Precision.HIGHEST)[0:1]
    n = x.shape[0]
    rowid = lax.broadcasted_iota(jnp.int32, (n, LANE), 0)
    hid_w = CMP_HIDDEN
    outs = []
    for g in range(NSA_GROUPS):
        a = hcat[:, 2 * hid_w * g:2 * hid_w * g + hid_w]
        b = hcat[:, 2 * hid_w * g + hid_w:2 * hid_w * (g + 1)]
        hid = a + pltpu.roll(b, n - 1, 0) + const
        act = hid * _sigmoid(hid)
        out = _dot(act.astype(BF16), w2d)
        outs.append(jnp.where(rowid < n - 1, out, 0.0))
    return outs


def _compress_body(xk_ref, xv_ref, w1k_ref, w1v_ref, pek_ref, pev_ref, w1kf_ref, w1vf_ref,
                   w2k_ref, w2v_ref, kc_ref, vct_ref):
    kc = _compress_one(xk_ref[0], w1k_ref[...], pek_ref[...], w1kf_ref[...], w2k_ref[...])
    vc = _compress_one(xv_ref[0], w1v_ref[...], pev_ref[...], w1vf_ref[...], w2v_ref[...])
    for g in range(NSA_GROUPS):
        kc_ref[0, g] = kc[g].astype(BF16)
        vct_ref[0, g] = vc[g].T[0:NSA_DH].astype(BF16)


def _compress(xk, xv, w1k, w1v, pek, pev, w1kf, w1vf, w2k, w2v):
    B, n, width = xk.shape
    full = lambda a: pl.BlockSpec(a.shape, lambda b: (0,) * a.ndim)
    xs = pl.BlockSpec((1, n, width), lambda b: (b, 0, 0))
    return pl.pallas_call(
        _compress_body,
        grid=(B,),
        in_specs=[xs, xs, full(w1k), full(w1v), full(pek), full(pev), full(w1kf), full(w1vf),
                  full(w2k), full(w2v)],
        out_specs=[pl.BlockSpec((1, NSA_GROUPS, n, LANE), lambda b: (b, 0, 0, 0)),
                   pl.BlockSpec((1, NSA_GROUPS, NSA_DH, n), lambda b: (b, 0, 0, 0))],
        out_shape=[jax.ShapeDtypeStruct((B, NSA_GROUPS, n, LANE), BF16),
                   jax.ShapeDtypeStruct((B, NSA_GROUPS, NSA_DH, n), BF16)],
        compiler_params=_params(("parallel",)),
        name="compress",
    )(xk, xv, w1k, w1v, pek, pev, w1kf, w1vf, w2k, w2v)


def _stack_heads(q_ref, tq):
    upper = lax.broadcasted_iota(jnp.int32, (tq, LANE), 1) >= HALF
    parts = []
    for hh in range(NSA_HG):
        slab = q_ref[0, :, LANE * (hh // 2):LANE * (hh // 2 + 1)]
        keep = upper if hh % 2 else jnp.logical_not(upper)
        parts.append(jnp.where(keep, slab, jnp.zeros_like(slab)))
    return jnp.concatenate(parts, axis=0)


def _slope(g, hh):
    return jnp.where(g == 0, 1.0, 2.0 ** -NSA_HG).astype(F32) * (2.0 ** -(hh + 1) * LOG2E)


def _with_ones(vt):
    return jnp.concatenate([vt, jnp.ones((ONES_ROWS, vt.shape[1]), vt.dtype)], axis=0)


def _nsa_cw_body(q_ref, kc_ref, vct_ref, kw_ref, vwt_ref, gt_ref, ovt_ref, gm_ref,
                 ocw_ref, sel_ref, cnt_ref, bc_sc, bw_sc, qst_sc, *, tq, tw, tf, seq, n_parts):
    g = pl.program_id(1)
    i = pl.program_id(2)
    t0 = i * tq
    qst_sc[...] = _stack_heads(q_ref, tq)
    gates = gt_ref[0]
    ncp = kc_ref.shape[2]
    nsp = ovt_ref.shape[0]
    wk = WINDOW + tw

    @pl.when(i == 0)
    def _():
        c_end_f = (lax.broadcasted_iota(jnp.int32, (ncp, tq), 0) * CMP_STRIDE + (CMP_LEN - 1)).astype(F32)
        key_f = lax.broadcasted_iota(jnp.int32, (wk, tw), 0).astype(F32)
        for hh in range(NSA_HG):
            bc_sc[hh] = _slope(g, hh) * c_end_f
            bw_sc[hh] = _slope(g, hh) * key_f

    def tile_body(ncp_e, nsp_e):
        vct = _with_ones(vct_ref[0, 0, :, 0:ncp_e])
        st = _dot_nt(kc_ref[0, 0, 0:ncp_e, :], qst_sc[...])
        t_c = t0 + lax.broadcasted_iota(jnp.int32, (ncp_e, tq), 1)
        c_end = lax.broadcasted_iota(jnp.int32, (ncp_e, tq), 0) * CMP_STRIDE + (CMP_LEN - 1)
        madd_c = jnp.where(t_c >= c_end, 0.0, NEG)
        psum = jnp.zeros((ncp_e, tq), F32)
        o_cmp = []
        for hh in range(NSA_HG):
            z = st[:, hh * tq:(hh + 1) * tq] + bc_sc[hh, 0:ncp_e] + madd_c
            m = jnp.max(z, axis=0, keepdims=True)
            e = jnp.exp2(z - m)
            o_aug = _dot(vct, e.astype(BF16))
            r = jnp.where(m > 0.5 * NEG, 1.0 / jnp.maximum(o_aug[NSA_DH:NSA_DH + 1], 1e-30), 0.0)
            psum = psum + e * r
            o_cmp.append(o_aug[0:NSA_DH] * r)

        ovt = ovt_ref[0:nsp_e, 0:ncp_e]
        p1 = psum.astype(BF16)
        r1 = psum - p1.astype(F32)
        p2 = r1.astype(BF16)
        p3 = (r1 - p2.astype(F32)).astype(BF16)
        p_slc = _dot(ovt, p1) + _dot(ovt, p2) + _dot(ovt, p3)

        n_forced = 3
        sels = []
        for c in range(tq // tw):
            blk = lax.broadcasted_iota(jnp.int32, (nsp_e, tw), 0)
            cur = (t0 + c * tw + lax.broadcasted_iota(jnp.int32, (nsp_e, tw), 1)) // SEL_BLOCK
            valid = blk <= cur
            forced = (blk == 0) | (blk >= cur - 1)
            score = jnp.where(valid & jnp.logical_not(forced), p_slc[:, c * tw:(c + 1) * tw], -1.0)
            for _ in range(max(min(SEL_TOPK, seq // SEL_BLOCK) - n_forced, 0)):
                m = jnp.max(score, axis=0, keepdims=True)
                first = jnp.min(jnp.where(score == m, blk, nsp), axis=0, keepdims=True)
                score = jnp.where(blk == first, -1.0, score)
            sels.append(jnp.where(valid & (score < 0.0), 1.0, 0.0))
        sel = jnp.concatenate(sels, axis=1)
        sel_ref[0, 0, 0:nsp_e] = sel
        if nsp_e < nsp:
            sel_ref[0, 0, nsp_e:nsp] = jnp.zeros((nsp - nsp_e, tq), F32)
        for f in range(tq // tf):
            per_blk = _dot_nt(jnp.ones((8, tf), BF16), sel[:, f * tf:(f + 1) * tf].astype(BF16))
            cnt_ref[0, 0, f] = _dot(per_blk.astype(BF16), gm_ref[0:nsp_e])

        for sub in range(tq // tw):
            k_start = pl.multiple_of(jnp.clip(t0 + sub * tw - WINDOW, 0, seq - wk), LANE)
            kw = kw_ref[0, pl.ds(k_start, wk), :]
            vwt = _with_ones(vwt_ref[0, :, pl.ds(k_start, wk)])
            qsub = jnp.concatenate(
                [qst_sc[hh * tq + sub * tw:hh * tq + (sub + 1) * tw] for hh in range(NSA_HG)], axis=0)
            st = _dot_nt(kw, qsub)
            t_w = t0 + sub * tw + lax.broadcasted_iota(jnp.int32, (wk, tw), 1)
            dist_i = t_w - (k_start + lax.broadcasted_iota(jnp.int32, (wk, tw), 0))
            madd_w = jnp.where((dist_i >= 0) & (dist_i < WINDOW), 0.0, NEG)
            cols = slice(sub * tw, (sub + 1) * tw)
            for hh in range(NSA_HG):
                z = st[:, hh * tw:(hh + 1) * tw] + bw_sc[hh] + madd_w
                e = jnp.exp2(z - jnp.max(z, axis=0, keepdims=True))
                o_aug = _dot(vwt, e.astype(BF16))
                o_win = o_aug[0:NSA_DH] * (1.0 / jnp.maximum(o_aug[NSA_DH:NSA_DH + 1], 1e-30))
                ocw_ref[0, 0, NSA_DH * hh:NSA_DH * (hh + 1), cols] = (
                    gates[hh:hh + 1, cols] * o_cmp[hh][:, cols]
                    + gates[2 * NSA_HG + hh:2 * NSA_HG + hh + 1, cols] * o_win)

    part = (i * n_parts) // pl.num_programs(2)
    for v in range(n_parts):
        pl.when(part == v)(functools.partial(tile_body, ncp * (v + 1) // n_parts, nsp * (v + 1) // n_parts))


def _cw_parts(nq, ncp, nsp):
    for n in (4, 2):
        if nq % n == 0 and (ncp // n) % LANE == 0 and (nsp // n) % 8 == 0:
            return n
    return 1


def _nsa_cw(q, kcd, vct, kwd, vwt, gt, ovt, gmat, tq, tf):
    B, S, _ = q.shape
    ncp = kcd.shape[2]
    nsp = ovt.shape[0]
    nq = S // tq
    tw = min(LANE, tq)
    return pl.pallas_call(
        functools.partial(_nsa_cw_body, tq=tq, tw=tw, tf=tf, seq=S, n_parts=_cw_parts(nq, ncp, nsp)),
        grid=(B, NSA_GROUPS, nq),
        in_specs=[pl.BlockSpec((1, tq, 2 * LANE), lambda b, g, i: (b, i, g)),
                  pl.BlockSpec((1, 1, ncp, LANE), lambda b, g, i: (b, g, 0, 0)),
                  pl.BlockSpec((1, 1, NSA_DH, ncp), lambda b, g, i: (b, g, 0, 0)),
                  pl.BlockSpec((1, S, LANE), lambda b, g, i: (b, 0, g)),
                  pl.BlockSpec((1, NSA_DH, S), lambda b, g, i: (b, g, 0)),
                  pl.BlockSpec((1, GATE_ROWS, tq), lambda b, g, i: (b, g, i)),
                  pl.BlockSpec(ovt.shape, lambda b, g, i: (0, 0)),
                  pl.BlockSpec(gmat.shape, lambda b, g, i: (0, 0))],
        out_specs=[pl.BlockSpec((1, 1, NSA_HG * NSA_DH, tq), lambda b, g, i: (b, g, 0, i)),
                   pl.BlockSpec((1, 1, nsp, tq), lambda b, g, i: (b, g, 0, i)),
                   pl.BlockSpec((1, 1, tq // tf, 8, LANE), lambda b, g, i: (b, g, i, 0, 0))],
        out_shape=[jax.ShapeDtypeStruct((B, NSA_GROUPS, NSA_HG * NSA_DH, S), F32),
                   jax.ShapeDtypeStruct((B, NSA_GROUPS, nsp, S), F32),
                   jax.ShapeDtypeStruct((B, NSA_GROUPS, S // tf, 8, LANE), F32)],
        scratch_shapes=[pltpu.VMEM((NSA_HG, ncp, tq), F32), pltpu.VMEM((NSA_HG, WINDOW + tw, tw), F32),
                        pltpu.VMEM((NSA_HG * tq, LANE), BF16)],
        compiler_params=_params(("arbitrary", "arbitrary", "arbitrary")),
        name="nsa_cw",
    )(q, kcd, vct, kwd, vwt, gt, ovt, gmat)


def _nsa_sel_body(flag_ref, q_ref, ks_ref, vst_ref, sel_ref, gt_ref, ocw_ref, o_ref,
                  m_sc, acc_sc, skb_sc, qst_sc, sa_sc, sb_sc, lst_sc, *, tq, tk, n_tiles):
    g = pl.program_id(1)
    i = pl.program_id(2)
    t0 = i * tq
    j_diag = t0 // tk

    @pl.when(i == 0)
    def _():
        key_f = lax.broadcasted_iota(jnp.int32, (tk, tq), 0).astype(F32)
        for hh in range(NSA_HG):
            skb_sc[hh] = _slope(g, hh) * key_f

    qst_sc[...] = _stack_heads(q_ref, tq)
    m_sc[...] = jnp.full(m_sc.shape, NEG, F32)
    acc_sc[...] = jnp.zeros(acc_sc.shape, F32)

    base = ((pl.program_id(0) * NSA_GROUPS + g) * pl.num_programs(2) + i) * n_tiles
    def add_tile(j, n):
        lst_sc[n] = j
        return n + jnp.where(flag_ref[base + j] > 0, 1, 0)

    n_act = lax.fori_loop(0, j_diag, add_tile, jnp.int32(0))
    lst_sc[n_act] = j_diag

    def scores(j, s_ref):
        k0 = pl.multiple_of(j * tk, tk)
        s_ref[...] = _dot_nt(ks_ref[0, pl.ds(k0, tk), :], qst_sc[...])

    def consume(j, s_ref, diagonal):
        k0 = pl.multiple_of(j * tk, tk)
        vt = _with_ones(vst_ref[0, :, pl.ds(k0, tk)])
        b0 = j * (tk // SEL_BLOCK)
        pieces = []
        for r in range(tk // SEL_BLOCK):
            chosen = sel_ref[0, 0, pl.ds(b0 + r, 1), :]
            pieces.append(jnp.broadcast_to(jnp.where(chosen > 0.5, 0.0, NEG), (SEL_BLOCK, tq)))
        madd = jnp.concatenate(pieces, axis=0)
        if diagonal:
            key_i = lax.broadcasted_iota(jnp.int32, (tk, tq), 0)
            qry_i = lax.broadcasted_iota(jnp.int32, (tk, tq), 1)
            madd = jnp.where(key_i + k0 <= qry_i + t0, madd, NEG)
        for hh in range(NSA_HG):
            c = _slope(g, hh) * (k0 - t0).astype(F32)
            z = s_ref[:, hh * tq:(hh + 1) * tq] + skb_sc[hh] + madd
            m_prev = m_sc[hh]
            m_new = jnp.maximum(m_prev, jnp.max(z, axis=0, keepdims=True) + c)
            alpha = jnp.exp2(m_prev - m_new)
            p = jnp.exp2(z - (m_new - c))
            acc_sc[hh] = alpha * acc_sc[hh] + _dot(vt, p.astype(BF16))
            m_sc[hh] = m_new

    scores(lst_sc[0], sa_sc)

    def pair(jj, carry):
        it = 2 * jj
        scores(lst_sc[it + 1], sb_sc)
        consume(lst_sc[it], sa_sc, False)
        scores(lst_sc[it + 2], sa_sc)
        consume(lst_sc[it + 1], sb_sc, False)
        return carry

    lax.fori_loop(0, n_act // 2, pair, 0)

    @pl.when(n_act % 2 == 0)
    def _():
        consume(j_diag, sa_sc, True)

    @pl.when(n_act % 2 == 1)
    def _():
        scores(j_diag, sb_sc)
        consume(lst_sc[n_act - 1], sa_sc, False)
        consume(j_diag, sb_sc, True)

    gates = gt_ref[0]
    for pair_i in range(NSA_HG // 2):
        halves = []
        for hh in (2 * pair_i, 2 * pair_i + 1):
            acc = acc_sc[hh]
            w = gates[NSA_HG + hh:NSA_HG + hh + 1] / jnp.maximum(acc[NSA_DH:NSA_DH + 1], 1e-30)
            halves.append(acc[0:NSA_DH] * w + ocw_ref[0, 0, NSA_DH * hh:NSA_DH * (hh + 1), :])
        o_ref[0, :, LANE * pair_i:LANE * (pair_i + 1)] = jnp.concatenate(halves, axis=0).T.astype(BF16)


def _nsa_sel(flags, q, ksd, vst, sel, gt, ocw, tq, tk):
    B, S, _ = q.shape
    nsp = sel.shape[2]
    n_tiles = S // tk
    assert tk % tq == 0, "a query tile must lie inside one (the diagonal) key tile"
    qs = pl.BlockSpec((1, tq, 2 * LANE), lambda b, g, i, f: (b, i, g))
    return pl.pallas_call(
        functools.partial(_nsa_sel_body, tq=tq, tk=tk, n_tiles=n_tiles),
        grid_spec=pltpu.PrefetchScalarGridSpec(
            num_scalar_prefetch=1,
            grid=(B, NSA_GROUPS, S // tq),
            in_specs=[qs,
                      pl.BlockSpec((1, S, LANE), lambda b, g, i, f: (b, 0, g)),
                      pl.BlockSpec((1, NSA_DH, S), lambda b, g, i, f: (b, g, 0)),
                      pl.BlockSpec((1, 1, nsp, tq), lambda b, g, i, f: (b, g, 0, i)),
                      pl.BlockSpec((1, GATE_ROWS, tq), lambda b, g, i, f: (b, g, i)),
                      pl.BlockSpec((1, 1, NSA_HG * NSA_DH, tq), lambda b, g, i, f: (b, g, 0, i))],
            out_specs=qs,
            scratch_shapes=[pltpu.VMEM((NSA_HG, 1, tq), F32),
                            pltpu.VMEM((NSA_HG, NSA_DH + ONES_ROWS, tq), F32),
                            pltpu.VMEM((NSA_HG, tk, tq), F32),
                            pltpu.VMEM((NSA_HG * tq, LANE), BF16),
                            pltpu.VMEM((tk, NSA_HG * tq), F32),
                            pltpu.VMEM((tk, NSA_HG * tq), F32),
                            pltpu.SMEM((n_tiles + 1,), jnp.int32)]),
        out_shape=jax.ShapeDtypeStruct((B, S, 4 * LANE), BF16),
        compiler_params=_params(("arbitrary", "arbitrary", "arbitrary")),
        name="nsa_sel",
    )(flags, q, ksd, vst, sel, gt, ocw)


def _mla_body(q_ref, k_ref, vt_ref, o_ref, m_sc, acc_sc, sa_sc, sb_sc, *, tq, tc, nh):
    n_full = pl.program_id(2)
    m_sc[...] = jnp.full(m_sc.shape, NEG, F32)
    acc_sc[...] = jnp.zeros(acc_sc.shape, F32)

    def scores(j, s_ref):
        k0 = pl.multiple_of(j * tq, tq)
        for hh in range(nh):
            lanes = slice(LANE * hh, LANE * (hh + 1))
            s_ref[hh] = _dot_nt(k_ref[0, pl.ds(k0, tq), lanes], q_ref[0, :, lanes])

    def consume(j, s_ref, diagonal):
        k0 = pl.multiple_of(j * tq, tq)
        for hh in range(nh):
            vt = _with_ones(vt_ref[0, MLA_V * hh:MLA_V * (hh + 1), pl.ds(k0, tq)])
            for c in range(tq // tc):
                cols = slice(c * tc, (c + 1) * tc)
                st = s_ref[hh, :, cols]
                if diagonal:
                    causal = (lax.broadcasted_iota(jnp.int32, (tq, tc), 0)
                              <= lax.broadcasted_iota(jnp.int32, (tq, tc), 1) + c * tc)
                    st = jnp.where(causal, st, NEG)
                m_prev = m_sc[hh, :, cols]
                m_new = jnp.maximum(m_prev, jnp.max(st, axis=0, keepdims=True))
                alpha = jnp.exp2(m_prev - m_new)
                p = jnp.exp2(st - m_new)
                acc_sc[hh, :, cols] = alpha * acc_sc[hh, :, cols] + _dot(vt, p.astype(BF16))
                m_sc[hh, :, cols] = m_new

    scores(0, sa_sc)

    def pair(jj, carry):
        j = 2 * jj
        scores(j + 1, sb_sc)
        consume(j, sa_sc, False)
        scores(j + 2, sa_sc)
        consume(j + 1, sb_sc, False)
        return carry

    lax.fori_loop(0, n_full // 2, pair, 0)

    @pl.when(n_full % 2 == 0)
    def _():
        consume(n_full, sa_sc, True)

    @pl.when(n_full % 2 == 1)
    def _():
        scores(n_full, sb_sc)
        consume(n_full - 1, sa_sc, False)
        consume(n_full, sb_sc, True)

    for pair_i in range(nh // 2):
        halves = []
        for hh in (2 * pair_i, 2 * pair_i + 1):
            acc = acc_sc[hh]
            halves.append(acc[0:MLA_V] * (1.0 / jnp.maximum(acc[MLA_V:MLA_V + 1], 1e-30)))
        o_ref[0, :, LANE * pair_i:LANE * (pair_i + 1)] = jnp.concatenate(halves, axis=0).T.astype(BF16)


def _mla(qm, km, vmt, tq, nh):
    B, S, _ = qm.shape
    return pl.pallas_call(
        functools.partial(_mla_body, tq=tq, tc=min(256, tq), nh=nh),
        grid=(B, MLA_HEADS // nh, S // tq),
        in_specs=[pl.BlockSpec((1, tq, nh * LANE), lambda b, p, i: (b, i, p)),
                  pl.BlockSpec((1, S, nh * LANE), lambda b, p, i: (b, 0, p)),
                  pl.BlockSpec((1, nh * MLA_V, S), lambda b, p, i: (b, p, 0))],
        out_specs=pl.BlockSpec((1, tq, nh * MLA_V), lambda b, p, i: (b, i, p)),
        out_shape=jax.ShapeDtypeStruct((B, S, MLA_HEADS * MLA_V), BF16),
        scratch_shapes=[pltpu.VMEM((nh, 1, tq), F32), pltpu.VMEM((nh, MLA_V + ONES_ROWS, tq), F32),
                        pltpu.VMEM((nh, tq, tq), F32), pltpu.VMEM((nh, tq, tq), F32)],
        compiler_params=_params(("parallel", "parallel", "arbitrary")),
        name="mla",
    )(qm, km, vmt)


def _merge_body(on_ref, om_ref, ga_ref, gb_ref, x_ref, mod_ref, won_ref, wom_ref, wout_ref, o_ref):
    y = (ga_ref[0].astype(F32) * _dot(on_ref[0], won_ref[...])
         + gb_ref[0].astype(F32) * _dot(om_ref[0], wom_ref[...]))
    o_ref[0] = x_ref[0] + mod_ref[0, 2:3, :] * _dot(y.astype(BF16), wout_ref[...])


def _merge(o_nsa, o_mla, ga, gb, x, mod, w_on, w_om, w_out, tm):
    B, S, D = x.shape
    row = lambda n: pl.BlockSpec((1, tm, n), lambda b, i: (b, i, 0))
    full = lambda a: pl.BlockSpec(a.shape, lambda b, i: (0,) * a.ndim)
    return pl.pallas_call(
        _merge_body,
        grid=(B, S // tm),
        in_specs=[row(o_nsa.shape[2]), row(o_mla.shape[2]), row(D), row(D), row(D),
                  pl.BlockSpec((1, 6, D), lambda b, i: (b, 0, 0)), full(w_on), full(w_om), full(w_out)],
        out_specs=row(D),
        out_shape=jax.ShapeDtypeStruct((B, S, D), F32),
        compiler_params=_params(("parallel", "parallel")),
        name="merge",
    )(o_nsa, o_mla, ga, gb, x, mod, w_on, w_om, w_out)


def _mlp_body(x_ref, mod_ref, gmlp_ref, w1_ref, w2_ref, gfin_ref, o_ref, *, fc, final):
    x1 = x_ref[0]
    h = _rms(x1, gmlp_ref[...]) * (1.0 + mod_ref[0, 4:5, :]) + mod_ref[0, 3:4, :]
    hb = h.astype(BF16)
    acc = jnp.zeros(x1.shape, F32)
    for c in range(w1_ref.shape[1] // fc):
        a = jnp.maximum(_dot(hb, w1_ref[:, c * fc:(c + 1) * fc]), 0.0)
        acc = acc + _dot((a * a).astype(BF16), w2_ref[c * fc:(c + 1) * fc, :])
    x2 = x1 + mod_ref[0, 5:6, :] * acc
    o_ref[0] = _rms(x2, gfin_ref[...]) if final else x2


def _mlp(x1, mod, g_mlp, w1, w2, g_fin, tm, fc, final):
    B, S, D = x1.shape
    row = pl.BlockSpec((1, tm, D), lambda b, i: (b, i, 0))
    full = lambda a: pl.BlockSpec(a.shape, lambda b, i: (0,) * a.ndim)
    return pl.pallas_call(
        functools.partial(_mlp_body, fc=fc, final=final),
        grid=(B, S // tm),
        in_specs=[row, pl.BlockSpec((1, 6, D), lambda b, i: (b, 0, 0)), full(g_mlp), full(w1), full(w2),
                  full(g_fin)],
        out_specs=row,
        out_shape=jax.ShapeDtypeStruct((B, S, D), F32),
        compiler_params=_params(("parallel", "parallel")),
        name="mlp",
    )(x1, mod, g_mlp, w1, w2, g_fin)


def _inproj_weight(w_in):
    d = w_in.shape[0]
    o = 0
    parts = {}
    for name, n in (("q", 512), ("kc", 128), ("vc", 128), ("ks", 128), ("vs", 128), ("kw", 128),
                    ("vw", 128), ("g", 24), ("qd", MLA_Q_RANK), ("kvd", MLA_KV_RANK), ("kr", MLA_ROPE),
                    ("ga", D_MODEL), ("gb", D_MODEL)):
        parts[name] = w_in[:, o:o + n]
        o += n
    wg = parts["g"].reshape(d, 3, NSA_GROUPS, NSA_HG).transpose(0, 2, 1, 3).reshape(d, NSA_GROUPS, 3 * NSA_HG)
    wg = jnp.pad(wg, ((0, 0), (0, 0), (0, GATE_ROWS - 3 * NSA_HG))).reshape(d, NSA_GROUPS * GATE_ROWS)
    wg = jnp.pad(wg, ((0, 0), (0, LANE - NSA_GROUPS * GATE_ROWS)))
    kr = jnp.pad(parts["kr"], ((0, 0), (MLA_NOPE, LANE - MLA_NOPE - MLA_ROPE)))
    w_all = jnp.concatenate(
        [p.astype(BF16) for p in
         (parts["q"], parts["kc"], parts["vc"], parts["ks"], parts["kw"], parts["vs"], parts["vw"], wg, kr,
          parts["qd"], parts["kvd"], parts["ga"], parts["gb"])], axis=1)
    assert w_all.shape[1] == IN_COLS
    return w_all


def _mla_q_weight(w_uq):
    r = w_uq.shape[0]
    w = w_uq.reshape(r, MLA_HEADS, MLA_NOPE + MLA_ROPE)
    w = jnp.pad(w, ((0, 0), (0, 0), (0, LANE - MLA_NOPE - MLA_ROPE)))
    return w.reshape(r, MLA_HEADS * LANE).astype(BF16)


def _mla_kv_weight(w_uk, w_uv):
    r = w_uk.shape[0]
    k = w_uk.reshape(r, MLA_HEADS, MLA_NOPE)
    k = jnp.pad(k, ((0, 0), (0, 0), (0, LANE - MLA_NOPE))).reshape(r, MLA_HEADS * LANE)
    return jnp.concatenate([k, w_uv], axis=1).astype(BF16)


def _rope_tables(seq):
    half = MLA_ROPE // 2
    pos = jnp.arange(seq, dtype=F32)
    inv_freq = ROPE_THETA ** (-jnp.arange(half, dtype=F32) / half)
    ang = pos[:, None] * inv_freq[None, :]
    cos, sin = jnp.cos(ang), jnp.sin(ang)
    ones = jnp.ones((seq, MLA_NOPE), F32)
    z_lo = jnp.zeros((seq, MLA_NOPE), F32)
    z_hi = jnp.zeros((seq, LANE - MLA_NOPE - MLA_ROPE), F32)
    return (jnp.concatenate([ones, cos, cos, z_hi], axis=1),
            jnp.concatenate([z_lo, sin, sin, z_hi], axis=1))


def _compress_weights(pe, w1, w2):
    hid = w1.shape[1]
    wa = w1[:CMP_STRIDE * NSA_DH].reshape(CMP_STRIDE, NSA_DH, hid)
    wb = w1[CMP_STRIDE * NSA_DH:].reshape(CMP_STRIDE, NSA_DH, hid)
    z = jnp.zeros_like(wa)
    cols = []
    for g in range(NSA_GROUPS):
        for w in (wa, wb):
            slots = [z, z]
            slots[g] = w
            cols.append(jnp.stack(slots, axis=1).reshape(CMP_STRIDE * NSA_GROUPS * NSA_DH, hid))
    w1big = jnp.concatenate(cols, axis=1).astype(BF16)
    pe8 = jnp.zeros((8, CMP_LEN * NSA_DH), F32).at[0].set(pe.reshape(-1))
    w2d = jnp.concatenate([w2, w2], axis=1).astype(BF16)
    return w1big, pe8, w1, w2d


def _overlap_table(seq):
    ns = seq // SEL_BLOCK
    nsp = -(-ns // LANE) * LANE
    ncp = seq // CMP_STRIDE
    s_id = jnp.arange(nsp)[:, None]
    c_id = jnp.arange(ncp)[None, :]
    c_start = c_id * CMP_STRIDE
    return ((c_start < s_id * SEL_BLOCK + SEL_BLOCK) & (c_start + CMP_LEN - 1 >= s_id * SEL_BLOCK)
            & (c_id < ncp - 1)).astype(BF16)


def kernel(x, c, w_ada, b_ada, g_mix, w_in, pe_ck, w_ck1, w_ck2, pe_cv, w_cv1, w_cv2, g_cq, w_uq, g_ckv,
           w_uk, w_uv, w_o_nsa, w_o_mla, w_out, g_mlp, w_fc1, w_fc2, g_final):
    B, S, D = x.shape
    depth = w_ada.shape[0]
    tm_in = min(512, S)
    tm_out = min(512, S)
    tq_nsa = min(256, S)
    tq_cw = min(512, S)
    tk_sel = min(256, S)
    tq_mla = min(512, S)
    cs, sn = _rope_tables(S)
    ovt = _overlap_table(S)
    gmat = (jnp.arange(ovt.shape[0])[:, None] // (tk_sel // SEL_BLOCK) == jnp.arange(LANE)[None, :]).astype(BF16)
    n_chunks = S // CMP_STRIDE
    for layer in range(depth):
        mod = _ada(c, w_ada[layer], b_ada[layer])
        (q, kc_in, vc_in, ksd, vst, kwd, vwt, gt, qm, km, vmt, ga, gb) = _inproj(
            x, mod, g_mix[layer][None], _inproj_weight(w_in[layer]), g_cq[layer][None],
            _mla_q_weight(w_uq[layer]), g_ckv[layer][None], _mla_kv_weight(w_uk[layer], w_uv[layer]),
            cs, sn, tm_in)
        w1k, pek, w1kf, w2k = _compress_weights(pe_ck[layer], w_ck1[layer], w_ck2[layer])
        w1v, pev, w1vf, w2v = _compress_weights(pe_cv[layer], w_cv1[layer], w_cv2[layer])
        width = CMP_STRIDE * NSA_GROUPS * NSA_DH
        kcd, vct = _compress(kc_in.reshape(B, n_chunks, width), vc_in.reshape(B, n_chunks, width),
                             w1k, w1v, pek, pev, w1kf, w1vf, w2k, w2v)
        ocw, sel, cnt = _nsa_cw(q, kcd, vct, kwd, vwt, gt, ovt, gmat, tq_cw, tq_nsa)
        flags = (cnt[:, :, :, 0, :S // tk_sel] > 0.5).astype(jnp.int32).reshape(-1)
        o_nsa = _nsa_sel(flags, q, ksd, vst, sel, gt, ocw, tq_nsa, tk_sel)
        o_mla = _mla(qm, km, vmt, tq_mla, 4)
        x1 = _merge(o_nsa, o_mla, ga, gb, x, mod, w_o_nsa[layer].astype(BF16), w_o_mla[layer].astype(BF16),
                    w_out[layer].astype(BF16), tm_out)
        x = _mlp(x1, mod, g_mlp[layer][None], w_fc1[layer].astype(BF16), w_fc2[layer].astype(BF16),
                 g_final[None], tm_out, 1024, layer == depth - 1)
    return x
```

```python
import functools
import math

import jax
import jax.numpy as jnp
from jax import lax
from jax.experimental import pallas as pl
from jax.experimental.pallas import tpu as pltpu

F32 = jnp.float32
BF16 = jnp.bfloat16

D_MODEL = 1024
NSA_GROUPS = 2
NSA_HG = 4
NSA_DH = 64
CMP_LEN = 32
CMP_STRIDE = 16
CMP_HIDDEN = 256
SEL_BLOCK = 64
SEL_TOPK = 16
WINDOW = 512
MLA_HEADS = 8
MLA_NOPE = 64
MLA_ROPE = 32
MLA_V = 64
MLA_Q_RANK = 384
MLA_KV_RANK = 256
ROPE_THETA = 10000.0
D_FF = 4 * D_MODEL
EPS = 1e-6
FORCE_BONUS = 1000.0

LANE = 128
HALF = LANE // 2
NEG = -1e30
LOG2E = math.log2(math.e)
VMEM_LIMIT = 52 * 1024 * 1024
GATE_ROWS = 16
ONES_ROWS = 16
MXU_COLS = 256
OFF_Q = 0
OFF_KC_VC = 512
OFF_KS_KW = 768
OFF_VS_VW = 1024
OFF_G_KR = 1280
OFF_QD = 1536
OFF_KVD = 1920
OFF_GA = 2176
OFF_GB = 3200
IN_COLS = 4224


def _dot(a, b):
    return jnp.dot(a, b, preferred_element_type=F32)


def _dot_nt(a, b):
    return lax.dot_general(a, b, (((1,), (1,)), ((), ())), preferred_element_type=F32)


def _rms(x, g):
    return x * lax.rsqrt(jnp.mean(x * x, axis=-1, keepdims=True) + EPS) * g


def _sigmoid(x):
    return 1.0 / (1.0 + jnp.exp(-x))


def _params(sem):
    return pltpu.CompilerParams(dimension_semantics=sem, vmem_limit_bytes=VMEM_LIMIT)


def _ada_body(c_ref, w_ref, b_ref, o_ref):
    o_ref[...] = jnp.dot(c_ref[...], w_ref[...], preferred_element_type=F32,
                         precision=lax.Precision.HIGHEST) + b_ref[...]


def _ada(c, w, b):
    nb, d = c.shape
    n = w.shape[1]
    rows = 8
    cp = jnp.zeros((rows, d), F32).at[:nb].set(c)
    tn = 1536
    out = pl.pallas_call(
        _ada_body,
        grid=(n // tn,),
        in_specs=[pl.BlockSpec((rows, d), lambda j: (0, 0)),
                  pl.BlockSpec((d, tn), lambda j: (0, j)),
                  pl.BlockSpec((1, tn), lambda j: (0, j))],
        out_specs=pl.BlockSpec((rows, tn), lambda j: (0, j)),
        out_shape=jax.ShapeDtypeStruct((rows, n), F32),
        compiler_params=_params(("arbitrary",)),
        name="ada",
    )(cp, w, b.reshape(1, n))
    return out[:nb].reshape(nb, 6, d)


def _inproj_body(x_ref, mod_ref, gmix_ref, w_ref, gcq_ref, wq2_ref, gckv_ref, wkv_ref,
                 cs_ref, sn_ref,
                 q_ref, kc_ref, vc_ref, ks_ref, vst_ref, kw_ref, vwt_ref, gt_ref,
                 qm_ref, km_ref, vmt_ref, ga_ref, gb_ref):
    x = x_ref[0]
    h = _rms(x, gmix_ref[...]) * (1.0 + mod_ref[0, 1:2, :]) + mod_ref[0, 0:1, :]
    hb = h.astype(BF16)

    def proj(lo, n):
        return _dot(hb, w_ref[:, lo:lo + n])

    lane = lax.broadcasted_iota(jnp.int32, (x.shape[0], LANE), 1)
    half = MLA_ROPE // 2

    def dup_groups(z):
        swapped = pltpu.roll(z, HALF, 1)
        lower = lane < HALF
        return jnp.concatenate([jnp.where(lower, z, swapped), jnp.where(lower, swapped, z)], axis=1)

    def rotary(a):
        partner = jnp.where(lane < MLA_NOPE + half, -pltpu.roll(a, LANE - half, 1), pltpu.roll(a, half, 1))
        return a * cs_ref[...] + partner * sn_ref[...]

    q_ref[0] = (proj(OFF_Q, 512) * (NSA_DH ** -0.5 * LOG2E)).astype(BF16)
    z = proj(OFF_KC_VC, MXU_COLS)
    kc_ref[0] = z[:, 0:LANE].astype(BF16)
    vc_ref[0] = z[:, LANE:].astype(BF16)
    z = proj(OFF_KS_KW, MXU_COLS)
    ks_ref[0] = dup_groups(z[:, 0:LANE]).astype(BF16)
    kw_ref[0] = dup_groups(z[:, LANE:]).astype(BF16)
    z = proj(OFF_VS_VW, MXU_COLS)
    vst_ref[0] = z[:, 0:LANE].T.astype(BF16)
    vwt_ref[0] = z[:, LANE:].T.astype(BF16)
    z = proj(OFF_G_KR, MXU_COLS)
    gt_ref[0] = _sigmoid(z[:, 0:LANE]).T[0:NSA_GROUPS * GATE_ROWS]
    kr = rotary(z[:, LANE:])
    ga_ref[0] = _sigmoid(proj(OFF_GA, 1024)).astype(BF16)
    gb_ref[0] = _sigmoid(proj(OFF_GB, 1024)).astype(BF16)

    nh = MLA_HEADS * LANE
    cq = _rms(proj(OFF_QD, MLA_Q_RANK), gcq_ref[...]).astype(BF16)
    q2 = _dot(cq, wq2_ref[...])
    scale = (MLA_NOPE + MLA_ROPE) ** -0.5 * LOG2E
    for hh in range(MLA_HEADS):
        qm_ref[0, :, LANE * hh:LANE * (hh + 1)] = (rotary(q2[:, LANE * hh:LANE * (hh + 1)]) * scale).astype(BF16)
    ckv = _rms(proj(OFF_KVD, MLA_KV_RANK), gckv_ref[...]).astype(BF16)
    kv = _dot(ckv, wkv_ref[...])
    for hh in range(MLA_HEADS):
        km_ref[0, :, LANE * hh:LANE * (hh + 1)] = (kv[:, LANE * hh:LANE * (hh + 1)] + kr).astype(BF16)
    vmt_ref[0] = kv[:, nh:].T.astype(BF16)


def _inproj(x, mod, g_mix, w_all, g_cq, wq2, g_ckv, wkv, cs, sn, tm):
    B, S, D = x.shape
    row = lambda n: pl.BlockSpec((1, tm, n), lambda b, i: (b, i, 0))
    col = lambda n: pl.BlockSpec((1, n, tm), lambda b, i: (b, 0, i))
    full = lambda a: pl.BlockSpec(a.shape, lambda b, i: (0,) * a.ndim)
    tok = lambda n, dt: (row(n), jax.ShapeDtypeStruct((B, S, n), dt))
    tr = lambda n, dt: (col(n), jax.ShapeDtypeStruct((B, n, S), dt))
    outs = [tok(512, BF16), tok(128, BF16), tok(128, BF16), tok(256, BF16), tr(LANE, BF16),
            tok(256, BF16), tr(LANE, BF16), tr(NSA_GROUPS * GATE_ROWS, F32),
            tok(1024, BF16), tok(1024, BF16), tr(MLA_HEADS * MLA_V, BF16), tok(1024, BF16), tok(1024, BF16)]
    return pl.pallas_call(
        _inproj_body,
        grid=(B, S // tm),
        in_specs=[row(D),
                  pl.BlockSpec((1, 6, D), lambda b, i: (b, 0, 0)),
                  full(g_mix), full(w_all), full(g_cq), full(wq2), full(g_ckv), full(wkv),
                  pl.BlockSpec((tm, LANE), lambda b, i: (i, 0)),
                  pl.BlockSpec((tm, LANE), lambda b, i: (i, 0))],
        out_specs=[o[0] for o in outs],
        out_shape=[o[1] for o in outs],
        compiler_params=_params(("parallel", "parallel")),
        name="inproj",
    )(x, mod, g_mix, w_all, g_cq, wq2, g_ckv, wkv, cs, sn)


def _compress_one(x, w1big, pe, w1f, w2d):
    hcat = _dot(x, w1big)
    const = jnp.dot(pe, w1f, preferred_element_type=F32, precision=lax.Precision.HIGHEST)[0:1]
    n = x.shape[0]
    rowid = lax.broadcasted_iota(jnp.int32, (n, LANE), 0)
    hid_w = CMP_HIDDEN
    outs = []
    for g in range(NSA_GROUPS):
        a = hcat[:, 2 * hid_w * g:2 * hid_w * g + hid_w]
        b = hcat[:, 2 * hid_w * g + hid_w:2 * hid_w * (g + 1)]
        hid = a + pltpu.roll(b, n - 1, 0) + const
        act = hid * _sigmoid(hid)
        out = _dot(act.astype(BF16), w2d)
        outs.append(jnp.where(rowid < n - 1, out, 0.0))
    return outs


def _compress_body(xk_ref, xv_ref, w1k_ref, w1v_ref, pek_ref, pev_ref, w1kf_ref, w1vf_ref,
                   w2k_ref, w2v_ref, kc_ref, vct_ref):
    kc = _compress_one(xk_ref[0], w1k_ref[...], pek_ref[...], w1kf_ref[...], w2k_ref[...])
    vc = _compress_one(xv_ref[0], w1v_ref[...], pev_ref[...], w1vf_ref[...], w2v_ref[...])
    for g in range(NSA_GROUPS):
        kc_ref[0, g] = kc[g].astype(BF16)
        vct_ref[0, g] = vc[g].T[0:NSA_DH].astype(BF16)


def _compress(xk, xv, w1k, w1v, pek, pev, w1kf, w1vf, w2k, w2v):
    B, n, width = xk.shape
    full = lambda a: pl.BlockSpec(a.shape, lambda b: (0,) * a.ndim)
    xs = pl.BlockSpec((1, n, width), lambda b: (b, 0, 0))
    return pl.pallas_call(
        _compress_body,
        grid=(B,),
        in_specs=[xs, xs, full(w1k), full(w1v), full(pek), full(pev), full(w1kf), full(w1vf),
                  full(w2k), full(w2v)],
        out_specs=[pl.BlockSpec((1, NSA_GROUPS, n, LANE), lambda b: (b, 0, 0, 0)),
                   pl.BlockSpec((1, NSA_GROUPS, NSA_DH, n), lambda b: (b, 0, 0, 0))],
        out_shape=[jax.ShapeDtypeStruct((B, NSA_GROUPS, n, LANE), BF16),
                   jax.ShapeDtypeStruct((B, NSA_GROUPS, NSA_DH, n), BF16)],
        compiler_params=_params(("parallel",)),
        name="compress",
    )(xk, xv, w1k, w1v, pek, pev, w1kf, w1vf, w2k, w2v)


def _stack_heads(q_ref, tq):
    upper = lax.broadcasted_iota(jnp.int32, (tq, LANE), 1) >= HALF
    parts = []
    for hh in range(NSA_HG):
        slab = q_ref[0, :, LANE * (hh // 2):LANE * (hh // 2 + 1)]
        keep = upper if hh % 2 else jnp.logical_not(upper)
        parts.append(jnp.where(keep, slab, jnp.zeros_like(slab)))
    return jnp.concatenate(parts, axis=0)


def _slope(g, hh):
    return jnp.where(g == 0, 1.0, 2.0 ** -NSA_HG).astype(F32) * (2.0 ** -(hh + 1) * LOG2E)


def _with_ones(vt):
    return jnp.concatenate([vt, jnp.ones((ONES_ROWS, vt.shape[1]), vt.dtype)], axis=0)


def _nsa_cw_body(q_ref, kc_ref, vct_ref, kw_ref, vwt_ref, gt_ref, ovt_ref, gm_ref,
                 ocw_ref, sel_ref, cnt_ref, bc_sc, bw_sc, qst_sc, *, tq, tw, tf, seq, n_parts):
    g = pl.program_id(1)
    i = pl.program_id(2)
    t0 = i * tq
    qst_sc[...] = _stack_heads(q_ref, tq)
    gates = gt_ref[0]
    ncp = kc_ref.shape[2]
    nsp = ovt_ref.shape[0]
    wk = WINDOW + tw

    @pl.when(i == 0)
    def _():
        c_end_f = (lax.broadcasted_iota(jnp.int32, (ncp, tq), 0) * CMP_STRIDE + (CMP_LEN - 1)).astype(F32)
        key_f = lax.broadcasted_iota(jnp.int32, (wk, tw), 0).astype(F32)
        for hh in range(NSA_HG):
            bc_sc[hh] = _slope(g, hh) * c_end_f
            bw_sc[hh] = _slope(g, hh) * key_f

    def tile_body(ncp_e, nsp_e):
        vct = _with_ones(vct_ref[0, 0, :, 0:ncp_e])
        st = _dot_nt(kc_ref[0, 0, 0:ncp_e, :], qst_sc[...])
        t_c = t0 + lax.broadcasted_iota(jnp.int32, (ncp_e, tq), 1)
        c_end = lax.broadcasted_iota(jnp.int32, (ncp_e, tq), 0) * CMP_STRIDE + (CMP_LEN - 1)
        madd_c = jnp.where(t_c >= c_end, 0.0, NEG)
        psum = jnp.zeros((ncp_e, tq), F32)
        o_cmp = []
        for hh in range(NSA_HG):
            z = st[:, hh * tq:(hh + 1) * tq] + bc_sc[hh, 0:ncp_e] + madd_c
            m = jnp.max(z, axis=0, keepdims=True)
            e = jnp.exp2(z - m)
            o_aug = _dot(vct, e.astype(BF16))
            r = jnp.where(m > 0.5 * NEG, 1.0 / jnp.maximum(o_aug[NSA_DH:NSA_DH + 1], 1e-30), 0.0)
            psum = psum + e * r
            o_cmp.append(o_aug[0:NSA_DH] * r)

        ovt = ovt_ref[0:nsp_e, 0:ncp_e]
        p1 = psum.astype(BF16)
        r1 = psum - p1.astype(F32)
        p2 = r1.astype(BF16)
        p3 = (r1 - p2.astype(F32)).astype(BF16)
        p_slc = _dot(ovt, p1) + _dot(ovt, p2) + _dot(ovt, p3)

        n_forced = 3
        sels = []
        for c in range(tq // LANE):
            blk = lax.broadcasted_iota(jnp.int32, (nsp_e, LANE), 0)
            cur = (t0 + c * LANE + lax.broadcasted_iota(jnp.int32, (nsp_e, LANE), 1)) // SEL_BLOCK
            valid = blk <= cur
            forced = (blk == 0) | (blk >= cur - 1)
            score = jnp.where(valid & jnp.logical_not(forced), p_slc[:, c * LANE:(c + 1) * LANE], -1.0)
            for _ in range(max(min(SEL_TOPK, seq // SEL_BLOCK) - n_forced, 0)):
                m = jnp.max(score, axis=0, keepdims=True)
                first = jnp.min(jnp.where(score == m, blk, nsp), axis=0, keepdims=True)
                score = jnp.where(blk == first, -1.0, score)
            sels.append(jnp.where(valid & (score < 0.0), 1.0, 0.0))
        sel = jnp.concatenate(sels, axis=1)
        sel_ref[0, 0, 0:nsp_e] = sel
        if nsp_e < nsp:
            sel_ref[0, 0, nsp_e:nsp] = jnp.zeros((nsp - nsp_e, tq), F32)
        for f in range(tq // tf):
            per_blk = _dot_nt(jnp.ones((8, tf), BF16), sel[:, f * tf:(f + 1) * tf].astype(BF16))
            cnt_ref[0, 0, f] = _dot(per_blk.astype(BF16), gm_ref[0:nsp_e])

        for sub in range(tq // tw):
            k_start = pl.multiple_of(jnp.clip(t0 + sub * tw - WINDOW, 0, seq - wk), LANE)
            kw = kw_ref[0, pl.ds(k_start, wk), :]
            vwt = _with_ones(vwt_ref[0, :, pl.ds(k_start, wk)])
            qsub = jnp.concatenate(
                [qst_sc[hh * tq + sub * tw:hh * tq + (sub + 1) * tw] for hh in range(NSA_HG)], axis=0)
            st = _dot_nt(kw, qsub)
            t_w = t0 + sub * tw + lax.broadcasted_iota(jnp.int32, (wk, tw), 1)
            dist_i = t_w - (k_start + lax.broadcasted_iota(jnp.int32, (wk, tw), 0))
            madd_w = jnp.where((dist_i >= 0) & (dist_i < WINDOW), 0.0, NEG)
            cols = slice(sub * tw, (sub + 1) * tw)
            for hh in range(NSA_HG):
                z = st[:, hh * tw:(hh + 1) * tw] + bw_sc[hh] + madd_w
                e = jnp.exp2(z - jnp.max(z, axis=0, keepdims=True))
                o_aug = _dot(vwt, e.astype(BF16))
                o_win = o_aug[0:NSA_DH] * (1.0 / jnp.maximum(o_aug[NSA_DH:NSA_DH + 1], 1e-30))
                ocw_ref[0, 0, NSA_DH * hh:NSA_DH * (hh + 1), cols] = (
                    gates[hh:hh + 1, cols] * o_cmp[hh][:, cols]
                    + gates[2 * NSA_HG + hh:2 * NSA_HG + hh + 1, cols] * o_win)

    part = (i * n_parts) // pl.num_programs(2)
    for v in range(n_parts):
        pl.when(part == v)(functools.partial(tile_body, ncp * (v + 1) // n_parts, nsp * (v + 1) // n_parts))


def _cw_parts(nq, ncp, nsp):
    for n in (4, 2):
        if nq % n == 0 and (ncp // n) % LANE == 0 and (nsp // n) % 8 == 0:
            return n
    return 1


def _nsa_cw(q, kcd, vct, kwd, vwt, gt, ovt, gmat, tq, tf):
    B, S, _ = q.shape
    ncp = kcd.shape[2]
    nsp = ovt.shape[0]
    nq = S // tq
    tw = min(MXU_COLS, tq)
    return pl.pallas_call(
        functools.partial(_nsa_cw_body, tq=tq, tw=tw, tf=tf, seq=S, n_parts=_cw_parts(nq, ncp, nsp)),
        grid=(B, NSA_GROUPS, nq),
        in_specs=[pl.BlockSpec((1, tq, 2 * LANE), lambda b, g, i: (b, i, g)),
                  pl.BlockSpec((1, 1, ncp, LANE), lambda b, g, i: (b, g, 0, 0)),
                  pl.BlockSpec((1, 1, NSA_DH, ncp), lambda b, g, i: (b, g, 0, 0)),
                  pl.BlockSpec((1, S, LANE), lambda b, g, i: (b, 0, g)),
                  pl.BlockSpec((1, NSA_DH, S), lambda b, g, i: (b, g, 0)),
                  pl.BlockSpec((1, GATE_ROWS, tq), lambda b, g, i: (b, g, i)),
                  pl.BlockSpec(ovt.shape, lambda b, g, i: (0, 0)),
                  pl.BlockSpec(gmat.shape, lambda b, g, i: (0, 0))],
        out_specs=[pl.BlockSpec((1, 1, NSA_HG * NSA_DH, tq), lambda b, g, i: (b, g, 0, i)),
                   pl.BlockSpec((1, 1, nsp, tq), lambda b, g, i: (b, g, 0, i)),
                   pl.BlockSpec((1, 1, tq // tf, 8, LANE), lambda b, g, i: (b, g, i, 0, 0))],
        out_shape=[jax.ShapeDtypeStruct((B, NSA_GROUPS, NSA_HG * NSA_DH, S), F32),
                   jax.ShapeDtypeStruct((B, NSA_GROUPS, nsp, S), F32),
                   jax.ShapeDtypeStruct((B, NSA_GROUPS, S // tf, 8, LANE), F32)],
        scratch_shapes=[pltpu.VMEM((NSA_HG, ncp, tq), F32), pltpu.VMEM((NSA_HG, WINDOW + tw, tw), F32),
                        pltpu.VMEM((NSA_HG * tq, LANE), BF16)],
        compiler_params=_params(("arbitrary", "arbitrary", "arbitrary")),
        name="nsa_cw",
    )(q, kcd, vct, kwd, vwt, gt, ovt, gmat)


def _nsa_sel_body(flag_ref, q_ref, ks_ref, vst_ref, sel_ref, gt_ref, ocw_ref, o_ref,
                  m_sc, acc_sc, skb_sc, qst_sc, sa_sc, sb_sc, lst_sc, *, tq, tk, n_tiles):
    g = pl.program_id(1)
    i = pl.program_id(2)
    t0 = i * tq
    j_diag = t0 // tk

    @pl.when(i == 0)
    def _():
        key_f = lax.broadcasted_iota(jnp.int32, (tk, tq), 0).astype(F32)
        for hh in range(NSA_HG):
            skb_sc[hh] = _slope(g, hh) * key_f

    qst_sc[...] = _stack_heads(q_ref, tq)
    m_sc[...] = jnp.full(m_sc.shape, NEG, F32)
    acc_sc[...] = jnp.zeros(acc_sc.shape, F32)

    base = ((pl.program_id(0) * NSA_GROUPS + g) * pl.num_programs(2) + i) * n_tiles
    n_act = jnp.int32(0)
    for j in range(n_tiles):
        lst_sc[n_act] = j
        n_act = n_act + jnp.where((flag_ref[base + j] > 0) & (j < j_diag), 1, 0)
    lst_sc[n_act] = j_diag

    def scores(j, s_ref):
        k0 = pl.multiple_of(j * tk, tk)
        s_ref[...] = _dot_nt(ks_ref[0, pl.ds(k0, tk), :], qst_sc[...])

    def consume(j, s_ref, diagonal):
        k0 = pl.multiple_of(j * tk, tk)
        vt = _with_ones(vst_ref[0, :, pl.ds(k0, tk)])
        b0 = j * (tk // SEL_BLOCK)
        pieces = []
        for r in range(tk // SEL_BLOCK):
            chosen = sel_ref[0, 0, pl.ds(b0 + r, 1), :]
            pieces.append(jnp.broadcast_to(jnp.where(chosen > 0.5, 0.0, NEG), (SEL_BLOCK, tq)))
        madd = jnp.concatenate(pieces, axis=0)
        if diagonal:
            key_i = lax.broadcasted_iota(jnp.int32, (tk, tq), 0)
            qry_i = lax.broadcasted_iota(jnp.int32, (tk, tq), 1)
            madd = jnp.where(key_i + k0 <= qry_i + t0, madd, NEG)
        for hh in range(NSA_HG):
            c = _slope(g, hh) * (k0 - t0).astype(F32)
            z = s_ref[:, hh * tq:(hh + 1) * tq] + skb_sc[hh] + madd
            m_prev = m_sc[hh]
            m_new = jnp.maximum(m_prev, jnp.max(z, axis=0, keepdims=True) + c)
            alpha = jnp.exp2(m_prev - m_new)
            p = jnp.exp2(z - (m_new - c))
            acc_sc[hh] = alpha * acc_sc[hh] + _dot(vt, p.astype(BF16))
            m_sc[hh] = m_new

    scores(lst_sc[0], sa_sc)

    def pair(jj, carry):
        it = 2 * jj
        scores(lst_sc[it + 1], sb_sc)
        consume(lst_sc[it], sa_sc, False)
        scores(lst_sc[it + 2], sa_sc)
        consume(lst_sc[it + 1], sb_sc, False)
        return carry

    lax.fori_loop(0, n_act // 2, pair, 0)

    @pl.when(n_act % 2 == 0)
    def _():
        consume(j_diag, sa_sc, True)

    @pl.when(n_act % 2 == 1)
    def _():
        scores(j_diag, sb_sc)
        consume(lst_sc[n_act - 1], sa_sc, False)
        consume(j_diag, sb_sc, True)

    gates = gt_ref[0]
    for pair_i in range(NSA_HG // 2):
        halves = []
        for hh in (2 * pair_i, 2 * pair_i + 1):
            acc = acc_sc[hh]
            w = gates[NSA_HG + hh:NSA_HG + hh + 1] / jnp.maximum(acc[NSA_DH:NSA_DH + 1], 1e-30)
            halves.append(acc[0:NSA_DH] * w + ocw_ref[0, 0, NSA_DH * hh:NSA_DH * (hh + 1), :])
        o_ref[0, :, LANE * pair_i:LANE * (pair_i + 1)] = jnp.concatenate(halves, axis=0).T.astype(BF16)


def _nsa_sel(flags, q, ksd, vst, sel, gt, ocw, tq, tk):
    B, S, _ = q.shape
    nsp = sel.shape[2]
    n_tiles = S // tk
    assert tk % tq == 0, "a query tile must lie inside one (the diagonal) key tile"
    qs = pl.BlockSpec((1, tq, 2 * LANE), lambda b, g, i, f: (b, i, g))
    return pl.pallas_call(
        functools.partial(_nsa_sel_body, tq=tq, tk=tk, n_tiles=n_tiles),
        grid_spec=pltpu.PrefetchScalarGridSpec(
            num_scalar_prefetch=1,
            grid=(B, NSA_GROUPS, S // tq),
            in_specs=[qs,
                      pl.BlockSpec((1, S, LANE), lambda b, g, i, f: (b, 0, g)),
                      pl.BlockSpec((1, NSA_DH, S), lambda b, g, i, f: (b, g, 0)),
                      pl.BlockSpec((1, 1, nsp, tq), lambda b, g, i, f: (b, g, 0, i)),
                      pl.BlockSpec((1, GATE_ROWS, tq), lambda b, g, i, f: (b, g, i)),
                      pl.BlockSpec((1, 1, NSA_HG * NSA_DH, tq), lambda b, g, i, f: (b, g, 0, i))],
            out_specs=qs,
            scratch_shapes=[pltpu.VMEM((NSA_HG, 1, tq), F32),
                            pltpu.VMEM((NSA_HG, NSA_DH + ONES_ROWS, tq), F32),
                            pltpu.VMEM((NSA_HG, tk, tq), F32),
                            pltpu.VMEM((NSA_HG * tq, LANE), BF16),
                            pltpu.VMEM((tk, NSA_HG * tq), F32),
                            pltpu.VMEM((tk, NSA_HG * tq), F32),
                            pltpu.SMEM((n_tiles + 1,), jnp.int32)]),
        out_shape=jax.ShapeDtypeStruct((B, S, 4 * LANE), BF16),
        compiler_params=_params(("arbitrary", "arbitrary", "arbitrary")),
        name="nsa_sel",
    )(flags, q, ksd, vst, sel, gt, ocw)


def _mla_body(q_ref, k_ref, vt_ref, o_ref, m_sc, acc_sc, sa_sc, sb_sc, *, tq, tc, nh):
    n_full = pl.program_id(2)
    m_sc[...] = jnp.full(m_sc.shape, NEG, F32)
    acc_sc[...] = jnp.zeros(acc_sc.shape, F32)

    def scores(j, s_ref):
        k0 = pl.multiple_of(j * tq, tq)
        for hh in range(nh):
            lanes = slice(LANE * hh, LANE * (hh + 1))
            s_ref[hh] = _dot_nt(k_ref[0, pl.ds(k0, tq), lanes], q_ref[0, :, lanes])

    def consume(j, s_ref, diagonal):
        k0 = pl.multiple_of(j * tq, tq)
        for hh in range(nh):
            vt = _with_ones(vt_ref[0, MLA_V * hh:MLA_V * (hh + 1), pl.ds(k0, tq)])
            for c in range(tq // tc):
                cols = slice(c * tc, (c + 1) * tc)
                st = s_ref[hh, :, cols]
                if diagonal:
                    causal = (lax.broadcasted_iota(jnp.int32, (tq, tc), 0)
                              <= lax.broadcasted_iota(jnp.int32, (tq, tc), 1) + c * tc)
                    st = jnp.where(causal, st, NEG)
                m_prev = m_sc[hh, :, cols]
                m_new = jnp.maximum(m_prev, jnp.max(st, axis=0, keepdims=True))
                alpha = jnp.exp2(m_prev - m_new)
                p = jnp.exp2(st - m_new)
                acc_sc[hh, :, cols] = alpha * acc_sc[hh, :, cols] + _dot(vt, p.astype(BF16))
                m_sc[hh, :, cols] = m_new

    scores(0, sa_sc)

    def pair(jj, carry):
        j = 2 * jj
        scores(j + 1, sb_sc)
        consume(j, sa_sc, False)
        scores(j + 2, sa_sc)
        consume(j + 1, sb_sc, False)
        return carry

    lax.fori_loop(0, n_full // 2, pair, 0)

    @pl.when(n_full % 2 == 0)
    def _():
        consume(n_full, sa_sc, True)

    @pl.when(n_full % 2 == 1)
    def _():
        scores(n_full, sb_sc)
        consume(n_full - 1, sa_sc, False)
        consume(n_full, sb_sc, True)

    for pair_i in range(nh // 2):
        halves = []
        for hh in (2 * pair_i, 2 * pair_i + 1):
            acc = acc_sc[hh]
            halves.append(acc[0:MLA_V] * (1.0 / jnp.maximum(acc[MLA_V:MLA_V + 1], 1e-30)))
        o_ref[0, :, LANE * pair_i:LANE * (pair_i + 1)] = jnp.concatenate(halves, axis=0).T.astype(BF16)


def _mla(qm, km, vmt, tq, nh):
    B, S, _ = qm.shape
    return pl.pallas_call(
        functools.partial(_mla_body, tq=tq, tc=min(256, tq), nh=nh),
        grid=(B, MLA_HEADS // nh, S // tq),
        in_specs=[pl.BlockSpec((1, tq, nh * LANE), lambda b, p, i: (b, i, p)),
                  pl.BlockSpec((1, S, nh * LANE), lambda b, p, i: (b, 0, p)),
                  pl.BlockSpec((1, nh * MLA_V, S), lambda b, p, i: (b, p, 0))],
        out_specs=pl.BlockSpec((1, tq, nh * MLA_V), lambda b, p, i: (b, i, p)),
        out_shape=jax.ShapeDtypeStruct((B, S, MLA_HEADS * MLA_V), BF16),
        scratch_shapes=[pltpu.VMEM((nh, 1, tq), F32), pltpu.VMEM((nh, MLA_V + ONES_ROWS, tq), F32),
                        pltpu.VMEM((nh, tq, tq), F32), pltpu.VMEM((nh, tq, tq), F32)],
        compiler_params=_params(("parallel", "parallel", "arbitrary")),
        name="mla",
    )(qm, km, vmt)


def _merge_body(on_ref, om_ref, ga_ref, gb_ref, x_ref, mod_ref, won_ref, wom_ref, wout_ref, o_ref):
    y = (ga_ref[0].astype(F32) * _dot(on_ref[0], won_ref[...])
         + gb_ref[0].astype(F32) * _dot(om_ref[0], wom_ref[...]))
    o_ref[0] = x_ref[0] + mod_ref[0, 2:3, :] * _dot(y.astype(BF16), wout_ref[...])


def _merge(o_nsa, o_mla, ga, gb, x, mod, w_on, w_om, w_out, tm):
    B, S, D = x.shape
    row = lambda n: pl.BlockSpec((1, tm, n), lambda b, i: (b, i, 0))
    full = lambda a: pl.BlockSpec(a.shape, lambda b, i: (0,) * a.ndim)
    return pl.pallas_call(
        _merge_body,
        grid=(B, S // tm),
        in_specs=[row(o_nsa.shape[2]), row(o_mla.shape[2]), row(D), row(D), row(D),
                  pl.BlockSpec((1, 6, D), lambda b, i: (b, 0, 0)), full(w_on), full(w_om), full(w_out)],
        out_specs=row(D),
        out_shape=jax.ShapeDtypeStruct((B, S, D), F32),
        compiler_params=_params(("parallel", "parallel")),
        name="merge",
    )(o_nsa, o_mla, ga, gb, x, mod, w_on, w_om, w_out)


def _mlp_body(x_ref, mod_ref, gmlp_ref, w1_ref, w2_ref, gfin_ref, o_ref, *, fc, final):
    x1 = x_ref[0]
    h = _rms(x1, gmlp_ref[...]) * (1.0 + mod_ref[0, 4:5, :]) + mod_ref[0, 3:4, :]
    hb = h.astype(BF16)
    acc = jnp.zeros(x1.shape, F32)
    for c in range(w1_ref.shape[1] // fc):
        a = jnp.maximum(_dot(hb, w1_ref[:, c * fc:(c + 1) * fc]), 0.0)
        acc = acc + _dot((a * a).astype(BF16), w2_ref[c * fc:(c + 1) * fc, :])
    x2 = x1 + mod_ref[0, 5:6, :] * acc
    o_ref[0] = _rms(x2, gfin_ref[...]) if final else x2


def _mlp(x1, mod, g_mlp, w1, w2, g_fin, tm, fc, final):
    B, S, D = x1.shape
    row = pl.BlockSpec((1, tm, D), lambda b, i: (b, i, 0))
    full = lambda a: pl.BlockSpec(a.shape, lambda b, i: (0,) * a.ndim)
    return pl.pallas_call(
        functools.partial(_mlp_body, fc=fc, final=final),
        grid=(B, S // tm),
        in_specs=[row, pl.BlockSpec((1, 6, D), lambda b, i: (b, 0, 0)), full(g_mlp), full(w1), full(w2),
                  full(g_fin)],
        out_specs=row,
        out_shape=jax.ShapeDtypeStruct((B, S, D), F32),
        compiler_params=_params(("parallel", "parallel")),
        name="mlp",
    )(x1, mod, g_mlp, w1, w2, g_fin)


def _inproj_weight(w_in):
    d = w_in.shape[0]
    o = 0
    parts = {}
    for name, n in (("q", 512), ("kc", 128), ("vc", 128), ("ks", 128), ("vs", 128), ("kw", 128),
                    ("vw", 128), ("g", 24), ("qd", MLA_Q_RANK), ("kvd", MLA_KV_RANK), ("kr", MLA_ROPE),
                    ("ga", D_MODEL), ("gb", D_MODEL)):
        parts[name] = w_in[:, o:o + n]
        o += n
    wg = parts["g"].reshape(d, 3, NSA_GROUPS, NSA_HG).transpose(0, 2, 1, 3).reshape(d, NSA_GROUPS, 3 * NSA_HG)
    wg = jnp.pad(wg, ((0, 0), (0, 0), (0, GATE_ROWS - 3 * NSA_HG))).reshape(d, NSA_GROUPS * GATE_ROWS)
    wg = jnp.pad(wg, ((0, 0), (0, LANE - NSA_GROUPS * GATE_ROWS)))
    kr = jnp.pad(parts["kr"], ((0, 0), (MLA_NOPE, LANE - MLA_NOPE - MLA_ROPE)))
    w_all = jnp.concatenate(
        [p.astype(BF16) for p in
         (parts["q"], parts["kc"], parts["vc"], parts["ks"], parts["kw"], parts["vs"], parts["vw"], wg, kr,
          parts["qd"], parts["kvd"], parts["ga"], parts["gb"])], axis=1)
    assert w_all.shape[1] == IN_COLS
    return w_all


def _mla_q_weight(w_uq):
    r = w_uq.shape[0]
    w = w_uq.reshape(r, MLA_HEADS, MLA_NOPE + MLA_ROPE)
    w = jnp.pad(w, ((0, 0), (0, 0), (0, LANE - MLA_NOPE - MLA_ROPE)))
    return w.reshape(r, MLA_HEADS * LANE).astype(BF16)


def _mla_kv_weight(w_uk, w_uv):
    r = w_uk.shape[0]
    k = w_uk.reshape(r, MLA_HEADS, MLA_NOPE)
    k = jnp.pad(k, ((0, 0), (0, 0), (0, LANE - MLA_NOPE))).reshape(r, MLA_HEADS * LANE)
    return jnp.concatenate([k, w_uv], axis=1).astype(BF16)


def _rope_tables(seq):
    half = MLA_ROPE // 2
    pos = jnp.arange(seq, dtype=F32)
    inv_freq = ROPE_THETA ** (-jnp.arange(half, dtype=F32) / half)
    ang = pos[:, None] * inv_freq[None, :]
    cos, sin = jnp.cos(ang), jnp.sin(ang)
    ones = jnp.ones((seq, MLA_NOPE), F32)
    z_lo = jnp.zeros((seq, MLA_NOPE), F32)
    z_hi = jnp.zeros((seq, LANE - MLA_NOPE - MLA_ROPE), F32)
    return (jnp.concatenate([ones, cos, cos, z_hi], axis=1),
            jnp.concatenate([z_lo, sin, sin, z_hi], axis=1))


def _compress_weights(pe, w1, w2):
    hid = w1.shape[1]
    wa = w1[:CMP_STRIDE * NSA_DH].reshape(CMP_STRIDE, NSA_DH, hid)
    wb = w1[CMP_STRIDE * NSA_DH:].reshape(CMP_STRIDE, NSA_DH, hid)
    z = jnp.zeros_like(wa)
    cols = []
    for g in range(NSA_GROUPS):
        for w in (wa, wb):
            slots = [z, z]
            slots[g] = w
            cols.append(jnp.stack(slots, axis=1).reshape(CMP_STRIDE * NSA_GROUPS * NSA_DH, hid))
    w1big = jnp.concatenate(cols, axis=1).astype(BF16)
    pe8 = jnp.zeros((8, CMP_LEN * NSA_DH), F32).at[0].set(pe.reshape(-1))
    w2d = jnp.concatenate([w2, w2], axis=1).astype(BF16)
    return w1big, pe8, w1, w2d


def _overlap_table(seq):
    ns = seq // SEL_BLOCK
    nsp = -(-ns // LANE) * LANE
    ncp = seq // CMP_STRIDE
    s_id = jnp.arange(nsp)[:, None]
    c_id = jnp.arange(ncp)[None, :]
    c_start = c_id * CMP_STRIDE
    return ((c_start < s_id * SEL_BLOCK + SEL_BLOCK) & (c_start + CMP_LEN - 1 >= s_id * SEL_BLOCK)
            & (c_id < ncp - 1)).astype(BF16)


def _tiles(seq):
    tm_in = min(512, seq)
    tm_out = min(512, seq)
    tq_cw = min(1024, seq)
    tq_sel = min(256, seq)
    tk_sel = min(256, seq)
    tq_mla = min(512, seq)
    mla_heads_per_step = 4
    ff_chunk = 1024
    return tm_in, tm_out, tq_cw, tq_sel, tk_sel, tq_mla, mla_heads_per_step, ff_chunk


def kernel(x, c, w_ada, b_ada, g_mix, w_in, pe_ck, w_ck1, w_ck2, pe_cv, w_cv1, w_cv2, g_cq, w_uq, g_ckv,
           w_uk, w_uv, w_o_nsa, w_o_mla, w_out, g_mlp, w_fc1, w_fc2, g_final):
    B, S, D = x.shape
    depth = w_ada.shape[0]
    tm_in, tm_out, tq_cw, tq_nsa, tk_sel, tq_mla, mla_heads_per_step, ff_chunk = _tiles(S)
    cs, sn = _rope_tables(S)
    ovt = _overlap_table(S)
    gmat = (jnp.arange(ovt.shape[0])[:, None] // (tk_sel // SEL_BLOCK) == jnp.arange(LANE)[None, :]).astype(BF16)
    n_chunks = S // CMP_STRIDE
    for layer in range(depth):
        mod = _ada(c, w_ada[layer], b_ada[layer])
        (q, kc_in, vc_in, ksd, vst, kwd, vwt, gt, qm, km, vmt, ga, gb) = _inproj(
            x, mod, g_mix[layer][None], _inproj_weight(w_in[layer]), g_cq[layer][None],
            _mla_q_weight(w_uq[layer]), g_ckv[layer][None], _mla_kv_weight(w_uk[layer], w_uv[layer]),
            cs, sn, tm_in)
        w1k, pek, w1kf, w2k = _compress_weights(pe_ck[layer], w_ck1[layer], w_ck2[layer])
        w1v, pev, w1vf, w2v = _compress_weights(pe_cv[layer], w_cv1[layer], w_cv2[layer])
        width = CMP_STRIDE * NSA_GROUPS * NSA_DH
        kcd, vct = _compress(kc_in.reshape(B, n_chunks, width), vc_in.reshape(B, n_chunks, width),
                             w1k, w1v, pek, pev, w1kf, w1vf, w2k, w2v)
        ocw, sel, cnt = _nsa_cw(q, kcd, vct, kwd, vwt, gt, ovt, gmat, tq_cw, tq_nsa)
        flags = (cnt[:, :, :, 0, :S // tk_sel] > 0.5).astype(jnp.int32).reshape(-1)
        o_nsa = _nsa_sel(flags, q, ksd, vst, sel, gt, ocw, tq_nsa, tk_sel)
        o_mla = _mla(qm, km, vmt, tq_mla, mla_heads_per_step)
        x1 = _merge(o_nsa, o_mla, ga, gb, x, mod, w_o_nsa[layer].astype(BF16), w_o_mla[layer].astype(BF16),
                    w_out[layer].astype(BF16), tm_out)
        x = _mlp(x1, mod, g_mlp[layer][None], w_fc1[layer].astype(BF16), w_fc2[layer].astype(BF16),
                 g_final[None], tm_out, ff_chunk, layer == depth - 1)
    return x
```

```python
import functools
import math

import jax
import jax.numpy as jnp
from jax import lax
from jax.experimental import pallas as pl
from jax.experimental.pallas import tpu as pltpu

F32 = jnp.float32
BF16 = jnp.bfloat16

D_MODEL = 1024
NSA_GROUPS = 2
NSA_HG = 4
NSA_DH = 64
CMP_LEN = 32
CMP_STRIDE = 16
CMP_HIDDEN = 256
SEL_BLOCK = 64
SEL_TOPK = 16
WINDOW = 512
MLA_HEADS = 8
MLA_NOPE = 64
MLA_ROPE = 32
MLA_V = 64
MLA_Q_RANK = 384
MLA_KV_RANK = 256
ROPE_THETA = 10000.0
D_FF = 4 * D_MODEL
EPS = 1e-6
FORCE_BONUS = 1000.0

LANE = 128
HALF = LANE // 2
NEG = -1e30
LOG2E = math.log2(math.e)
VMEM_LIMIT = 52 * 1024 * 1024
GATE_ROWS = 16
ONES_ROWS = 16
MXU_COLS = 256
OFF_Q = 0
OFF_KC_VC = 512
OFF_KS_KW = 768
OFF_VS_VW = 1024
OFF_G_KR = 1280
OFF_QD = 1536
OFF_KVD = 1920
OFF_GA = 2176
OFF_GB = 3200
IN_COLS = 4224


def _dot(a, b):
    return jnp.dot(a, b, preferred_element_type=F32)


def _dot_nt(a, b):
    return lax.dot_general(a, b, (((1,), (1,)), ((), ())), preferred_element_type=F32)


def _rms(x, g):
    return x * lax.rsqrt(jnp.mean(x * x, axis=-1, keepdims=True) + EPS) * g


def _sigmoid(x):
    return 1.0 / (1.0 + jnp.exp(-x))


def _params(sem):
    return pltpu.CompilerParams(dimension_semantics=sem, vmem_limit_bytes=VMEM_LIMIT)


def _ada_body(c_ref, w_ref, b_ref, o_ref):
    o_ref[...] = jnp.dot(c_ref[...], w_ref[...], preferred_element_type=F32,
                         precision=lax.Precision.HIGHEST) + b_ref[...]


def _ada(c, w, b):
    nb, d = c.shape
    n = w.shape[1]
    rows = 8
    cp = jnp.zeros((rows, d), F32).at[:nb].set(c)
    tn = 1536
    out = pl.pallas_call(
        _ada_body,
        grid=(n // tn,),
        in_specs=[pl.BlockSpec((rows, d), lambda j: (0, 0)),
                  pl.BlockSpec((d, tn), lambda j: (0, j)),
                  pl.BlockSpec((1, tn), lambda j: (0, j))],
        out_specs=pl.BlockSpec((rows, tn), lambda j: (0, j)),
        out_shape=jax.ShapeDtypeStruct((rows, n), F32),
        compiler_params=_params(("arbitrary",)),
        name="ada",
    )(cp, w, b.reshape(1, n))
    return out[:nb].reshape(nb, 6, d)


def _inproj_body(x_ref, mod_ref, gmix_ref, w_ref, gcq_ref, wq2_ref, gckv_ref, wkv_ref,
                 cs_ref, sn_ref,
                 q_ref, kc_ref, vc_ref, ks_ref, vst_ref, kw_ref, vwt_ref, gt_ref,
                 qm_ref, km_ref, vmt_ref, ga_ref, gb_ref):
    x = x_ref[0]
    h = _rms(x, gmix_ref[...]) * (1.0 + mod_ref[0, 1:2, :]) + mod_ref[0, 0:1, :]
    hb = h.astype(BF16)

    def proj(lo, n):
        return _dot(hb, w_ref[:, lo:lo + n])

    lane = lax.broadcasted_iota(jnp.int32, (x.shape[0], LANE), 1)
    half = MLA_ROPE // 2

    def dup_groups(z):
        swapped = pltpu.roll(z, HALF, 1)
        lower = lane < HALF
        return jnp.concatenate([jnp.where(lower, z, swapped), jnp.where(lower, swapped, z)], axis=1)

    def rotary(a):
        partner = jnp.where(lane < MLA_NOPE + half, -pltpu.roll(a, LANE - half, 1), pltpu.roll(a, half, 1))
        return a * cs_ref[...] + partner * sn_ref[...]

    q_ref[0] = (proj(OFF_Q, 512) * (NSA_DH ** -0.5 * LOG2E)).astype(BF16)
    z = proj(OFF_KC_VC, MXU_COLS)
    kc_ref[0] = z[:, 0:LANE].astype(BF16)
    vc_ref[0] = z[:, LANE:].astype(BF16)
    z = proj(OFF_KS_KW, MXU_COLS)
    ks_ref[0] = dup_groups(z[:, 0:LANE]).astype(BF16)
    kw_ref[0] = dup_groups(z[:, LANE:]).astype(BF16)
    z = proj(OFF_VS_VW, MXU_COLS)
    vst_ref[0] = z[:, 0:LANE].T.astype(BF16)
    vwt_ref[0] = z[:, LANE:].T.astype(BF16)
    z = proj(OFF_G_KR, MXU_COLS)
    gt_ref[0] = _sigmoid(z[:, 0:LANE]).T[0:NSA_GROUPS * GATE_ROWS]
    kr = rotary(z[:, LANE:])
    ga_ref[0] = _sigmoid(proj(OFF_GA, 1024)).astype(BF16)
    gb_ref[0] = _sigmoid(proj(OFF_GB, 1024)).astype(BF16)

    nh = MLA_HEADS * LANE
    cq = _rms(proj(OFF_QD, MLA_Q_RANK), gcq_ref[...]).astype(BF16)
    q2 = _dot(cq, wq2_ref[...])
    scale = (MLA_NOPE + MLA_ROPE) ** -0.5 * LOG2E
    for hh in range(MLA_HEADS):
        qm_ref[0, :, LANE * hh:LANE * (hh + 1)] = (rotary(q2[:, LANE * hh:LANE * (hh + 1)]) * scale).astype(BF16)
    ckv = _rms(proj(OFF_KVD, MLA_KV_RANK), gckv_ref[...]).astype(BF16)
    kv = _dot(ckv, wkv_ref[...])
    for hh in range(MLA_HEADS):
        km_ref[0, :, LANE * hh:LANE * (hh + 1)] = (kv[:, LANE * hh:LANE * (hh + 1)] + kr).astype(BF16)
    vmt_ref[0] = kv[:, nh:].T.astype(BF16)


def _inproj(x, mod, g_mix, w_all, g_cq, wq2, g_ckv, wkv, cs, sn, tm):
    B, S, D = x.shape
    row = lambda n: pl.BlockSpec((1, tm, n), lambda b, i: (b, i, 0))
    col = lambda n: pl.BlockSpec((1, n, tm), lambda b, i: (b, 0, i))
    full = lambda a: pl.BlockSpec(a.shape, lambda b, i: (0,) * a.ndim)
    tok = lambda n, dt: (row(n), jax.ShapeDtypeStruct((B, S, n), dt))
    tr = lambda n, dt: (col(n), jax.ShapeDtypeStruct((B, n, S), dt))
    outs = [tok(512, BF16), tok(128, BF16), tok(128, BF16), tok(256, BF16), tr(LANE, BF16),
            tok(256, BF16), tr(LANE, BF16), tr(NSA_GROUPS * GATE_ROWS, F32),
            tok(1024, BF16), tok(1024, BF16), tr(MLA_HEADS * MLA_V, BF16), tok(1024, BF16), tok(1024, BF16)]
    return pl.pallas_call(
        _inproj_body,
        grid=(B, S // tm),
        in_specs=[row(D),
                  pl.BlockSpec((1, 6, D), lambda b, i: (b, 0, 0)),
                  full(g_mix), full(w_all), full(g_cq), full(wq2), full(g_ckv), full(wkv),
                  pl.BlockSpec((tm, LANE), lambda b, i: (i, 0)),
                  pl.BlockSpec((tm, LANE), lambda b, i: (i, 0))],
        out_specs=[o[0] for o in outs],
        out_shape=[o[1] for o in outs],
        compiler_params=_params(("parallel", "parallel")),
        name="inproj",
    )(x, mod, g_mix, w_all, g_cq, wq2, g_ckv, wkv, cs, sn)


def _compress_one(x, w1big, pe, w1f, w2d):
    hcat = _dot(x, w1big)
    const = jnp.dot(pe, w1f, preferred_element_type=F32, precision=lax.Precision.HIGHEST)[0:1]
    n = x.shape[0]
    rowid = lax.broadcasted_iota(jnp.int32, (n, LANE), 0)
    hid_w = CMP_HIDDEN
    outs = []
    for g in range(NSA_GROUPS):
        a = hcat[:, 2 * hid_w * g:2 * hid_w * g + hid_w]
        b = hcat[:, 2 * hid_w * g + hid_w:2 * hid_w * (g + 1)]
        hid = a + pltpu.roll(b, n - 1, 0) + const
        act = hid * _sigmoid(hid)
        out = _dot(act.astype(BF16), w2d)
        outs.append(jnp.where(rowid < n - 1, out, 0.0))
    return outs


def _compress_body(xk_ref, xv_ref, w1k_ref, w1v_ref, pek_ref, pev_ref, w1kf_ref, w1vf_ref,
                   w2k_ref, w2v_ref, kc_ref, vct_ref):
    kc = _compress_one(xk_ref[0], w1k_ref[...], pek_ref[...], w1kf_ref[...], w2k_ref[...])
    vc = _compress_one(xv_ref[0], w1v_ref[...], pev_ref[...], w1vf_ref[...], w2v_ref[...])
    for g in range(NSA_GROUPS):
        kc_ref[0, g] = kc[g].astype(BF16)
        vct_ref[0, g] = vc[g].T[0:NSA_DH].astype(BF16)


def _compress(xk, xv, w1k, w1v, pek, pev, w1kf, w1vf, w2k, w2v):
    B, n, width = xk.shape
    full = lambda a: pl.BlockSpec(a.shape, lambda b: (0,) * a.ndim)
    xs = pl.BlockSpec((1, n, width), lambda b: (b, 0, 0))
    return pl.pallas_call(
        _compress_body,
        grid=(B,),
        in_specs=[xs, xs, full(w1k), full(w1v), full(pek), full(pev), full(w1kf), full(w1vf),
                  full(w2k), full(w2v)],
        out_specs=[pl.BlockSpec((1, NSA_GROUPS, n, LANE), lambda b: (b, 0, 0, 0)),
                   pl.BlockSpec((1, NSA_GROUPS, NSA_DH, n), lambda b: (b, 0, 0, 0))],
        out_shape=[jax.ShapeDtypeStruct((B, NSA_GROUPS, n, LANE), BF16),
                   jax.ShapeDtypeStruct((B, NSA_GROUPS, NSA_DH, n), BF16)],
        compiler_params=_params(("parallel",)),
        name="compress",
    )(xk, xv, w1k, w1v, pek, pev, w1kf, w1vf, w2k, w2v)


def _stack_heads(q_ref, tq):
    upper = lax.broadcasted_iota(jnp.int32, (tq, LANE), 1) >= HALF
    parts = []
    for hh in range(NSA_HG):
        slab = q_ref[0, :, LANE * (hh // 2):LANE * (hh // 2 + 1)]
        keep = upper if hh % 2 else jnp.logical_not(upper)
        parts.append(jnp.where(keep, slab, jnp.zeros_like(slab)))
    return jnp.concatenate(parts, axis=0)


def _slope(g, hh):
    return jnp.where(g == 0, 1.0, 2.0 ** -NSA_HG).astype(F32) * (2.0 ** -(hh + 1) * LOG2E)


def _with_ones(vt):
    return jnp.concatenate([vt, jnp.ones((ONES_ROWS, vt.shape[1]), vt.dtype)], axis=0)


def _nsa_cw_body(q_ref, kc_ref, vct_ref, kw_ref, vwt_ref, gt_ref, ovt_ref, gm_ref,
                 ocw_ref, sel_ref, cnt_ref, bc_sc, bw_sc, qst_sc, *, tq, tw, tf, seq, n_parts):
    g = pl.program_id(1)
    i = pl.program_id(2)
    t0 = i * tq
    qst_sc[...] = _stack_heads(q_ref, tq)
    gates = gt_ref[0]
    ncp = kc_ref.shape[2]
    nsp = ovt_ref.shape[0]
    wk = WINDOW + tw

    @pl.when(i == 0)
    def _():
        c_end_f = (lax.broadcasted_iota(jnp.int32, (ncp, tq), 0) * CMP_STRIDE + (CMP_LEN - 1)).astype(F32)
        key_f = lax.broadcasted_iota(jnp.int32, (wk, tw), 0).astype(F32)
        for hh in range(NSA_HG):
            bc_sc[hh] = _slope(g, hh) * c_end_f
            bw_sc[hh] = _slope(g, hh) * key_f

    def tile_body(ncp_e, nsp_e):
        vct = _with_ones(vct_ref[0, 0, :, 0:ncp_e])
        st = _dot_nt(kc_ref[0, 0, 0:ncp_e, :], qst_sc[...])
        t_c = t0 + lax.broadcasted_iota(jnp.int32, (ncp_e, tq), 1)
        c_end = lax.broadcasted_iota(jnp.int32, (ncp_e, tq), 0) * CMP_STRIDE + (CMP_LEN - 1)
        madd_c = jnp.where(t_c >= c_end, 0.0, NEG)
        psum = jnp.zeros((ncp_e, tq), F32)
        o_cmp = []
        for hh in range(NSA_HG):
            z = st[:, hh * tq:(hh + 1) * tq] + bc_sc[hh, 0:ncp_e] + madd_c
            m = jnp.max(z, axis=0, keepdims=True)
            e = jnp.exp2(z - m)
            o_aug = _dot(vct, e.astype(BF16))
            r = jnp.where(m > 0.5 * NEG, 1.0 / jnp.maximum(o_aug[NSA_DH:NSA_DH + 1], 1e-30), 0.0)
            psum = psum + e * r
            o_cmp.append(o_aug[0:NSA_DH] * r)

        ovt = ovt_ref[0:nsp_e, 0:ncp_e]
        p1 = psum.astype(BF16)
        r1 = psum - p1.astype(F32)
        p2 = r1.astype(BF16)
        p3 = (r1 - p2.astype(F32)).astype(BF16)
        p_slc = _dot(ovt, p1) + _dot(ovt, p2) + _dot(ovt, p3)

        n_forced = 3
        sels = []
        for c in range(tq // LANE):
            blk = lax.broadcasted_iota(jnp.int32, (nsp_e, LANE), 0)
            cur = (t0 + c * LANE + lax.broadcasted_iota(jnp.int32, (nsp_e, LANE), 1)) // SEL_BLOCK
            valid = blk <= cur
            forced = (blk == 0) | (blk >= cur - 1)
            score = jnp.where(valid & jnp.logical_not(forced), p_slc[:, c * LANE:(c + 1) * LANE], -1.0)
            for _ in range(max(min(SEL_TOPK, seq // SEL_BLOCK) - n_forced, 0)):
                m = jnp.max(score, axis=0, keepdims=True)
                first = jnp.min(jnp.where(score == m, blk, nsp), axis=0, keepdims=True)
                score = jnp.where(blk == first, -1.0, score)
            sels.append(jnp.where(valid & (score < 0.0), 1.0, 0.0))
        sel = jnp.concatenate(sels, axis=1)
        sel_ref[0, 0, 0:nsp_e] = sel
        if nsp_e < nsp:
            sel_ref[0, 0, nsp_e:nsp] = jnp.zeros((nsp - nsp_e, tq), F32)
        for f in range(tq // tf):
            per_blk = _dot_nt(jnp.ones((8, tf), BF16), sel[:, f * tf:(f + 1) * tf].astype(BF16))
            cnt_ref[0, 0, f] = _dot(per_blk.astype(BF16), gm_ref[0:nsp_e])

        for sub in range(tq // tw):
            k_start = pl.multiple_of(jnp.clip(t0 + sub * tw - WINDOW, 0, seq - wk), LANE)
            kw = kw_ref[0, pl.ds(k_start, wk), :]
            vwt = _with_ones(vwt_ref[0, :, pl.ds(k_start, wk)])
            qsub = jnp.concatenate(
                [qst_sc[hh * tq + sub * tw:hh * tq + (sub + 1) * tw] for hh in range(NSA_HG)], axis=0)
            st = _dot_nt(kw, qsub)
            t_w = t0 + sub * tw + lax.broadcasted_iota(jnp.int32, (wk, tw), 1)
            dist_i = t_w - (k_start + lax.broadcasted_iota(jnp.int32, (wk, tw), 0))
            madd_w = jnp.where((dist_i >= 0) & (dist_i < WINDOW), 0.0, NEG)
            cols = slice(sub * tw, (sub + 1) * tw)
            for hh in range(NSA_HG):
                z = st[:, hh * tw:(hh + 1) * tw] + bw_sc[hh] + madd_w
                e = jnp.exp2(z - jnp.max(z, axis=0, keepdims=True))
                o_aug = _dot(vwt, e.astype(BF16))
                o_win = o_aug[0:NSA_DH] * (1.0 / jnp.maximum(o_aug[NSA_DH:NSA_DH + 1], 1e-30))
                ocw_ref[0, 0, NSA_DH * hh:NSA_DH * (hh + 1), cols] = (
                    gates[hh:hh + 1, cols] * o_cmp[hh][:, cols]
                    + gates[2 * NSA_HG + hh:2 * NSA_HG + hh + 1, cols] * o_win)

    part = (i * n_parts) // pl.num_programs(2)
    for v in range(n_parts):
        pl.when(part == v)(functools.partial(tile_body, ncp * (v + 1) // n_parts, nsp * (v + 1) // n_parts))


def _cw_parts(nq, ncp, nsp):
    for n in (4, 2):
        if nq % n == 0 and (ncp // n) % LANE == 0 and (nsp // n) % 8 == 0:
            return n
    return 1


def _nsa_cw(q, kcd, vct, kwd, vwt, gt, ovt, gmat, tq, tf):
    B, S, _ = q.shape
    ncp = kcd.shape[2]
    nsp = ovt.shape[0]
    nq = S // tq
    tw = min(MXU_COLS, tq)
    return pl.pallas_call(
        functools.partial(_nsa_cw_body, tq=tq, tw=tw, tf=tf, seq=S, n_parts=_cw_parts(nq, ncp, nsp)),
        grid=(B, NSA_GROUPS, nq),
        in_specs=[pl.BlockSpec((1, tq, 2 * LANE), lambda b, g, i: (b, i, g)),
                  pl.BlockSpec((1, 1, ncp, LANE), lambda b, g, i: (b, g, 0, 0)),
                  pl.BlockSpec((1, 1, NSA_DH, ncp), lambda b, g, i: (b, g, 0, 0)),
                  pl.BlockSpec((1, S, LANE), lambda b, g, i: (b, 0, g)),
                  pl.BlockSpec((1, NSA_DH, S), lambda b, g, i: (b, g, 0)),
                  pl.BlockSpec((1, GATE_ROWS, tq), lambda b, g, i: (b, g, i)),
                  pl.BlockSpec(ovt.shape, lambda b, g, i: (0, 0)),
                  pl.BlockSpec(gmat.shape, lambda b, g, i: (0, 0))],
        out_specs=[pl.BlockSpec((1, 1, NSA_HG * NSA_DH, tq), lambda b, g, i: (b, g, 0, i)),
                   pl.BlockSpec((1, 1, nsp, tq), lambda b, g, i: (b, g, 0, i)),
                   pl.BlockSpec((1, 1, tq // tf, 8, LANE), lambda b, g, i: (b, g, i, 0, 0))],
        out_shape=[jax.ShapeDtypeStruct((B, NSA_GROUPS, NSA_HG * NSA_DH, S), F32),
                   jax.ShapeDtypeStruct((B, NSA_GROUPS, nsp, S), F32),
                   jax.ShapeDtypeStruct((B, NSA_GROUPS, S // tf, 8, LANE), F32)],
        scratch_shapes=[pltpu.VMEM((NSA_HG, ncp, tq), F32), pltpu.VMEM((NSA_HG, WINDOW + tw, tw), F32),
                        pltpu.VMEM((NSA_HG * tq, LANE), BF16)],
        compiler_params=_params(("arbitrary", "arbitrary", "arbitrary")),
        name="nsa_cw",
    )(q, kcd, vct, kwd, vwt, gt, ovt, gmat)


def _nsa_sel_body(flag_ref, q_ref, ks_ref, vst_ref, sel_ref, gt_ref, ocw_ref, o_ref,
                  m_sc, acc_sc, skb_sc, qst_sc, sa_sc, sb_sc, lst_sc, *, tq, tk, n_tiles):
    g = pl.program_id(1)
    i = pl.program_id(2)
    t0 = i * tq
    j_diag = t0 // tk

    @pl.when(i == 0)
    def _():
        key_f = lax.broadcasted_iota(jnp.int32, (tk, tq), 0).astype(F32)
        for hh in range(NSA_HG):
            skb_sc[hh] = _slope(g, hh) * key_f

    qst_sc[...] = _stack_heads(q_ref, tq)
    m_sc[...] = jnp.full(m_sc.shape, NEG, F32)
    acc_sc[...] = jnp.zeros(acc_sc.shape, F32)

    base = ((pl.program_id(0) * NSA_GROUPS + g) * pl.num_programs(2) + i) * n_tiles
    n_act = jnp.int32(0)
    for j in range(n_tiles):
        lst_sc[n_act] = j
        n_act = n_act + jnp.where((flag_ref[base + j] > 0) & (j < j_diag), 1, 0)
    lst_sc[n_act] = j_diag

    def scores(j, s_ref):
        k0 = pl.multiple_of(j * tk, tk)
        s_ref[...] = _dot_nt(ks_ref[0, pl.ds(k0, tk), :], qst_sc[...])

    def consume(j, s_ref, diagonal):
        k0 = pl.multiple_of(j * tk, tk)
        vt = _with_ones(vst_ref[0, :, pl.ds(k0, tk)])
        b0 = j * (tk // SEL_BLOCK)
        pieces = []
        for r in range(tk // SEL_BLOCK):
            chosen = sel_ref[0, 0, pl.ds(b0 + r, 1), :]
            pieces.append(jnp.broadcast_to(jnp.where(chosen > 0.5, 0.0, NEG), (SEL_BLOCK, tq)))
        madd = jnp.concatenate(pieces, axis=0)
        if diagonal:
            key_i = lax.broadcasted_iota(jnp.int32, (tk, tq), 0)
            qry_i = lax.broadcasted_iota(jnp.int32, (tk, tq), 1)
            madd = jnp.where(key_i + k0 <= qry_i + t0, madd, NEG)
        for hh in range(NSA_HG):
            c = _slope(g, hh) * (k0 - t0).astype(F32)
            z = s_ref[:, hh * tq:(hh + 1) * tq] + skb_sc[hh] + madd
            m_prev = m_sc[hh]
            m_new = jnp.maximum(m_prev, jnp.max(z, axis=0, keepdims=True) + c)
            alpha = jnp.exp2(m_prev - m_new)
            p = jnp.exp2(z - (m_new - c))
            acc_sc[hh] = alpha * acc_sc[hh] + _dot(vt, p.astype(BF16))
            m_sc[hh] = m_new

    scores(lst_sc[0], sa_sc)

    def pair(jj, carry):
        it = 2 * jj
        scores(lst_sc[it + 1], sb_sc)
        consume(lst_sc[it], sa_sc, False)
        scores(lst_sc[it + 2], sa_sc)
        consume(lst_sc[it + 1], sb_sc, False)
        return carry

    lax.fori_loop(0, n_act // 2, pair, 0)

    @pl.when(n_act % 2 == 0)
    def _():
        consume(j_diag, sa_sc, True)

    @pl.when(n_act % 2 == 1)
    def _():
        scores(j_diag, sb_sc)
        consume(lst_sc[n_act - 1], sa_sc, False)
        consume(j_diag, sb_sc, True)

    gates = gt_ref[0]
    for pair_i in range(NSA_HG // 2):
        halves = []
        for hh in (2 * pair_i, 2 * pair_i + 1):
            acc = acc_sc[hh]
            w = gates[NSA_HG + hh:NSA_HG + hh + 1] / jnp.maximum(acc[NSA_DH:NSA_DH + 1], 1e-30)
            halves.append(acc[0:NSA_DH] * w + ocw_ref[0, 0, NSA_DH * hh:NSA_DH * (hh + 1), :])
        o_ref[0, :, LANE * pair_i:LANE * (pair_i + 1)] = jnp.concatenate(halves, axis=0).T.astype(BF16)


def _nsa_sel(flags, q, ksd, vst, sel, gt, ocw, tq, tk):
    B, S, _ = q.shape
    nsp = sel.shape[2]
    n_tiles = S // tk
    assert tk % tq == 0, "a query tile must lie inside one (the diagonal) key tile"
    qs = pl.BlockSpec((1, tq, 2 * LANE), lambda b, g, i, f: (b, i, g))
    return pl.pallas_call(
        functools.partial(_nsa_sel_body, tq=tq, tk=tk, n_tiles=n_tiles),
        grid_spec=pltpu.PrefetchScalarGridSpec(
            num_scalar_prefetch=1,
            grid=(B, NSA_GROUPS, S // tq),
            in_specs=[qs,
                      pl.BlockSpec((1, S, LANE), lambda b, g, i, f: (b, 0, g)),
                      pl.BlockSpec((1, NSA_DH, S), lambda b, g, i, f: (b, g, 0)),
                      pl.BlockSpec((1, 1, nsp, tq), lambda b, g, i, f: (b, g, 0, i)),
                      pl.BlockSpec((1, GATE_ROWS, tq), lambda b, g, i, f: (b, g, i)),
                      pl.BlockSpec((1, 1, NSA_HG * NSA_DH, tq), lambda b, g, i, f: (b, g, 0, i))],
            out_specs=qs,
            scratch_shapes=[pltpu.VMEM((NSA_HG, 1, tq), F32),
                            pltpu.VMEM((NSA_HG, NSA_DH + ONES_ROWS, tq), F32),
                            pltpu.VMEM((NSA_HG, tk, tq), F32),
                            pltpu.VMEM((NSA_HG * tq, LANE), BF16),
                            pltpu.VMEM((tk, NSA_HG * tq), F32),
                            pltpu.VMEM((tk, NSA_HG * tq), F32),
                            pltpu.SMEM((n_tiles + 1,), jnp.int32)]),
        out_shape=jax.ShapeDtypeStruct((B, S, 4 * LANE), BF16),
        compiler_params=_params(("arbitrary", "arbitrary", "arbitrary")),
        name="nsa_sel",
    )(flags, q, ksd, vst, sel, gt, ocw)


def _mla_body(q_ref, k_ref, vt_ref, o_ref, m_sc, acc_sc, sa_sc, sb_sc, *, tq, tc, nh):
    n_full = pl.program_id(2)
    m_sc[...] = jnp.full(m_sc.shape, NEG, F32)
    acc_sc[...] = jnp.zeros(acc_sc.shape, F32)

    def scores(j, s_ref):
        k0 = pl.multiple_of(j * tq, tq)
        for hh in range(nh):
            lanes = slice(LANE * hh, LANE * (hh + 1))
            s_ref[hh] = _dot_nt(k_ref[0, pl.ds(k0, tq), lanes], q_ref[0, :, lanes])

    def consume(j, s_ref, diagonal):
        k0 = pl.multiple_of(j * tq, tq)
        for hh in range(nh):
            vt = _with_ones(vt_ref[0, MLA_V * hh:MLA_V * (hh + 1), pl.ds(k0, tq)])
            for c in range(tq // tc):
                cols = slice(c * tc, (c + 1) * tc)
                st = s_ref[hh, :, cols]
                if diagonal:
                    causal = (lax.broadcasted_iota(jnp.int32, (tq, tc), 0)
                              <= lax.broadcasted_iota(jnp.int32, (tq, tc), 1) + c * tc)
                    st = jnp.where(causal, st, NEG)
                m_prev = m_sc[hh, :, cols]
                m_new = jnp.maximum(m_prev, jnp.max(st, axis=0, keepdims=True))
                alpha = jnp.exp2(m_prev - m_new)
                p = jnp.exp2(st - m_new)
                acc_sc[hh, :, cols] = alpha * acc_sc[hh, :, cols] + _dot(vt, p.astype(BF16))
                m_sc[hh, :, cols] = m_new

    scores(0, sa_sc)

    def pair(jj, carry):
        j = 2 * jj
        scores(j + 1, sb_sc)
        consume(j, sa_sc, False)
        scores(j + 2, sa_sc)
        consume(j + 1, sb_sc, False)
        return carry

    lax.fori_loop(0, n_full // 2, pair, 0)

    @pl.when(n_full % 2 == 0)
    def _():
        consume(n_full, sa_sc, True)

    @pl.when(n_full % 2 == 1)
    def _():
        scores(n_full, sb_sc)
        consume(n_full - 1, sa_sc, False)
        consume(n_full, sb_sc, True)

    for pair_i in range(nh // 2):
        halves = []
        for hh in (2 * pair_i, 2 * pair_i + 1):
            acc = acc_sc[hh]
            halves.append(acc[0:MLA_V] * (1.0 / jnp.maximum(acc[MLA_V:MLA_V + 1], 1e-30)))
        o_ref[0, :, LANE * pair_i:LANE * (pair_i + 1)] = jnp.concatenate(halves, axis=0).T.astype(BF16)


def _mla(qm, km, vmt, tq, nh):
    B, S, _ = qm.shape
    return pl.pallas_call(
        functools.partial(_mla_body, tq=tq, tc=min(256, tq), nh=nh),
        grid=(B, MLA_HEADS // nh, S // tq),
        in_specs=[pl.BlockSpec((1, tq, nh * LANE), lambda b, p, i: (b, i, p)),
                  pl.BlockSpec((1, S, nh * LANE), lambda b, p, i: (b, 0, p)),
                  pl.BlockSpec((1, nh * MLA_V, S), lambda b, p, i: (b, p, 0))],
        out_specs=pl.BlockSpec((1, tq, nh * MLA_V), lambda b, p, i: (b, i, p)),
        out_shape=jax.ShapeDtypeStruct((B, S, MLA_HEADS * MLA_V), BF16),
        scratch_shapes=[pltpu.VMEM((nh, 1, tq), F32), pltpu.VMEM((nh, MLA_V + ONES_ROWS, tq), F32),
                        pltpu.VMEM((nh, tq, tq), F32), pltpu.VMEM((nh, tq, tq), F32)],
        compiler_params=_params(("parallel", "parallel", "arbitrary")),
        name="mla",
    )(qm, km, vmt)


def _merge_mlp_body(on_ref, om_ref, ga_ref, gb_ref, x_ref, mod_ref, won_ref, wom_ref, wout_ref,
                    gmlp_ref, w1_ref, w2_ref, gfin_ref, o_ref, *, fc, final):
    y = (ga_ref[0].astype(F32) * _dot(on_ref[0], won_ref[...])
         + gb_ref[0].astype(F32) * _dot(om_ref[0], wom_ref[...]))
    x1 = x_ref[0] + mod_ref[0, 2:3, :] * _dot(y.astype(BF16), wout_ref[...])
    h = _rms(x1, gmlp_ref[...]) * (1.0 + mod_ref[0, 4:5, :]) + mod_ref[0, 3:4, :]
    hb = h.astype(BF16)
    acc = jnp.zeros(x1.shape, F32)
    for c in range(w1_ref.shape[1] // fc):
        a = jnp.maximum(_dot(hb, w1_ref[:, c * fc:(c + 1) * fc]), 0.0)
        acc = acc + _dot((a * a).astype(BF16), w2_ref[c * fc:(c + 1) * fc, :])
    x2 = x1 + mod_ref[0, 5:6, :] * acc
    o_ref[0] = _rms(x2, gfin_ref[...]) if final else x2


def _merge_mlp(o_nsa, o_mla, ga, gb, x, mod, w_on, w_om, w_out, g_mlp, w1, w2, g_fin, tm, fc, final):
    B, S, D = x.shape
    row = lambda n: pl.BlockSpec((1, tm, n), lambda b, i: (b, i, 0))
    full = lambda a: pl.BlockSpec(a.shape, lambda b, i: (0,) * a.ndim, pipeline_mode=pl.Buffered(1))
    return pl.pallas_call(
        functools.partial(_merge_mlp_body, fc=fc, final=final),
        grid=(B, S // tm),
        in_specs=[row(o_nsa.shape[2]), row(o_mla.shape[2]), row(D), row(D), row(D),
                  pl.BlockSpec((1, 6, D), lambda b, i: (b, 0, 0)), full(w_on), full(w_om), full(w_out),
                  full(g_mlp), full(w1), full(w2), full(g_fin)],
        out_specs=row(D),
        out_shape=jax.ShapeDtypeStruct((B, S, D), F32),
        compiler_params=_params(("parallel", "parallel")),
        name="merge_mlp",
    )(o_nsa, o_mla, ga, gb, x, mod, w_on, w_om, w_out, g_mlp, w1, w2, g_fin)


def _inproj_weight(w_in):
    d = w_in.shape[0]
    o = 0
    parts = {}
    for name, n in (("q", 512), ("kc", 128), ("vc", 128), ("ks", 128), ("vs", 128), ("kw", 128),
                    ("vw", 128), ("g", 24), ("qd", MLA_Q_RANK), ("kvd", MLA_KV_RANK), ("kr", MLA_ROPE),
                    ("ga", D_MODEL), ("gb", D_MODEL)):
        parts[name] = w_in[:, o:o + n]
        o += n
    wg = parts["g"].reshape(d, 3, NSA_GROUPS, NSA_HG).transpose(0, 2, 1, 3).reshape(d, NSA_GROUPS, 3 * NSA_HG)
    wg = jnp.pad(wg, ((0, 0), (0, 0), (0, GATE_ROWS - 3 * NSA_HG))).reshape(d, NSA_GROUPS * GATE_ROWS)
    wg = jnp.pad(wg, ((0, 0), (0, LANE - NSA_GROUPS * GATE_ROWS)))
    kr = jnp.pad(parts["kr"], ((0, 0), (MLA_NOPE, LANE - MLA_NOPE - MLA_ROPE)))
    w_all = jnp.concatenate(
        [p.astype(BF16) for p in
         (parts["q"], parts["kc"], parts["vc"], parts["ks"], parts["kw"], parts["vs"], parts["vw"], wg, kr,
          parts["qd"], parts["kvd"], parts["ga"], parts["gb"])], axis=1)
    assert w_all.shape[1] == IN_COLS
    return w_all


def _mla_q_weight(w_uq):
    r = w_uq.shape[0]
    w = w_uq.reshape(r, MLA_HEADS, MLA_NOPE + MLA_ROPE)
    w = jnp.pad(w, ((0, 0), (0, 0), (0, LANE - MLA_NOPE - MLA_ROPE)))
    return w.reshape(r, MLA_HEADS * LANE).astype(BF16)


def _mla_kv_weight(w_uk, w_uv):
    r = w_uk.shape[0]
    k = w_uk.reshape(r, MLA_HEADS, MLA_NOPE)
    k = jnp.pad(k, ((0, 0), (0, 0), (0, LANE - MLA_NOPE))).reshape(r, MLA_HEADS * LANE)
    return jnp.concatenate([k, w_uv], axis=1).astype(BF16)


def _rope_tables(seq):
    half = MLA_ROPE // 2
    pos = jnp.arange(seq, dtype=F32)
    inv_freq = ROPE_THETA ** (-jnp.arange(half, dtype=F32) / half)
    ang = pos[:, None] * inv_freq[None, :]
    cos, sin = jnp.cos(ang), jnp.sin(ang)
    ones = jnp.ones((seq, MLA_NOPE), F32)
    z_lo = jnp.zeros((seq, MLA_NOPE), F32)
    z_hi = jnp.zeros((seq, LANE - MLA_NOPE - MLA_ROPE), F32)
    return (jnp.concatenate([ones, cos, cos, z_hi], axis=1),
            jnp.concatenate([z_lo, sin, sin, z_hi], axis=1))


def _compress_weights(pe, w1, w2):
    hid = w1.shape[1]
    wa = w1[:CMP_STRIDE * NSA_DH].reshape(CMP_STRIDE, NSA_DH, hid)
    wb = w1[CMP_STRIDE * NSA_DH:].reshape(CMP_STRIDE, NSA_DH, hid)
    z = jnp.zeros_like(wa)
    cols = []
    for g in range(NSA_GROUPS):
        for w in (wa, wb):
            slots = [z, z]
            slots[g] = w
            cols.append(jnp.stack(slots, axis=1).reshape(CMP_STRIDE * NSA_GROUPS * NSA_DH, hid))
    w1big = jnp.concatenate(cols, axis=1).astype(BF16)
    pe8 = jnp.zeros((8, CMP_LEN * NSA_DH), F32).at[0].set(pe.reshape(-1))
    w2d = jnp.concatenate([w2, w2], axis=1).astype(BF16)
    return w1big, pe8, w1, w2d


def _overlap_table(seq):
    ns = seq // SEL_BLOCK
    nsp = -(-ns // LANE) * LANE
    ncp = seq // CMP_STRIDE
    s_id = jnp.arange(nsp)[:, None]
    c_id = jnp.arange(ncp)[None, :]
    c_start = c_id * CMP_STRIDE
    return ((c_start < s_id * SEL_BLOCK + SEL_BLOCK) & (c_start + CMP_LEN - 1 >= s_id * SEL_BLOCK)
            & (c_id < ncp - 1)).astype(BF16)


def _tiles(seq):
    tm_in = min(512, seq)
    tm_out = min(512, seq)
    tq_cw = min(1024, seq)
    tq_sel = min(256, seq)
    tk_sel = min(256, seq)
    tq_mla = min(512, seq)
    mla_heads_per_step = 4
    ff_chunk = 1024
    return tm_in, tm_out, tq_cw, tq_sel, tk_sel, tq_mla, mla_heads_per_step, ff_chunk


def kernel(x, c, w_ada, b_ada, g_mix, w_in, pe_ck, w_ck1, w_ck2, pe_cv, w_cv1, w_cv2, g_cq, w_uq, g_ckv,
           w_uk, w_uv, w_o_nsa, w_o_mla, w_out, g_mlp, w_fc1, w_fc2, g_final):
    B, S, D = x.shape
    depth = w_ada.shape[0]
    tm_in, tm_out, tq_cw, tq_nsa, tk_sel, tq_mla, mla_heads_per_step, ff_chunk = _tiles(S)
    cs, sn = _rope_tables(S)
    ovt = _overlap_table(S)
    gmat = (jnp.arange(ovt.shape[0])[:, None] // (tk_sel // SEL_BLOCK) == jnp.arange(LANE)[None, :]).astype(BF16)
    n_chunks = S // CMP_STRIDE
    for layer in range(depth):
        mod = _ada(c, w_ada[layer], b_ada[layer])
        (q, kc_in, vc_in, ksd, vst, kwd, vwt, gt, qm, km, vmt, ga, gb) = _inproj(
            x, mod, g_mix[layer][None], _inproj_weight(w_in[layer]), g_cq[layer][None],
            _mla_q_weight(w_uq[layer]), g_ckv[layer][None], _mla_kv_weight(w_uk[layer], w_uv[layer]),
            cs, sn, tm_in)
        w1k, pek, w1kf, w2k = _compress_weights(pe_ck[layer], w_ck1[layer], w_ck2[layer])
        w1v, pev, w1vf, w2v = _compress_weights(pe_cv[layer], w_cv1[layer], w_cv2[layer])
        width = CMP_STRIDE * NSA_GROUPS * NSA_DH
        kcd, vct = _compress(kc_in.reshape(B, n_chunks, width), vc_in.reshape(B, n_chunks, width),
                             w1k, w1v, pek, pev, w1kf, w1vf, w2k, w2v)
        ocw, sel, cnt = _nsa_cw(q, kcd, vct, kwd, vwt, gt, ovt, gmat, tq_cw, tq_nsa)
        flags = (cnt[:, :, :, 0, :S // tk_sel] > 0.5).astype(jnp.int32).reshape(-1)
        o_nsa = _nsa_sel(flags, q, ksd, vst, sel, gt, ocw, tq_nsa, tk_sel)
        o_mla = _mla(qm, km, vmt, tq_mla, mla_heads_per_step)
        x = _merge_mlp(o_nsa, o_mla, ga, gb, x, mod, w_o_nsa[layer].astype(BF16), w_o_mla[layer].astype(BF16),
                       w_out[layer].astype(BF16), g_mlp[layer][None], w_fc1[layer].astype(BF16),
                       w_fc2[layer].astype(BF16), g_final[None], tm_out, ff_chunk, layer == depth - 1)
    return x
```

```python
import functools
import math

import jax
import jax.numpy as jnp
from jax import lax
from jax.experimental import pallas as pl
from jax.experimental.pallas import tpu as pltpu

F32 = jnp.float32
BF16 = jnp.bfloat16

D_MODEL = 1024
NSA_GROUPS = 2
NSA_HG = 4
NSA_DH = 64
CMP_LEN = 32
CMP_STRIDE = 16
CMP_HIDDEN = 256
SEL_BLOCK = 64
SEL_TOPK = 16
WINDOW = 512
MLA_HEADS = 8
MLA_NOPE = 64
MLA_ROPE = 32
MLA_V = 64
MLA_Q_RANK = 384
MLA_KV_RANK = 256
ROPE_THETA = 10000.0
D_FF = 4 * D_MODEL
EPS = 1e-6
FORCE_BONUS = 1000.0

LANE = 128
HALF = LANE // 2
NEG = -1e30
LOG2E = math.log2(math.e)
VMEM_LIMIT = 52 * 1024 * 1024
GATE_ROWS = 16
ONES_ROWS = 16
MXU_COLS = 256
OFF_Q = 0
OFF_KC_VC = 512
OFF_KS_KW = 768
OFF_VS_VW = 1024
OFF_G_KR = 1280
OFF_QD = 1536
OFF_KVD = 1920
OFF_GA = 2176
OFF_GB = 3200
IN_COLS = 4224


def _dot(a, b):
    return jnp.dot(a, b, preferred_element_type=F32)


def _dot_nt(a, b):
    return lax.dot_general(a, b, (((1,), (1,)), ((), ())), preferred_element_type=F32)


def _rms(x, g):
    return x * lax.rsqrt(jnp.mean(x * x, axis=-1, keepdims=True) + EPS) * g


def _sigmoid(x):
    return 1.0 / (1.0 + jnp.exp(-x))


def _params(sem):
    return pltpu.CompilerParams(dimension_semantics=sem, vmem_limit_bytes=VMEM_LIMIT)


def _ada_body(c_ref, w_ref, b_ref, o_ref):
    o_ref[...] = jnp.dot(c_ref[...], w_ref[...], preferred_element_type=F32,
                         precision=lax.Precision.HIGHEST) + b_ref[...]


def _ada(c, w, b):
    nb, d = c.shape
    n = w.shape[1]
    rows = 8
    cp = jnp.zeros((rows, d), F32).at[:nb].set(c)
    tn = 1536
    out = pl.pallas_call(
        _ada_body,
        grid=(n // tn,),
        in_specs=[pl.BlockSpec((rows, d), lambda j: (0, 0)),
                  pl.BlockSpec((d, tn), lambda j: (0, j)),
                  pl.BlockSpec((1, tn), lambda j: (0, j))],
        out_specs=pl.BlockSpec((rows, tn), lambda j: (0, j)),
        out_shape=jax.ShapeDtypeStruct((rows, n), F32),
        compiler_params=_params(("arbitrary",)),
        name="ada",
    )(cp, w, b.reshape(1, n))
    return out[:nb].reshape(nb, 6, d)


def _inproj_body(x_ref, mod_ref, gmix_ref, w_ref, gcq_ref, wq2_ref, gckv_ref, wkv_ref,
                 cs_ref, sn_ref,
                 q_ref, kc_ref, vc_ref, ks_ref, vst_ref, kw_ref, vwt_ref, gt_ref,
                 qm_ref, km_ref, vmt_ref, ga_ref, gb_ref):
    x = x_ref[0]
    h = _rms(x, gmix_ref[...]) * (1.0 + mod_ref[0, 1:2, :]) + mod_ref[0, 0:1, :]
    hb = h.astype(BF16)

    def proj(lo, n):
        return _dot(hb, w_ref[:, lo:lo + n])

    lane = lax.broadcasted_iota(jnp.int32, (x.shape[0], LANE), 1)
    half = MLA_ROPE // 2

    def dup_groups(z):
        swapped = pltpu.roll(z, HALF, 1)
        lower = lane < HALF
        return jnp.concatenate([jnp.where(lower, z, swapped), jnp.where(lower, swapped, z)], axis=1)

    def rotary(a):
        partner = jnp.where(lane < MLA_NOPE + half, -pltpu.roll(a, LANE - half, 1), pltpu.roll(a, half, 1))
        return a * cs_ref[...] + partner * sn_ref[...]

    q_ref[0] = (proj(OFF_Q, 512) * (NSA_DH ** -0.5 * LOG2E)).astype(BF16)
    z = proj(OFF_KC_VC, MXU_COLS)
    kc_ref[0] = z[:, 0:LANE].astype(BF16)
    vc_ref[0] = z[:, LANE:].astype(BF16)
    z = proj(OFF_KS_KW, MXU_COLS)
    ks_ref[0] = dup_groups(z[:, 0:LANE]).astype(BF16)
    kw_ref[0] = dup_groups(z[:, LANE:]).astype(BF16)
    z = proj(OFF_VS_VW, MXU_COLS)
    vst_ref[0] = z[:, 0:LANE].T.astype(BF16)
    vwt_ref[0] = z[:, LANE:].T.astype(BF16)
    z = proj(OFF_G_KR, MXU_COLS)
    gt_ref[0] = _sigmoid(z[:, 0:LANE]).T[0:NSA_GROUPS * GATE_ROWS]
    kr = rotary(z[:, LANE:])
    ga_ref[0] = _sigmoid(proj(OFF_GA, 1024)).astype(BF16)
    gb_ref[0] = _sigmoid(proj(OFF_GB, 1024)).astype(BF16)

    nh = MLA_HEADS * LANE
    cq = _rms(proj(OFF_QD, MLA_Q_RANK), gcq_ref[...]).astype(BF16)
    q2 = _dot(cq, wq2_ref[...])
    scale = (MLA_NOPE + MLA_ROPE) ** -0.5 * LOG2E
    for hh in range(MLA_HEADS):
        qm_ref[0, :, LANE * hh:LANE * (hh + 1)] = (rotary(q2[:, LANE * hh:LANE * (hh + 1)]) * scale).astype(BF16)
    ckv = _rms(proj(OFF_KVD, MLA_KV_RANK), gckv_ref[...]).astype(BF16)
    kv = _dot(ckv, wkv_ref[...])
    for hh in range(MLA_HEADS):
        km_ref[0, :, LANE * hh:LANE * (hh + 1)] = (kv[:, LANE * hh:LANE * (hh + 1)] + kr).astype(BF16)
    vmt_ref[0] = kv[:, nh:].T.astype(BF16)


def _inproj(x, mod, g_mix, w_all, g_cq, wq2, g_ckv, wkv, cs, sn, tm):
    B, S, D = x.shape
    row = lambda n: pl.BlockSpec((1, tm, n), lambda b, i: (b, i, 0))
    col = lambda n: pl.BlockSpec((1, n, tm), lambda b, i: (b, 0, i))
    full = lambda a: pl.BlockSpec(a.shape, lambda b, i: (0,) * a.ndim)
    tok = lambda n, dt: (row(n), jax.ShapeDtypeStruct((B, S, n), dt))
    tr = lambda n, dt: (col(n), jax.ShapeDtypeStruct((B, n, S), dt))
    outs = [tok(512, BF16), tok(128, BF16), tok(128, BF16), tok(256, BF16), tr(LANE, BF16),
            tok(256, BF16), tr(LANE, BF16), tr(NSA_GROUPS * GATE_ROWS, F32),
            tok(1024, BF16), tok(1024, BF16), tr(MLA_HEADS * MLA_V, BF16), tok(1024, BF16), tok(1024, BF16)]
    return pl.pallas_call(
        _inproj_body,
        grid=(B, S // tm),
        in_specs=[row(D),
                  pl.BlockSpec((1, 6, D), lambda b, i: (b, 0, 0)),
                  full(g_mix), full(w_all), full(g_cq), full(wq2), full(g_ckv), full(wkv),
                  pl.BlockSpec((tm, LANE), lambda b, i: (i, 0)),
                  pl.BlockSpec((tm, LANE), lambda b, i: (i, 0))],
        out_specs=[o[0] for o in outs],
        out_shape=[o[1] for o in outs],
        compiler_params=_params(("parallel", "parallel")),
        name="inproj",
    )(x, mod, g_mix, w_all, g_cq, wq2, g_ckv, wkv, cs, sn)


def _compress_one(x, w1big, pe, w1f, w2d):
    hcat = _dot(x, w1big)
    const = jnp.dot(pe, w1f, preferred_element_type=F32, precision=lax.Precision.HIGHEST)[0:1]
    n = x.shape[0]
    rowid = lax.broadcasted_iota(jnp.int32, (n, LANE), 0)
    hid_w = CMP_HIDDEN
    outs = []
    for g in range(NSA_GROUPS):
        a = hcat[:, 2 * hid_w * g:2 * hid_w * g + hid_w]
        b = hcat[:, 2 * hid_w * g + hid_w:2 * hid_w * (g + 1)]
        hid = a + pltpu.roll(b, n - 1, 0) + const
        act = hid * _sigmoid(hid)
        out = _dot(act.astype(BF16), w2d)
        outs.append(jnp.where(rowid < n - 1, out, 0.0))
    return outs


def _compress_body(xk_ref, xv_ref, w1k_ref, w1v_ref, pek_ref, pev_ref, w1kf_ref, w1vf_ref,
                   w2k_ref, w2v_ref, kc_ref, vct_ref):
    kc = _compress_one(xk_ref[0], w1k_ref[...], pek_ref[...], w1kf_ref[...], w2k_ref[...])
    vc = _compress_one(xv_ref[0], w1v_ref[...], pev_ref[...], w1vf_ref[...], w2v_ref[...])
    for g in range(NSA_GROUPS):
        kc_ref[0, g] = kc[g].astype(BF16)
        vct_ref[0, g] = vc[g].T[0:NSA_DH].astype(BF16)


def _compress(xk, xv, w1k, w1v, pek, pev, w1kf, w1vf, w2k, w2v):
    B, n, width = xk.shape
    full = lambda a: pl.BlockSpec(a.shape, lambda b: (0,) * a.ndim)
    xs = pl.BlockSpec((1, n, width), lambda b: (b, 0, 0))
    return pl.pallas_call(
        _compress_body,
        grid=(B,),
        in_specs=[xs, xs, full(w1k), full(w1v), full(pek), full(pev), full(w1kf), full(w1vf),
                  full(w2k), full(w2v)],
        out_specs=[pl.BlockSpec((1, NSA_GROUPS, n, LANE), lambda b: (b, 0, 0, 0)),
                   pl.BlockSpec((1, NSA_GROUPS, NSA_DH, n), lambda b: (b, 0, 0, 0))],
        out_shape=[jax.ShapeDtypeStruct((B, NSA_GROUPS, n, LANE), BF16),
                   jax.ShapeDtypeStruct((B, NSA_GROUPS, NSA_DH, n), BF16)],
        compiler_params=_params(("parallel",)),
        name="compress",
    )(xk, xv, w1k, w1v, pek, pev, w1kf, w1vf, w2k, w2v)


def _stack_heads(q_ref, tq):
    upper = lax.broadcasted_iota(jnp.int32, (tq, LANE), 1) >= HALF
    parts = []
    for hh in range(NSA_HG):
        slab = q_ref[0, :, LANE * (hh // 2):LANE * (hh // 2 + 1)]
        keep = upper if hh % 2 else jnp.logical_not(upper)
        parts.append(jnp.where(keep, slab, jnp.zeros_like(slab)))
    return jnp.concatenate(parts, axis=0)


def _slope(g, hh):
    return jnp.where(g == 0, 1.0, 2.0 ** -NSA_HG).astype(F32) * (2.0 ** -(hh + 1) * LOG2E)


def _with_ones(vt):
    return jnp.concatenate([vt, jnp.ones((ONES_ROWS, vt.shape[1]), vt.dtype)], axis=0)


def _nsa_cw_body(q_ref, kc_ref, vct_ref, kw_ref, vwt_ref, gt_ref, ovt_ref, gm_ref,
                 ocw_ref, sel_ref, cnt_ref, bc_sc, bw_sc, qst_sc, *, tq, tw, tf, seq, n_parts):
    g = pl.program_id(1)
    i = pl.program_id(2)
    t0 = i * tq
    qst_sc[...] = _stack_heads(q_ref, tq)
    gates = gt_ref[0]
    ncp = kc_ref.shape[2]
    nsp = ovt_ref.shape[0]
    wk = WINDOW + tw

    @pl.when(i == 0)
    def _():
        c_end_f = (lax.broadcasted_iota(jnp.int32, (ncp, tq), 0) * CMP_STRIDE + (CMP_LEN - 1)).astype(F32)
        key_f = lax.broadcasted_iota(jnp.int32, (wk, tw), 0).astype(F32)
        for hh in range(NSA_HG):
            bc_sc[hh] = _slope(g, hh) * c_end_f
            bw_sc[hh] = _slope(g, hh) * key_f

    def tile_body(ncp_e, nsp_e):
        vct = _with_ones(vct_ref[0, 0, :, 0:ncp_e])
        st = _dot_nt(kc_ref[0, 0, 0:ncp_e, :], qst_sc[...])
        t_c = t0 + lax.broadcasted_iota(jnp.int32, (ncp_e, tq), 1)
        c_end = lax.broadcasted_iota(jnp.int32, (ncp_e, tq), 0) * CMP_STRIDE + (CMP_LEN - 1)
        madd_c = jnp.where(t_c >= c_end, 0.0, NEG)
        psum = jnp.zeros((ncp_e, tq), F32)
        o_cmp = []
        for hh in range(NSA_HG):
            z = st[:, hh * tq:(hh + 1) * tq] + bc_sc[hh, 0:ncp_e] + madd_c
            m = jnp.max(z, axis=0, keepdims=True)
            e = jnp.exp2(z - m)
            o_aug = _dot(vct, e.astype(BF16))
            r = jnp.where(m > 0.5 * NEG, 1.0 / jnp.maximum(o_aug[NSA_DH:NSA_DH + 1], 1e-30), 0.0)
            psum = psum + e * r
            o_cmp.append(o_aug[0:NSA_DH] * r)

        ovt = ovt_ref[0:nsp_e, 0:ncp_e]
        p1 = psum.astype(BF16)
        r1 = psum - p1.astype(F32)
        p2 = r1.astype(BF16)
        p3 = (r1 - p2.astype(F32)).astype(BF16)
        p_slc = _dot(ovt, p1) + _dot(ovt, p2) + _dot(ovt, p3)

        n_forced = 3
        sels = []
        for c in range(tq // LANE):
            blk = lax.broadcasted_iota(jnp.int32, (nsp_e, LANE), 0)
            cur = (t0 + c * LANE + lax.broadcasted_iota(jnp.int32, (nsp_e, LANE), 1)) // SEL_BLOCK
            valid = blk <= cur
            forced = (blk == 0) | (blk >= cur - 1)
            score = jnp.where(valid & jnp.logical_not(forced), p_slc[:, c * LANE:(c + 1) * LANE], -1.0)
            for _ in range(max(min(SEL_TOPK, seq // SEL_BLOCK) - n_forced, 0)):
                m = jnp.max(score, axis=0, keepdims=True)
                first = jnp.min(jnp.where(score == m, blk, nsp), axis=0, keepdims=True)
                score = jnp.where(blk == first, -1.0, score)
            sels.append(jnp.where(valid & (score < 0.0), 1.0, 0.0))
        sel = jnp.concatenate(sels, axis=1)
        sel_ref[0, 0, 0:nsp_e] = sel
        if nsp_e < nsp:
            sel_ref[0, 0, nsp_e:nsp] = jnp.zeros((nsp - nsp_e, tq), F32)
        for f in range(tq // tf):
            per_blk = _dot_nt(jnp.ones((8, tf), BF16), sel[:, f * tf:(f + 1) * tf].astype(BF16))
            cnt_ref[0, 0, f] = _dot(per_blk.astype(BF16), gm_ref[0:nsp_e])

        for sub in range(tq // tw):
            k_start = pl.multiple_of(jnp.clip(t0 + sub * tw - WINDOW, 0, seq - wk), LANE)
            kw = kw_ref[0, pl.ds(k_start, wk), :]
            vwt = _with_ones(vwt_ref[0, :, pl.ds(k_start, wk)])
            qsub = jnp.concatenate(
                [qst_sc[hh * tq + sub * tw:hh * tq + (sub + 1) * tw] for hh in range(NSA_HG)], axis=0)
            st = _dot_nt(kw, qsub)
            t_w = t0 + sub * tw + lax.broadcasted_iota(jnp.int32, (wk, tw), 1)
            dist_i = t_w - (k_start + lax.broadcasted_iota(jnp.int32, (wk, tw), 0))
            madd_w = jnp.where((dist_i >= 0) & (dist_i < WINDOW), 0.0, NEG)
            cols = slice(sub * tw, (sub + 1) * tw)
            for hh in range(NSA_HG):
                z = st[:, hh * tw:(hh + 1) * tw] + bw_sc[hh] + madd_w
                e = jnp.exp2(z - jnp.max(z, axis=0, keepdims=True))
                o_aug = _dot(vwt, e.astype(BF16))
                o_win = o_aug[0:NSA_DH] * (1.0 / jnp.maximum(o_aug[NSA_DH:NSA_DH + 1], 1e-30))
                ocw_ref[0, 0, NSA_DH * hh:NSA_DH * (hh + 1), cols] = (
                    gates[hh:hh + 1, cols] * o_cmp[hh][:, cols]
                    + gates[2 * NSA_HG + hh:2 * NSA_HG + hh + 1, cols] * o_win)

    part = (i * n_parts) // pl.num_programs(2)
    for v in range(n_parts):
        pl.when(part == v)(functools.partial(tile_body, ncp * (v + 1) // n_parts, nsp * (v + 1) // n_parts))


def _cw_parts(nq, ncp, nsp):
    for n in (4, 2):
        if nq % n == 0 and (ncp // n) % LANE == 0 and (nsp // n) % 8 == 0:
            return n
    return 1


def _nsa_cw(q, kcd, vct, kwd, vwt, gt, ovt, gmat, tq, tf):
    B, S, _ = q.shape
    ncp = kcd.shape[2]
    nsp = ovt.shape[0]
    nq = S // tq
    tw = min(MXU_COLS, tq)
    return pl.pallas_call(
        functools.partial(_nsa_cw_body, tq=tq, tw=tw, tf=tf, seq=S, n_parts=_cw_parts(nq, ncp, nsp)),
        grid=(B, NSA_GROUPS, nq),
        in_specs=[pl.BlockSpec((1, tq, 2 * LANE), lambda b, g, i: (b, i, g)),
                  pl.BlockSpec((1, 1, ncp, LANE), lambda b, g, i: (b, g, 0, 0)),
                  pl.BlockSpec((1, 1, NSA_DH, ncp), lambda b, g, i: (b, g, 0, 0)),
                  pl.BlockSpec((1, S, LANE), lambda b, g, i: (b, 0, g)),
                  pl.BlockSpec((1, NSA_DH, S), lambda b, g, i: (b, g, 0)),
                  pl.BlockSpec((1, GATE_ROWS, tq), lambda b, g, i: (b, g, i)),
                  pl.BlockSpec(ovt.shape, lambda b, g, i: (0, 0)),
                  pl.BlockSpec(gmat.shape, lambda b, g, i: (0, 0))],
        out_specs=[pl.BlockSpec((1, 1, NSA_HG * NSA_DH, tq), lambda b, g, i: (b, g, 0, i)),
                   pl.BlockSpec((1, 1, nsp, tq), lambda b, g, i: (b, g, 0, i)),
                   pl.BlockSpec((1, 1, tq // tf, 8, LANE), lambda b, g, i: (b, g, i, 0, 0))],
        out_shape=[jax.ShapeDtypeStruct((B, NSA_GROUPS, NSA_HG * NSA_DH, S), F32),
                   jax.ShapeDtypeStruct((B, NSA_GROUPS, nsp, S), F32),
                   jax.ShapeDtypeStruct((B, NSA_GROUPS, S // tf, 8, LANE), F32)],
        scratch_shapes=[pltpu.VMEM((NSA_HG, ncp, tq), F32), pltpu.VMEM((NSA_HG, WINDOW + tw, tw), F32),
                        pltpu.VMEM((NSA_HG * tq, LANE), BF16)],
        compiler_params=_params(("arbitrary", "arbitrary", "arbitrary")),
        name="nsa_cw",
    )(q, kcd, vct, kwd, vwt, gt, ovt, gmat)


def _nsa_sel_body(flag_ref, q_ref, ks_ref, vst_ref, sel_ref, gt_ref, ocw_ref, o_ref,
                  m_sc, acc_sc, skb_sc, qst_sc, sa_sc, sb_sc, lst_sc, *, tq, tk, n_tiles):
    g = pl.program_id(1)
    i = pl.program_id(2)
    t0 = i * tq
    j_diag = t0 // tk

    @pl.when(i == 0)
    def _():
        key_f = lax.broadcasted_iota(jnp.int32, (tk, tq), 0).astype(F32)
        for hh in range(NSA_HG):
            skb_sc[hh] = _slope(g, hh) * key_f

    qst_sc[...] = _stack_heads(q_ref, tq)
    m_sc[...] = jnp.full(m_sc.shape, NEG, F32)
    acc_sc[...] = jnp.zeros(acc_sc.shape, F32)

    base = ((pl.program_id(0) * NSA_GROUPS + g) * pl.num_programs(2) + i) * n_tiles
    n_act = jnp.int32(0)
    for j in range(n_tiles):
        lst_sc[n_act] = j
        n_act = n_act + jnp.where((flag_ref[base + j] > 0) & (j < j_diag), 1, 0)
    lst_sc[n_act] = j_diag

    def scores(j, s_ref):
        k0 = pl.multiple_of(j * tk, tk)
        s_ref[...] = _dot_nt(ks_ref[0, pl.ds(k0, tk), :], qst_sc[...])

    def consume(j, s_ref, diagonal):
        k0 = pl.multiple_of(j * tk, tk)
        vt = _with_ones(vst_ref[0, :, pl.ds(k0, tk)])
        b0 = j * (tk // SEL_BLOCK)
        pieces = []
        for r in range(tk // SEL_BLOCK):
            chosen = sel_ref[0, 0, pl.ds(b0 + r, 1), :]
            pieces.append(jnp.broadcast_to(jnp.where(chosen > 0.5, 0.0, NEG), (SEL_BLOCK, tq)))
        madd = jnp.concatenate(pieces, axis=0)
        if diagonal:
            key_i = lax.broadcasted_iota(jnp.int32, (tk, tq), 0)
            qry_i = lax.broadcasted_iota(jnp.int32, (tk, tq), 1)
            madd = jnp.where(key_i + k0 <= qry_i + t0, madd, NEG)
        for hh in range(NSA_HG):
            c = _slope(g, hh) * (k0 - t0).astype(F32)
            z = s_ref[:, hh * tq:(hh + 1) * tq] + skb_sc[hh] + madd
            m_prev = m_sc[hh]
            m_new = jnp.maximum(m_prev, jnp.max(z, axis=0, keepdims=True) + c)
            alpha = jnp.exp2(m_prev - m_new)
            p = jnp.exp2(z - (m_new - c))
            acc_sc[hh] = alpha * acc_sc[hh] + _dot(vt, p.astype(BF16))
            m_sc[hh] = m_new

    scores(lst_sc[0], sa_sc)

    def pair(jj, carry):
        it = 2 * jj
        scores(lst_sc[it + 1], sb_sc)
        consume(lst_sc[it], sa_sc, False)
        scores(lst_sc[it + 2], sa_sc)
        consume(lst_sc[it + 1], sb_sc, False)
        return carry

    lax.fori_loop(0, n_act // 2, pair, 0)

    @pl.when(n_act % 2 == 0)
    def _():
        consume(j_diag, sa_sc, True)

    @pl.when(n_act % 2 == 1)
    def _():
        scores(j_diag, sb_sc)
        consume(lst_sc[n_act - 1], sa_sc, False)
        consume(j_diag, sb_sc, True)

    gates = gt_ref[0]
    for pair_i in range(NSA_HG // 2):
        halves = []
        for hh in (2 * pair_i, 2 * pair_i + 1):
            acc = acc_sc[hh]
            w = gates[NSA_HG + hh:NSA_HG + hh + 1] / jnp.maximum(acc[NSA_DH:NSA_DH + 1], 1e-30)
            halves.append(acc[0:NSA_DH] * w + ocw_ref[0, 0, NSA_DH * hh:NSA_DH * (hh + 1), :])
        o_ref[0, :, LANE * pair_i:LANE * (pair_i + 1)] = jnp.concatenate(halves, axis=0).T.astype(BF16)


def _nsa_sel(flags, q, ksd, vst, sel, gt, ocw, tq, tk):
    B, S, _ = q.shape
    nsp = sel.shape[2]
    n_tiles = S // tk
    assert tk % tq == 0, "a query tile must lie inside one (the diagonal) key tile"
    qs = pl.BlockSpec((1, tq, 2 * LANE), lambda b, g, i, f: (b, i, g))
    return pl.pallas_call(
        functools.partial(_nsa_sel_body, tq=tq, tk=tk, n_tiles=n_tiles),
        grid_spec=pltpu.PrefetchScalarGridSpec(
            num_scalar_prefetch=1,
            grid=(B, NSA_GROUPS, S // tq),
            in_specs=[qs,
                      pl.BlockSpec((1, S, LANE), lambda b, g, i, f: (b, 0, g)),
                      pl.BlockSpec((1, NSA_DH, S), lambda b, g, i, f: (b, g, 0)),
                      pl.BlockSpec((1, 1, nsp, tq), lambda b, g, i, f: (b, g, 0, i)),
                      pl.BlockSpec((1, GATE_ROWS, tq), lambda b, g, i, f: (b, g, i)),
                      pl.BlockSpec((1, 1, NSA_HG * NSA_DH, tq), lambda b, g, i, f: (b, g, 0, i))],
            out_specs=qs,
            scratch_shapes=[pltpu.VMEM((NSA_HG, 1, tq), F32),
                            pltpu.VMEM((NSA_HG, NSA_DH + ONES_ROWS, tq), F32),
                            pltpu.VMEM((NSA_HG, tk, tq), F32),
                            pltpu.VMEM((NSA_HG * tq, LANE), BF16),
                            pltpu.VMEM((tk, NSA_HG * tq), F32),
                            pltpu.VMEM((tk, NSA_HG * tq), F32),
                            pltpu.SMEM((n_tiles + 1,), jnp.int32)]),
        out_shape=jax.ShapeDtypeStruct((B, S, 4 * LANE), BF16),
        compiler_params=_params(("arbitrary", "arbitrary", "arbitrary")),
        name="nsa_sel",
    )(flags, q, ksd, vst, sel, gt, ocw)


def _mla_body(q_ref, k_ref, vt_ref, o_ref, m_sc, acc_sc, sa_sc, sb_sc, *, tq, tc, nh):
    n_full = pl.program_id(2)
    m_sc[...] = jnp.full(m_sc.shape, NEG, F32)
    acc_sc[...] = jnp.zeros(acc_sc.shape, F32)

    def scores(j, s_ref):
        k0 = pl.multiple_of(j * tq, tq)
        for hh in range(nh):
            lanes = slice(LANE * hh, LANE * (hh + 1))
            s_ref[hh] = _dot_nt(k_ref[0, pl.ds(k0, tq), lanes], q_ref[0, :, lanes])

    def consume(j, s_ref, diagonal):
        k0 = pl.multiple_of(j * tq, tq)
        for hh in range(nh):
            vt = _with_ones(vt_ref[0, MLA_V * hh:MLA_V * (hh + 1), pl.ds(k0, tq)])
            for c in range(tq // tc):
                cols = slice(c * tc, (c + 1) * tc)
                st = s_ref[hh, :, cols]
                if diagonal:
                    causal = (lax.broadcasted_iota(jnp.int32, (tq, tc), 0)
                              <= lax.broadcasted_iota(jnp.int32, (tq, tc), 1) + c * tc)
                    st = jnp.where(causal, st, NEG)
                m_prev = m_sc[hh, :, cols]
                m_new = jnp.maximum(m_prev, jnp.max(st, axis=0, keepdims=True))
                alpha = jnp.exp2(m_prev - m_new)
                p = jnp.exp2(st - m_new)
                acc_sc[hh, :, cols] = alpha * acc_sc[hh, :, cols] + _dot(vt, p.astype(BF16))
                m_sc[hh, :, cols] = m_new

    scores(0, sa_sc)

    def pair(jj, carry):
        j = 2 * jj
        scores(j + 1, sb_sc)
        consume(j, sa_sc, False)
        scores(j + 2, sa_sc)
        consume(j + 1, sb_sc, False)
        return carry

    lax.fori_loop(0, n_full // 2, pair, 0)

    @pl.when(n_full % 2 == 0)
    def _():
        consume(n_full, sa_sc, True)

    @pl.when(n_full % 2 == 1)
    def _():
        scores(n_full, sb_sc)
        consume(n_full - 1, sa_sc, False)
        consume(n_full, sb_sc, True)

    for pair_i in range(nh // 2):
        halves = []
        for hh in (2 * pair_i, 2 * pair_i + 1):
            acc = acc_sc[hh]
            halves.append(acc[0:MLA_V] * (1.0 / jnp.maximum(acc[MLA_V:MLA_V + 1], 1e-30)))
        o_ref[0, :, LANE * pair_i:LANE * (pair_i + 1)] = jnp.concatenate(halves, axis=0).T.astype(BF16)


def _mla(qm, km, vmt, tq, nh):
    B, S, _ = qm.shape
    return pl.pallas_call(
        functools.partial(_mla_body, tq=tq, tc=min(256, tq), nh=nh),
        grid=(B, MLA_HEADS // nh, S // tq),
        in_specs=[pl.BlockSpec((1, tq, nh * LANE), lambda b, p, i: (b, i, p)),
                  pl.BlockSpec((1, S, nh * LANE), lambda b, p, i: (b, 0, p), pipeline_mode=pl.Buffered(1)),
                  pl.BlockSpec((1, nh * MLA_V, S), lambda b, p, i: (b, p, 0), pipeline_mode=pl.Buffered(1))],
        out_specs=pl.BlockSpec((1, tq, nh * MLA_V), lambda b, p, i: (b, i, p)),
        out_shape=jax.ShapeDtypeStruct((B, S, MLA_HEADS * MLA_V), BF16),
        scratch_shapes=[pltpu.VMEM((nh, 1, tq), F32), pltpu.VMEM((nh, MLA_V + ONES_ROWS, tq), F32),
                        pltpu.VMEM((nh, tq, tq), F32), pltpu.VMEM((nh, tq, tq), F32)],
        compiler_params=_params(("parallel", "parallel", "arbitrary")),
        name="mla",
    )(qm, km, vmt)


def _merge_mlp_body(on_ref, om_ref, ga_ref, gb_ref, x_ref, mod_ref, won_ref, wom_ref, wout_ref,
                    gmlp_ref, w1_ref, w2_ref, gfin_ref, o_ref, *, fc, final):
    y = (ga_ref[0].astype(F32) * _dot(on_ref[0], won_ref[...])
         + gb_ref[0].astype(F32) * _dot(om_ref[0], wom_ref[...]))
    x1 = x_ref[0] + mod_ref[0, 2:3, :] * _dot(y.astype(BF16), wout_ref[...])
    h = _rms(x1, gmlp_ref[...]) * (1.0 + mod_ref[0, 4:5, :]) + mod_ref[0, 3:4, :]
    hb = h.astype(BF16)
    acc = jnp.zeros(x1.shape, F32)
    for c in range(w1_ref.shape[1] // fc):
        a = jnp.maximum(_dot(hb, w1_ref[:, c * fc:(c + 1) * fc]), 0.0)
        acc = acc + _dot((a * a).astype(BF16), w2_ref[c * fc:(c + 1) * fc, :])
    x2 = x1 + mod_ref[0, 5:6, :] * acc
    o_ref[0] = _rms(x2, gfin_ref[...]) if final else x2


def _merge_mlp(o_nsa, o_mla, ga, gb, x, mod, w_on, w_om, w_out, g_mlp, w1, w2, g_fin, tm, fc, final):
    B, S, D = x.shape
    row = lambda n: pl.BlockSpec((1, tm, n), lambda b, i: (b, i, 0))
    full = lambda a: pl.BlockSpec(a.shape, lambda b, i: (0,) * a.ndim, pipeline_mode=pl.Buffered(1))
    return pl.pallas_call(
        functools.partial(_merge_mlp_body, fc=fc, final=final),
        grid=(B, S // tm),
        in_specs=[row(o_nsa.shape[2]), row(o_mla.shape[2]), row(D), row(D), row(D),
                  pl.BlockSpec((1, 6, D), lambda b, i: (b, 0, 0)), full(w_on), full(w_om), full(w_out),
                  full(g_mlp), full(w1), full(w2), full(g_fin)],
        out_specs=row(D),
        out_shape=jax.ShapeDtypeStruct((B, S, D), F32),
        compiler_params=_params(("parallel", "parallel")),
        name="merge_mlp",
    )(o_nsa, o_mla, ga, gb, x, mod, w_on, w_om, w_out, g_mlp, w1, w2, g_fin)


def _inproj_weight(w_in):
    d = w_in.shape[0]
    o = 0
    parts = {}
    for name, n in (("q", 512), ("kc", 128), ("vc", 128), ("ks", 128), ("vs", 128), ("kw", 128),
                    ("vw", 128), ("g", 24), ("qd", MLA_Q_RANK), ("kvd", MLA_KV_RANK), ("kr", MLA_ROPE),
                    ("ga", D_MODEL), ("gb", D_MODEL)):
        parts[name] = w_in[:, o:o + n]
        o += n
    wg = parts["g"].reshape(d, 3, NSA_GROUPS, NSA_HG).transpose(0, 2, 1, 3).reshape(d, NSA_GROUPS, 3 * NSA_HG)
    wg = jnp.pad(wg, ((0, 0), (0, 0), (0, GATE_ROWS - 3 * NSA_HG))).reshape(d, NSA_GROUPS * GATE_ROWS)
    wg = jnp.pad(wg, ((0, 0), (0, LANE - NSA_GROUPS * GATE_ROWS)))
    kr = jnp.pad(parts["kr"], ((0, 0), (MLA_NOPE, LANE - MLA_NOPE - MLA_ROPE)))
    w_all = jnp.concatenate(
        [p.astype(BF16) for p in
         (parts["q"], parts["kc"], parts["vc"], parts["ks"], parts["kw"], parts["vs"], parts["vw"], wg, kr,
          parts["qd"], parts["kvd"], parts["ga"], parts["gb"])], axis=1)
    assert w_all.shape[1] == IN_COLS
    return w_all


def _mla_q_weight(w_uq):
    r = w_uq.shape[0]
    w = w_uq.reshape(r, MLA_HEADS, MLA_NOPE + MLA_ROPE)
    w = jnp.pad(w, ((0, 0), (0, 0), (0, LANE - MLA_NOPE - MLA_ROPE)))
    return w.reshape(r, MLA_HEADS * LANE).astype(BF16)


def _mla_kv_weight(w_uk, w_uv):
    r = w_uk.shape[0]
    k = w_uk.reshape(r, MLA_HEADS, MLA_NOPE)
    k = jnp.pad(k, ((0, 0), (0, 0), (0, LANE - MLA_NOPE))).reshape(r, MLA_HEADS * LANE)
    return jnp.concatenate([k, w_uv], axis=1).astype(BF16)


def _rope_tables(seq):
    half = MLA_ROPE // 2
    pos = jnp.arange(seq, dtype=F32)
    inv_freq = ROPE_THETA ** (-jnp.arange(half, dtype=F32) / half)
    ang = pos[:, None] * inv_freq[None, :]
    cos, sin = jnp.cos(ang), jnp.sin(ang)
    ones = jnp.ones((seq, MLA_NOPE), F32)
    z_lo = jnp.zeros((seq, MLA_NOPE), F32)
    z_hi = jnp.zeros((seq, LANE - MLA_NOPE - MLA_ROPE), F32)
    return (jnp.concatenate([ones, cos, cos, z_hi], axis=1),
            jnp.concatenate([z_lo, sin, sin, z_hi], axis=1))


def _compress_weights(pe, w1, w2):
    hid = w1.shape[1]
    wa = w1[:CMP_STRIDE * NSA_DH].reshape(CMP_STRIDE, NSA_DH, hid)
    wb = w1[CMP_STRIDE * NSA_DH:].reshape(CMP_STRIDE, NSA_DH, hid)
    z = jnp.zeros_like(wa)
    cols = []
    for g in range(NSA_GROUPS):
        for w in (wa, wb):
            slots = [z, z]
            slots[g] = w
            cols.append(jnp.stack(slots, axis=1).reshape(CMP_STRIDE * NSA_GROUPS * NSA_DH, hid))
    w1big = jnp.concatenate(cols, axis=1).astype(BF16)
    pe8 = jnp.zeros((8, CMP_LEN * NSA_DH), F32).at[0].set(pe.reshape(-1))
    w2d = jnp.concatenate([w2, w2], axis=1).astype(BF16)
    return w1big, pe8, w1, w2d


def _overlap_table(seq):
    ns = seq // SEL_BLOCK
    nsp = -(-ns // LANE) * LANE
    ncp = seq // CMP_STRIDE
    s_id = jnp.arange(nsp)[:, None]
    c_id = jnp.arange(ncp)[None, :]
    c_start = c_id * CMP_STRIDE
    return ((c_start < s_id * SEL_BLOCK + SEL_BLOCK) & (c_start + CMP_LEN - 1 >= s_id * SEL_BLOCK)
            & (c_id < ncp - 1)).astype(BF16)


def _tiles(seq):
    tm_in = min(512, seq)
    tm_out = min(512, seq)
    tq_cw = min(1024, seq)
    tq_sel = min(256, seq)
    tk_sel = min(256, seq)
    tq_mla = min(512, seq)
    mla_heads_per_step = 8
    ff_chunk = 1024
    return tm_in, tm_out, tq_cw, tq_sel, tk_sel, tq_mla, mla_heads_per_step, ff_chunk


def kernel(x, c, w_ada, b_ada, g_mix, w_in, pe_ck, w_ck1, w_ck2, pe_cv, w_cv1, w_cv2, g_cq, w_uq, g_ckv,
           w_uk, w_uv, w_o_nsa, w_o_mla, w_out, g_mlp, w_fc1, w_fc2, g_final):
    B, S, D = x.shape
    depth = w_ada.shape[0]
    tm_in, tm_out, tq_cw, tq_nsa, tk_sel, tq_mla, mla_heads_per_step, ff_chunk = _tiles(S)
    cs, sn = _rope_tables(S)
    ovt = _overlap_table(S)
    gmat = (jnp.arange(ovt.shape[0])[:, None] // (tk_sel // SEL_BLOCK) == jnp.arange(LANE)[None, :]).astype(BF16)
    n_chunks = S // CMP_STRIDE
    for layer in range(depth):
        mod = _ada(c, w_ada[layer], b_ada[layer])
        (q, kc_in, vc_in, ksd, vst, kwd, vwt, gt, qm, km, vmt, ga, gb) = _inproj(
            x, mod, g_mix[layer][None], _inproj_weight(w_in[layer]), g_cq[layer][None],
            _mla_q_weight(w_uq[layer]), g_ckv[layer][None], _mla_kv_weight(w_uk[layer], w_uv[layer]),
            cs, sn, tm_in)
        w1k, pek, w1kf, w2k = _compress_weights(pe_ck[layer], w_ck1[layer], w_ck2[layer])
        w1v, pev, w1vf, w2v = _compress_weights(pe_cv[layer], w_cv1[layer], w_cv2[layer])
        width = CMP_STRIDE * NSA_GROUPS * NSA_DH
        kcd, vct = _compress(kc_in.reshape(B, n_chunks, width), vc_in.reshape(B, n_chunks, width),
                             w1k, w1v, pek, pev, w1kf, w1vf, w2k, w2v)
        ocw, sel, cnt = _nsa_cw(q, kcd, vct, kwd, vwt, gt, ovt, gmat, tq_cw, tq_nsa)
        flags = (cnt[:, :, :, 0, :S // tk_sel] > 0.5).astype(jnp.int32).reshape(-1)
        o_nsa = _nsa_sel(flags, q, ksd, vst, sel, gt, ocw, tq_nsa, tk_sel)
        o_mla = _mla(qm, km, vmt, tq_mla, mla_heads_per_step)
        x = _merge_mlp(o_nsa, o_mla, ga, gb, x, mod, w_o_nsa[layer].astype(BF16), w_o_mla[layer].astype(BF16),
                       w_out[layer].astype(BF16), g_mlp[layer][None], w_fc1[layer].astype(BF16),
                       w_fc2[layer].astype(BF16), g_final[None], tm_out, ff_chunk, layer == depth - 1)
    return x
```

```python
import functools
import math

import jax
import jax.numpy as jnp
from jax import lax
from jax.experimental import pallas as pl
from jax.experimental.pallas import tpu as pltpu

F32 = jnp.float32
BF16 = jnp.bfloat16

D_MODEL = 1024
NSA_GROUPS = 2
NSA_HG = 4
NSA_DH = 64
CMP_LEN = 32
CMP_STRIDE = 16
CMP_HIDDEN = 256
SEL_BLOCK = 64
SEL_TOPK = 16
WINDOW = 512
MLA_HEADS = 8
MLA_NOPE = 64
MLA_ROPE = 32
MLA_V = 64
MLA_Q_RANK = 384
MLA_KV_RANK = 256
ROPE_THETA = 10000.0
D_FF = 4 * D_MODEL
EPS = 1e-6
FORCE_BONUS = 1000.0

LANE = 128
HALF = LANE // 2
NEG = -1e30
LOG2E = math.log2(math.e)
VMEM_LIMIT = 52 * 1024 * 1024
GATE_ROWS = 16
ONES_ROWS = 16
MXU_COLS = 256
OFF_Q = 0
OFF_KC_VC = 512
OFF_KS_KW = 768
OFF_VS_VW = 1024
OFF_G_KR = 1280
OFF_QD = 1536
OFF_KVD = 1920
OFF_GA = 2176
OFF_GB = 3200
IN_COLS = 4224


def _dot(a, b):
    return jnp.dot(a, b, preferred_element_type=F32)


def _dot_nt(a, b):
    return lax.dot_general(a, b, (((1,), (1,)), ((), ())), preferred_element_type=F32)


def _rms(x, g):
    return x * lax.rsqrt(jnp.mean(x * x, axis=-1, keepdims=True) + EPS) * g


def _sigmoid(x):
    return 1.0 / (1.0 + jnp.exp(-x))


def _params(sem):
    return pltpu.CompilerParams(dimension_semantics=sem, vmem_limit_bytes=VMEM_LIMIT)


def _ada_body(c_ref, w_ref, b_ref, o_ref):
    o_ref[...] = jnp.dot(c_ref[...], w_ref[...], preferred_element_type=F32,
                         precision=lax.Precision.HIGHEST) + b_ref[...]


def _ada(c, w, b):
    nb, d = c.shape
    n = w.shape[1]
    rows = 8
    cp = jnp.zeros((rows, d), F32).at[:nb].set(c)
    tn = 1536
    out = pl.pallas_call(
        _ada_body,
        grid=(n // tn,),
        in_specs=[pl.BlockSpec((rows, d), lambda j: (0, 0)),
                  pl.BlockSpec((d, tn), lambda j: (0, j)),
                  pl.BlockSpec((1, tn), lambda j: (0, j))],
        out_specs=pl.BlockSpec((rows, tn), lambda j: (0, j)),
        out_shape=jax.ShapeDtypeStruct((rows, n), F32),
        compiler_params=_params(("arbitrary",)),
        name="ada",
    )(cp, w, b.reshape(1, n))
    return out[:nb].reshape(nb, 6, d)


def _inproj_body(x_ref, mod_ref, gmix_ref, w_ref, gcq_ref, wq2_ref, gckv_ref, wkv_ref,
                 cs_ref, sn_ref,
                 q_ref, kc_ref, vc_ref, ks_ref, vst_ref, kw_ref, vwt_ref, gt_ref,
                 qm_ref, km_ref, vmt_ref, ga_ref, gb_ref):
    x = x_ref[0]
    h = _rms(x, gmix_ref[...]) * (1.0 + mod_ref[0, 1:2, :]) + mod_ref[0, 0:1, :]
    hb = h.astype(BF16)

    def proj(lo, n):
        return _dot(hb, w_ref[:, lo:lo + n])

    lane = lax.broadcasted_iota(jnp.int32, (x.shape[0], LANE), 1)
    half = MLA_ROPE // 2

    def dup_groups(z):
        swapped = pltpu.roll(z, HALF, 1)
        lower = lane < HALF
        return jnp.concatenate([jnp.where(lower, z, swapped), jnp.where(lower, swapped, z)], axis=1)

    def rotary(a):
        partner = jnp.where(lane < MLA_NOPE + half, -pltpu.roll(a, LANE - half, 1), pltpu.roll(a, half, 1))
        return a * cs_ref[...] + partner * sn_ref[...]

    q_ref[0] = (proj(OFF_Q, 512) * (NSA_DH ** -0.5 * LOG2E)).astype(BF16)
    z = proj(OFF_KC_VC, MXU_COLS)
    kc_ref[0] = z[:, 0:LANE].astype(BF16)
    vc_ref[0] = z[:, LANE:].astype(BF16)
    z = proj(OFF_KS_KW, MXU_COLS)
    ks_ref[0] = dup_groups(z[:, 0:LANE]).astype(BF16)
    kw_ref[0] = dup_groups(z[:, LANE:]).astype(BF16)
    z = proj(OFF_VS_VW, MXU_COLS)
    vst_ref[0] = z[:, 0:LANE].T.astype(BF16)
    vwt_ref[0] = z[:, LANE:].T.astype(BF16)
    z = proj(OFF_G_KR, MXU_COLS)
    gt_ref[0] = _sigmoid(z[:, 0:LANE]).T[0:NSA_GROUPS * GATE_ROWS]
    kr = rotary(z[:, LANE:])
    ga_ref[0] = _sigmoid(proj(OFF_GA, 1024)).astype(BF16)
    gb_ref[0] = _sigmoid(proj(OFF_GB, 1024)).astype(BF16)

    nh = MLA_HEADS * LANE
    cq = _rms(proj(OFF_QD, MLA_Q_RANK), gcq_ref[...]).astype(BF16)
    q2 = _dot(cq, wq2_ref[...])
    scale = (MLA_NOPE + MLA_ROPE) ** -0.5 * LOG2E
    for hh in range(MLA_HEADS):
        qm_ref[0, :, LANE * hh:LANE * (hh + 1)] = (rotary(q2[:, LANE * hh:LANE * (hh + 1)]) * scale).astype(BF16)
    ckv = _rms(proj(OFF_KVD, MLA_KV_RANK), gckv_ref[...]).astype(BF16)
    kv = _dot(ckv, wkv_ref[...])
    for hh in range(MLA_HEADS):
        km_ref[0, :, LANE * hh:LANE * (hh + 1)] = (kv[:, LANE * hh:LANE * (hh + 1)] + kr).astype(BF16)
    vmt_ref[0] = kv[:, nh:].T.astype(BF16)


def _inproj(x, mod, g_mix, w_all, g_cq, wq2, g_ckv, wkv, cs, sn, tm):
    B, S, D = x.shape
    row = lambda n: pl.BlockSpec((1, tm, n), lambda b, i: (b, i, 0))
    col = lambda n: pl.BlockSpec((1, n, tm), lambda b, i: (b, 0, i))
    full = lambda a: pl.BlockSpec(a.shape, lambda b, i: (0,) * a.ndim, pipeline_mode=pl.Buffered(1))
    tok = lambda n, dt: (row(n), jax.ShapeDtypeStruct((B, S, n), dt))
    tr = lambda n, dt: (col(n), jax.ShapeDtypeStruct((B, n, S), dt))
    outs = [tok(512, BF16), tok(128, BF16), tok(128, BF16), tok(256, BF16), tr(LANE, BF16),
            tok(256, BF16), tr(LANE, BF16), tr(NSA_GROUPS * GATE_ROWS, F32),
            tok(1024, BF16), tok(1024, BF16), tr(MLA_HEADS * MLA_V, BF16), tok(1024, BF16), tok(1024, BF16)]
    return pl.pallas_call(
        _inproj_body,
        grid=(B, S // tm),
        in_specs=[row(D),
                  pl.BlockSpec((1, 6, D), lambda b, i: (b, 0, 0)),
                  full(g_mix), full(w_all), full(g_cq), full(wq2), full(g_ckv), full(wkv),
                  pl.BlockSpec((tm, LANE), lambda b, i: (i, 0)),
                  pl.BlockSpec((tm, LANE), lambda b, i: (i, 0))],
        out_specs=[o[0] for o in outs],
        out_shape=[o[1] for o in outs],
        compiler_params=_params(("parallel", "parallel")),
        name="inproj",
    )(x, mod, g_mix, w_all, g_cq, wq2, g_ckv, wkv, cs, sn)


def _compress_one(x, w1big, pe, w1f, w2d):
    hcat = _dot(x, w1big)
    const = jnp.dot(pe, w1f, preferred_element_type=F32, precision=lax.Precision.HIGHEST)[0:1]
    n = x.shape[0]
    rowid = lax.broadcasted_iota(jnp.int32, (n, LANE), 0)
    hid_w = CMP_HIDDEN
    outs = []
    for g in range(NSA_GROUPS):
        a = hcat[:, 2 * hid_w * g:2 * hid_w * g + hid_w]
        b = hcat[:, 2 * hid_w * g + hid_w:2 * hid_w * (g + 1)]
        hid = a + pltpu.roll(b, n - 1, 0) + const
        act = hid * _sigmoid(hid)
        out = _dot(act.astype(BF16), w2d)
        outs.append(jnp.where(rowid < n - 1, out, 0.0))
    return outs


def _compress_body(xk_ref, xv_ref, w1k_ref, w1v_ref, pek_ref, pev_ref, w1kf_ref, w1vf_ref,
                   w2k_ref, w2v_ref, kc_ref, vct_ref):
    kc = _compress_one(xk_ref[0], w1k_ref[...], pek_ref[...], w1kf_ref[...], w2k_ref[...])
    vc = _compress_one(xv_ref[0], w1v_ref[...], pev_ref[...], w1vf_ref[...], w2v_ref[...])
    for g in range(NSA_GROUPS):
        kc_ref[0, g] = kc[g].astype(BF16)
        vct_ref[0, g] = vc[g].T[0:NSA_DH].astype(BF16)


def _compress(xk, xv, w1k, w1v, pek, pev, w1kf, w1vf, w2k, w2v):
    B, n, width = xk.shape
    full = lambda a: pl.BlockSpec(a.shape, lambda b: (0,) * a.ndim)
    xs = pl.BlockSpec((1, n, width), lambda b: (b, 0, 0))
    return pl.pallas_call(
        _compress_body,
        grid=(B,),
        in_specs=[xs, xs, full(w1k), full(w1v), full(pek), full(pev), full(w1kf), full(w1vf),
                  full(w2k), full(w2v)],
        out_specs=[pl.BlockSpec((1, NSA_GROUPS, n, LANE), lambda b: (b, 0, 0, 0)),
                   pl.BlockSpec((1, NSA_GROUPS, NSA_DH, n), lambda b: (b, 0, 0, 0))],
        out_shape=[jax.ShapeDtypeStruct((B, NSA_GROUPS, n, LANE), BF16),
                   jax.ShapeDtypeStruct((B, NSA_GROUPS, NSA_DH, n), BF16)],
        compiler_params=_params(("parallel",)),
        name="compress",
    )(xk, xv, w1k, w1v, pek, pev, w1kf, w1vf, w2k, w2v)


def _stack_heads(q_ref, tq):
    upper = lax.broadcasted_iota(jnp.int32, (tq, LANE), 1) >= HALF
    parts = []
    for hh in range(NSA_HG):
        slab = q_ref[0, :, LANE * (hh // 2):LANE * (hh // 2 + 1)]
        keep = upper if hh % 2 else jnp.logical_not(upper)
        parts.append(jnp.where(keep, slab, jnp.zeros_like(slab)))
    return jnp.concatenate(parts, axis=0)


def _slope(g, hh):
    return jnp.where(g == 0, 1.0, 2.0 ** -NSA_HG).astype(F32) * (2.0 ** -(hh + 1) * LOG2E)


def _with_ones(vt):
    return jnp.concatenate([vt, jnp.ones((ONES_ROWS, vt.shape[1]), vt.dtype)], axis=0)


def _nsa_cw_body(q_ref, kc_ref, vct_ref, kw_ref, vwt_ref, gt_ref, ovt_ref, gm_ref,
                 ocw_ref, sel_ref, cnt_ref, bc_sc, bw_sc, qst_sc, *, tq, tw, tf, seq, n_parts):
    g = pl.program_id(1)
    i = pl.program_id(2)
    t0 = i * tq
    qst_sc[...] = _stack_heads(q_ref, tq)
    gates = gt_ref[0]
    ncp = kc_ref.shape[2]
    nsp = ovt_ref.shape[0]
    wk = WINDOW + tw

    @pl.when(i == 0)
    def _():
        c_end_f = (lax.broadcasted_iota(jnp.int32, (ncp, tq), 0) * CMP_STRIDE + (CMP_LEN - 1)).astype(F32)
        key_f = lax.broadcasted_iota(jnp.int32, (wk, tw), 0).astype(F32)
        for hh in range(NSA_HG):
            bc_sc[hh] = _slope(g, hh) * c_end_f
            bw_sc[hh] = _slope(g, hh) * key_f

    def tile_body(ncp_e, nsp_e):
        vct = _with_ones(vct_ref[0, 0, :, 0:ncp_e])
        st = _dot_nt(kc_ref[0, 0, 0:ncp_e, :], qst_sc[...])
        t_c = t0 + lax.broadcasted_iota(jnp.int32, (ncp_e, tq), 1)
        c_end = lax.broadcasted_iota(jnp.int32, (ncp_e, tq), 0) * CMP_STRIDE + (CMP_LEN - 1)
        madd_c = jnp.where(t_c >= c_end, 0.0, NEG)
        psum = jnp.zeros((ncp_e, tq), F32)
        o_cmp = []
        for hh in range(NSA_HG):
            z = st[:, hh * tq:(hh + 1) * tq] + bc_sc[hh, 0:ncp_e] + madd_c
            m = jnp.max(z, axis=0, keepdims=True)
            e = jnp.exp2(z - m)
            o_aug = _dot(vct, e.astype(BF16))
            r = jnp.where(m > 0.5 * NEG, 1.0 / jnp.maximum(o_aug[NSA_DH:NSA_DH + 1], 1e-30), 0.0)
            psum = psum + e * r
            o_cmp.append(o_aug[0:NSA_DH] * r)

        ovt = ovt_ref[0:nsp_e, 0:ncp_e]
        p1 = psum.astype(BF16)
        r1 = psum - p1.astype(F32)
        p2 = r1.astype(BF16)
        p3 = (r1 - p2.astype(F32)).astype(BF16)
        p_slc = _dot(ovt, p1) + _dot(ovt, p2) + _dot(ovt, p3)

        n_forced = 3
        sels = []
        for c in range(tq // LANE):
            blk = lax.broadcasted_iota(jnp.int32, (nsp_e, LANE), 0)
            cur = (t0 + c * LANE + lax.broadcasted_iota(jnp.int32, (nsp_e, LANE), 1)) // SEL_BLOCK
            valid = blk <= cur
            forced = (blk == 0) | (blk >= cur - 1)
            score = jnp.where(valid & jnp.logical_not(forced), p_slc[:, c * LANE:(c + 1) * LANE], -1.0)
            for _ in range(max(min(SEL_TOPK, seq // SEL_BLOCK) - n_forced, 0)):
                m = jnp.max(score, axis=0, keepdims=True)
                first = jnp.min(jnp.where(score == m, blk, nsp), axis=0, keepdims=True)
                score = jnp.where(blk == first, -1.0, score)
            sels.append(jnp.where(valid & (score < 0.0), 1.0, 0.0))
        sel = jnp.concatenate(sels, axis=1)
        sel_ref[0, 0, 0:nsp_e] = sel
        if nsp_e < nsp:
            sel_ref[0, 0, nsp_e:nsp] = jnp.zeros((nsp - nsp_e, tq), F32)
        for f in range(tq // tf):
            per_blk = _dot_nt(jnp.ones((8, tf), BF16), sel[:, f * tf:(f + 1) * tf].astype(BF16))
            cnt_ref[0, 0, f] = _dot(per_blk.astype(BF16), gm_ref[0:nsp_e])

        for sub in range(tq // tw):
            k_start = pl.multiple_of(jnp.clip(t0 + sub * tw - WINDOW, 0, seq - wk), LANE)
            kw = kw_ref[0, pl.ds(k_start, wk), :]
            vwt = _with_ones(vwt_ref[0, :, pl.ds(k_start, wk)])
            qsub = jnp.concatenate(
                [qst_sc[hh * tq + sub * tw:hh * tq + (sub + 1) * tw] for hh in range(NSA_HG)], axis=0)
            st = _dot_nt(kw, qsub)
            t_w = t0 + sub * tw + lax.broadcasted_iota(jnp.int32, (wk, tw), 1)
            dist_i = t_w - (k_start + lax.broadcasted_iota(jnp.int32, (wk, tw), 0))
            madd_w = jnp.where((dist_i >= 0) & (dist_i < WINDOW), 0.0, NEG)
            cols = slice(sub * tw, (sub + 1) * tw)
            for hh in range(NSA_HG):
                z = st[:, hh * tw:(hh + 1) * tw] + bw_sc[hh] + madd_w
                e = jnp.exp2(z - jnp.max(z, axis=0, keepdims=True))
                o_aug = _dot(vwt, e.astype(BF16))
                o_win = o_aug[0:NSA_DH] * (1.0 / jnp.maximum(o_aug[NSA_DH:NSA_DH + 1], 1e-30))
                ocw_ref[0, 0, NSA_DH * hh:NSA_DH * (hh + 1), cols] = (
                    gates[hh:hh + 1, cols] * o_cmp[hh][:, cols]
                    + gates[2 * NSA_HG + hh:2 * NSA_HG + hh + 1, cols] * o_win)

    part = (i * n_parts) // pl.num_programs(2)
    for v in range(n_parts):
        pl.when(part == v)(functools.partial(tile_body, ncp * (v + 1) // n_parts, nsp * (v + 1) // n_parts))


def _cw_parts(nq, ncp, nsp):
    for n in (4, 2):
        if nq % n == 0 and (ncp // n) % LANE == 0 and (nsp // n) % 8 == 0:
            return n
    return 1


def _nsa_cw(q, kcd, vct, kwd, vwt, gt, ovt, gmat, tq, tf):
    B, S, _ = q.shape
    ncp = kcd.shape[2]
    nsp = ovt.shape[0]
    nq = S // tq
    tw = min(MXU_COLS, tq)
    return pl.pallas_call(
        functools.partial(_nsa_cw_body, tq=tq, tw=tw, tf=tf, seq=S, n_parts=_cw_parts(nq, ncp, nsp)),
        grid=(B, NSA_GROUPS, nq),
        in_specs=[pl.BlockSpec((1, tq, 2 * LANE), lambda b, g, i: (b, i, g)),
                  pl.BlockSpec((1, 1, ncp, LANE), lambda b, g, i: (b, g, 0, 0)),
                  pl.BlockSpec((1, 1, NSA_DH, ncp), lambda b, g, i: (b, g, 0, 0)),
                  pl.BlockSpec((1, S, LANE), lambda b, g, i: (b, 0, g)),
                  pl.BlockSpec((1, NSA_DH, S), lambda b, g, i: (b, g, 0)),
                  pl.BlockSpec((1, GATE_ROWS, tq), lambda b, g, i: (b, g, i)),
                  pl.BlockSpec(ovt.shape, lambda b, g, i: (0, 0)),
                  pl.BlockSpec(gmat.shape, lambda b, g, i: (0, 0))],
        out_specs=[pl.BlockSpec((1, 1, NSA_HG * NSA_DH, tq), lambda b, g, i: (b, g, 0, i)),
                   pl.BlockSpec((1, 1, nsp, tq), lambda b, g, i: (b, g, 0, i)),
                   pl.BlockSpec((1, 1, tq // tf, 8, LANE), lambda b, g, i: (b, g, i, 0, 0))],
        out_shape=[jax.ShapeDtypeStruct((B, NSA_GROUPS, NSA_HG * NSA_DH, S), F32),
                   jax.ShapeDtypeStruct((B, NSA_GROUPS, nsp, S), F32),
                   jax.ShapeDtypeStruct((B, NSA_GROUPS, S // tf, 8, LANE), F32)],
        scratch_shapes=[pltpu.VMEM((NSA_HG, ncp, tq), F32), pltpu.VMEM((NSA_HG, WINDOW + tw, tw), F32),
                        pltpu.VMEM((NSA_HG * tq, LANE), BF16)],
        compiler_params=_params(("arbitrary", "arbitrary", "arbitrary")),
        name="nsa_cw",
    )(q, kcd, vct, kwd, vwt, gt, ovt, gmat)


def _nsa_sel_body(flag_ref, q_ref, ks_ref, vst_ref, sel_ref, gt_ref, ocw_ref, o_ref,
                  m_sc, acc_sc, skb_sc, qst_sc, sa_sc, sb_sc, lst_sc, *, tq, tk, n_tiles):
    g = pl.program_id(1)
    i = pl.program_id(2)
    t0 = i * tq
    j_diag = t0 // tk

    @pl.when(i == 0)
    def _():
        key_f = lax.broadcasted_iota(jnp.int32, (tk, tq), 0).astype(F32)
        for hh in range(NSA_HG):
            skb_sc[hh] = _slope(g, hh) * key_f

    qst_sc[...] = _stack_heads(q_ref, tq)
    m_sc[...] = jnp.full(m_sc.shape, NEG, F32)
    acc_sc[...] = jnp.zeros(acc_sc.shape, F32)

    base = ((pl.program_id(0) * NSA_GROUPS + g) * pl.num_programs(2) + i) * n_tiles
    n_act = jnp.int32(0)
    for j in range(n_tiles):
        lst_sc[n_act] = j
        n_act = n_act + jnp.where((flag_ref[base + j] > 0) & (j < j_diag), 1, 0)
    lst_sc[n_act] = j_diag

    def scores(j, s_ref):
        k0 = pl.multiple_of(j * tk, tk)
        s_ref[...] = _dot_nt(ks_ref[0, pl.ds(k0, tk), :], qst_sc[...])

    def consume(j, s_ref, diagonal):
        k0 = pl.multiple_of(j * tk, tk)
        vt = _with_ones(vst_ref[0, :, pl.ds(k0, tk)])
        b0 = j * (tk // SEL_BLOCK)
        pieces = []
        for r in range(tk // SEL_BLOCK):
            chosen = sel_ref[0, 0, pl.ds(b0 + r, 1), :]
            pieces.append(jnp.broadcast_to(jnp.where(chosen > 0.5, 0.0, NEG), (SEL_BLOCK, tq)))
        madd = jnp.concatenate(pieces, axis=0)
        if diagonal:
            key_i = lax.broadcasted_iota(jnp.int32, (tk, tq), 0)
            qry_i = lax.broadcasted_iota(jnp.int32, (tk, tq), 1)
            madd = jnp.where(key_i + k0 <= qry_i + t0, madd, NEG)
        for hh in range(NSA_HG):
            c = _slope(g, hh) * (k0 - t0).astype(F32)
            z = s_ref[:, hh * tq:(hh + 1) * tq] + skb_sc[hh] + madd
            m_prev = m_sc[hh]
            m_new = jnp.maximum(m_prev, jnp.max(z, axis=0, keepdims=True) + c)
            alpha = jnp.exp2(m_prev - m_new)
            p = jnp.exp2(z - (m_new - c))
            acc_sc[hh] = alpha * acc_sc[hh] + _dot(vt, p.astype(BF16))
            m_sc[hh] = m_new

    scores(lst_sc[0], sa_sc)

    def pair(jj, carry):
        it = 2 * jj
        scores(lst_sc[it + 1], sb_sc)
        consume(lst_sc[it], sa_sc, False)
        scores(lst_sc[it + 2], sa_sc)
        consume(lst_sc[it + 1], sb_sc, False)
        return carry

    lax.fori_loop(0, n_act // 2, pair, 0)

    @pl.when(n_act % 2 == 0)
    def _():
        consume(j_diag, sa_sc, True)

    @pl.when(n_act % 2 == 1)
    def _():
        scores(j_diag, sb_sc)
        consume(lst_sc[n_act - 1], sa_sc, False)
        consume(j_diag, sb_sc, True)

    gates = gt_ref[0]
    for pair_i in range(NSA_HG // 2):
        halves = []
        for hh in (2 * pair_i, 2 * pair_i + 1):
            acc = acc_sc[hh]
            w = gates[NSA_HG + hh:NSA_HG + hh + 1] / jnp.maximum(acc[NSA_DH:NSA_DH + 1], 1e-30)
            halves.append(acc[0:NSA_DH] * w + ocw_ref[0, 0, NSA_DH * hh:NSA_DH * (hh + 1), :])
        o_ref[0, :, LANE * pair_i:LANE * (pair_i + 1)] = jnp.concatenate(halves, axis=0).T.astype(BF16)


def _nsa_sel(flags, q, ksd, vst, sel, gt, ocw, tq, tk):
    B, S, _ = q.shape
    nsp = sel.shape[2]
    n_tiles = S // tk
    assert tk % tq == 0, "a query tile must lie inside one (the diagonal) key tile"
    qs = pl.BlockSpec((1, tq, 2 * LANE), lambda b, g, i, f: (b, i, g))
    return pl.pallas_call(
        functools.partial(_nsa_sel_body, tq=tq, tk=tk, n_tiles=n_tiles),
        grid_spec=pltpu.PrefetchScalarGridSpec(
            num_scalar_prefetch=1,
            grid=(B, NSA_GROUPS, S // tq),
            in_specs=[qs,
                      pl.BlockSpec((1, S, LANE), lambda b, g, i, f: (b, 0, g)),
                      pl.BlockSpec((1, NSA_DH, S), lambda b, g, i, f: (b, g, 0)),
                      pl.BlockSpec((1, 1, nsp, tq), lambda b, g, i, f: (b, g, 0, i)),
                      pl.BlockSpec((1, GATE_ROWS, tq), lambda b, g, i, f: (b, g, i)),
                      pl.BlockSpec((1, 1, NSA_HG * NSA_DH, tq), lambda b, g, i, f: (b, g, 0, i))],
            out_specs=qs,
            scratch_shapes=[pltpu.VMEM((NSA_HG, 1, tq), F32),
                            pltpu.VMEM((NSA_HG, NSA_DH + ONES_ROWS, tq), F32),
                            pltpu.VMEM((NSA_HG, tk, tq), F32),
                            pltpu.VMEM((NSA_HG * tq, LANE), BF16),
                            pltpu.VMEM((tk, NSA_HG * tq), F32),
                            pltpu.VMEM((tk, NSA_HG * tq), F32),
                            pltpu.SMEM((n_tiles + 1,), jnp.int32)]),
        out_shape=jax.ShapeDtypeStruct((B, S, 4 * LANE), BF16),
        compiler_params=_params(("arbitrary", "arbitrary", "arbitrary")),
        name="nsa_sel",
    )(flags, q, ksd, vst, sel, gt, ocw)


def _mla_body(q_ref, k_ref, vt_ref, o_ref, m_sc, acc_sc, sa_sc, sb_sc, *, tq, tc, nh):
    n_full = pl.program_id(2)
    m_sc[...] = jnp.full(m_sc.shape, NEG, F32)
    acc_sc[...] = jnp.zeros(acc_sc.shape, F32)

    def scores(j, s_ref):
        k0 = pl.multiple_of(j * tq, tq)
        for hh in range(nh):
            lanes = slice(LANE * hh, LANE * (hh + 1))
            s_ref[hh] = _dot_nt(k_ref[0, pl.ds(k0, tq), lanes], q_ref[0, :, lanes])

    def consume(j, s_ref, diagonal):
        k0 = pl.multiple_of(j * tq, tq)
        for hh in range(nh):
            vt = _with_ones(vt_ref[0, MLA_V * hh:MLA_V * (hh + 1), pl.ds(k0, tq)])
            for c in range(tq // tc):
                cols = slice(c * tc, (c + 1) * tc)
                st = s_ref[hh, :, cols]
                if diagonal:
                    causal = (lax.broadcasted_iota(jnp.int32, (tq, tc), 0)
                              <= lax.broadcasted_iota(jnp.int32, (tq, tc), 1) + c * tc)
                    st = jnp.where(causal, st, NEG)
                m_prev = m_sc[hh, :, cols]
                m_new = jnp.maximum(m_prev, jnp.max(st, axis=0, keepdims=True))
                alpha = jnp.exp2(m_prev - m_new)
                p = jnp.exp2(st - m_new)
                acc_sc[hh, :, cols] = alpha * acc_sc[hh, :, cols] + _dot(vt, p.astype(BF16))
                m_sc[hh, :, cols] = m_new

    scores(0, sa_sc)

    def pair(jj, carry):
        j = 2 * jj
        scores(j + 1, sb_sc)
        consume(j, sa_sc, False)
        scores(j + 2, sa_sc)
        consume(j + 1, sb_sc, False)
        return carry

    lax.fori_loop(0, n_full // 2, pair, 0)

    @pl.when(n_full % 2 == 0)
    def _():
        consume(n_full, sa_sc, True)

    @pl.when(n_full % 2 == 1)
    def _():
        scores(n_full, sb_sc)
        consume(n_full - 1, sa_sc, False)
        consume(n_full, sb_sc, True)

    for pair_i in range(nh // 2):
        halves = []
        for hh in (2 * pair_i, 2 * pair_i + 1):
            acc = acc_sc[hh]
            halves.append(acc[0:MLA_V] * (1.0 / jnp.maximum(acc[MLA_V:MLA_V + 1], 1e-30)))
        o_ref[0, :, LANE * pair_i:LANE * (pair_i + 1)] = jnp.concatenate(halves, axis=0).T.astype(BF16)


def _mla(qm, km, vmt, tq, nh):
    B, S, _ = qm.shape
    return pl.pallas_call(
        functools.partial(_mla_body, tq=tq, tc=min(256, tq), nh=nh),
        grid=(B, MLA_HEADS // nh, S // tq),
        in_specs=[pl.BlockSpec((1, tq, nh * LANE), lambda b, p, i: (b, i, p)),
                  pl.BlockSpec((1, S, nh * LANE), lambda b, p, i: (b, 0, p), pipeline_mode=pl.Buffered(1)),
                  pl.BlockSpec((1, nh * MLA_V, S), lambda b, p, i: (b, p, 0), pipeline_mode=pl.Buffered(1))],
        out_specs=pl.BlockSpec((1, tq, nh * MLA_V), lambda b, p, i: (b, i, p)),
        out_shape=jax.ShapeDtypeStruct((B, S, MLA_HEADS * MLA_V), BF16),
        scratch_shapes=[pltpu.VMEM((nh, 1, tq), F32), pltpu.VMEM((nh, MLA_V + ONES_ROWS, tq), F32),
                        pltpu.VMEM((nh, tq, tq), F32), pltpu.VMEM((nh, tq, tq), F32)],
        compiler_params=_params(("parallel", "parallel", "arbitrary")),
        name="mla",
    )(qm, km, vmt)


def _merge_mlp_body(on_ref, om_ref, ga_ref, gb_ref, x_ref, mod_ref, won_ref, wom_ref, wout_ref,
                    gmlp_ref, w1_ref, w2_ref, gfin_ref, o_ref, *, fc, final):
    y = (ga_ref[0].astype(F32) * _dot(on_ref[0], won_ref[...])
         + gb_ref[0].astype(F32) * _dot(om_ref[0], wom_ref[...]))
    x1 = x_ref[0] + mod_ref[0, 2:3, :] * _dot(y.astype(BF16), wout_ref[...])
    h = _rms(x1, gmlp_ref[...]) * (1.0 + mod_ref[0, 4:5, :]) + mod_ref[0, 3:4, :]
    hb = h.astype(BF16)
    acc = jnp.zeros(x1.shape, F32)
    for c in range(w1_ref.shape[1] // fc):
        a = jnp.maximum(_dot(hb, w1_ref[:, c * fc:(c + 1) * fc]), 0.0)
        acc = acc + _dot((a * a).astype(BF16), w2_ref[c * fc:(c + 1) * fc, :])
    x2 = x1 + mod_ref[0, 5:6, :] * acc
    o_ref[0] = _rms(x2, gfin_ref[...]) if final else x2


def _merge_mlp(o_nsa, o_mla, ga, gb, x, mod, w_on, w_om, w_out, g_mlp, w1, w2, g_fin, tm, fc, final):
    B, S, D = x.shape
    row = lambda n: pl.BlockSpec((1, tm, n), lambda b, i: (b, i, 0))
    full = lambda a: pl.BlockSpec(a.shape, lambda b, i: (0,) * a.ndim, pipeline_mode=pl.Buffered(1))
    return pl.pallas_call(
        functools.partial(_merge_mlp_body, fc=fc, final=final),
        grid=(B, S // tm),
        in_specs=[row(o_nsa.shape[2]), row(o_mla.shape[2]), row(D), row(D), row(D),
                  pl.BlockSpec((1, 6, D), lambda b, i: (b, 0, 0)), full(w_on), full(w_om), full(w_out),
                  full(g_mlp), full(w1), full(w2), full(g_fin)],
        out_specs=row(D),
        out_shape=jax.ShapeDtypeStruct((B, S, D), F32),
        compiler_params=_params(("parallel", "parallel")),
        name="merge_mlp",
    )(o_nsa, o_mla, ga, gb, x, mod, w_on, w_om, w_out, g_mlp, w1, w2, g_fin)


def _inproj_weight(w_in):
    d = w_in.shape[0]
    o = 0
    parts = {}
    for name, n in (("q", 512), ("kc", 128), ("vc", 128), ("ks", 128), ("vs", 128), ("kw", 128),
                    ("vw", 128), ("g", 24), ("qd", MLA_Q_RANK), ("kvd", MLA_KV_RANK), ("kr", MLA_ROPE),
                    ("ga", D_MODEL), ("gb", D_MODEL)):
        parts[name] = w_in[:, o:o + n]
        o += n
    wg = parts["g"].reshape(d, 3, NSA_GROUPS, NSA_HG).transpose(0, 2, 1, 3).reshape(d, NSA_GROUPS, 3 * NSA_HG)
    wg = jnp.pad(wg, ((0, 0), (0, 0), (0, GATE_ROWS - 3 * NSA_HG))).reshape(d, NSA_GROUPS * GATE_ROWS)
    wg = jnp.pad(wg, ((0, 0), (0, LANE - NSA_GROUPS * GATE_ROWS)))
    kr = jnp.pad(parts["kr"], ((0, 0), (MLA_NOPE, LANE - MLA_NOPE - MLA_ROPE)))
    w_all = jnp.concatenate(
        [p.astype(BF16) for p in
         (parts["q"], parts["kc"], parts["vc"], parts["ks"], parts["kw"], parts["vs"], parts["vw"], wg, kr,
          parts["qd"], parts["kvd"], parts["ga"], parts["gb"])], axis=1)
    assert w_all.shape[1] == IN_COLS
    return w_all


def _mla_q_weight(w_uq):
    r = w_uq.shape[0]
    w = w_uq.reshape(r, MLA_HEADS, MLA_NOPE + MLA_ROPE)
    w = jnp.pad(w, ((0, 0), (0, 0), (0, LANE - MLA_NOPE - MLA_ROPE)))
    return w.reshape(r, MLA_HEADS * LANE).astype(BF16)


def _mla_kv_weight(w_uk, w_uv):
    r = w_uk.shape[0]
    k = w_uk.reshape(r, MLA_HEADS, MLA_NOPE)
    k = jnp.pad(k, ((0, 0), (0, 0), (0, LANE - MLA_NOPE))).reshape(r, MLA_HEADS * LANE)
    return jnp.concatenate([k, w_uv], axis=1).astype(BF16)


def _rope_tables(seq):
    half = MLA_ROPE // 2
    pos = jnp.arange(seq, dtype=F32)
    inv_freq = ROPE_THETA ** (-jnp.arange(half, dtype=F32) / half)
    ang = pos[:, None] * inv_freq[None, :]
    cos, sin = jnp.cos(ang), jnp.sin(ang)
    ones = jnp.ones((seq, MLA_NOPE), F32)
    z_lo = jnp.zeros((seq, MLA_NOPE), F32)
    z_hi = jnp.zeros((seq, LANE - MLA_NOPE - MLA_ROPE), F32)
    return (jnp.concatenate([ones, cos, cos, z_hi], axis=1),
            jnp.concatenate([z_lo, sin, sin, z_hi], axis=1))


def _compress_weights(pe, w1, w2):
    hid = w1.shape[1]
    wa = w1[:CMP_STRIDE * NSA_DH].reshape(CMP_STRIDE, NSA_DH, hid)
    wb = w1[CMP_STRIDE * NSA_DH:].reshape(CMP_STRIDE, NSA_DH, hid)
    z = jnp.zeros_like(wa)
    cols = []
    for g in range(NSA_GROUPS):
        for w in (wa, wb):
            slots = [z, z]
            slots[g] = w
            cols.append(jnp.stack(slots, axis=1).reshape(CMP_STRIDE * NSA_GROUPS * NSA_DH, hid))
    w1big = jnp.concatenate(cols, axis=1).astype(BF16)
    pe8 = jnp.zeros((8, CMP_LEN * NSA_DH), F32).at[0].set(pe.reshape(-1))
    w2d = jnp.concatenate([w2, w2], axis=1).astype(BF16)
    return w1big, pe8, w1, w2d


def _overlap_table(seq):
    ns = seq // SEL_BLOCK
    nsp = -(-ns // LANE) * LANE
    ncp = seq // CMP_STRIDE
    s_id = jnp.arange(nsp)[:, None]
    c_id = jnp.arange(ncp)[None, :]
    c_start = c_id * CMP_STRIDE
    return ((c_start < s_id * SEL_BLOCK + SEL_BLOCK) & (c_start + CMP_LEN - 1 >= s_id * SEL_BLOCK)
            & (c_id < ncp - 1)).astype(BF16)


def _tiles(seq):
    tm_in = min(1024, seq)
    tm_out = min(512, seq)
    tq_cw = min(1024, seq)
    tq_sel = min(256, seq)
    tk_sel = min(256, seq)
    tq_mla = min(512, seq)
    mla_heads_per_step = 8
    ff_chunk = 1024
    return tm_in, tm_out, tq_cw, tq_sel, tk_sel, tq_mla, mla_heads_per_step, ff_chunk


def kernel(x, c, w_ada, b_ada, g_mix, w_in, pe_ck, w_ck1, w_ck2, pe_cv, w_cv1, w_cv2, g_cq, w_uq, g_ckv,
           w_uk, w_uv, w_o_nsa, w_o_mla, w_out, g_mlp, w_fc1, w_fc2, g_final):
    B, S, D = x.shape
    depth = w_ada.shape[0]
    tm_in, tm_out, tq_cw, tq_nsa, tk_sel, tq_mla, mla_heads_per_step, ff_chunk = _tiles(S)
    cs, sn = _rope_tables(S)
    ovt = _overlap_table(S)
    gmat = (jnp.arange(ovt.shape[0])[:, None] // (tk_sel // SEL_BLOCK) == jnp.arange(LANE)[None, :]).astype(BF16)
    n_chunks = S // CMP_STRIDE
    for layer in range(depth):
        mod = _ada(c, w_ada[layer], b_ada[layer])
        (q, kc_in, vc_in, ksd, vst, kwd, vwt, gt, qm, km, vmt, ga, gb) = _inproj(
            x, mod, g_mix[layer][None], _inproj_weight(w_in[layer]), g_cq[layer][None],
            _mla_q_weight(w_uq[layer]), g_ckv[layer][None], _mla_kv_weight(w_uk[layer], w_uv[layer]),
            cs, sn, tm_in)
        w1k, pek, w1kf, w2k = _compress_weights(pe_ck[layer], w_ck1[layer], w_ck2[layer])
        w1v, pev, w1vf, w2v = _compress_weights(pe_cv[layer], w_cv1[layer], w_cv2[layer])
        width = CMP_STRIDE * NSA_GROUPS * NSA_DH
        kcd, vct = _compress(kc_in.reshape(B, n_chunks, width), vc_in.reshape(B, n_chunks, width),
                             w1k, w1v, pek, pev, w1kf, w1vf, w2k, w2v)
        ocw, sel, cnt = _nsa_cw(q, kcd, vct, kwd, vwt, gt, ovt, gmat, tq_cw, tq_nsa)
        flags = (cnt[:, :, :, 0, :S // tk_sel] > 0.5).astype(jnp.int32).reshape(-1)
        o_nsa = _nsa_sel(flags, q, ksd, vst, sel, gt, ocw, tq_nsa, tk_sel)
        o_mla = _mla(qm, km, vmt, tq_mla, mla_heads_per_step)
        x = _merge_mlp(o_nsa, o_mla, ga, gb, x, mod, w_o_nsa[layer].astype(BF16), w_o_mla[layer].astype(BF16),
                       w_out[layer].astype(BF16), g_mlp[layer][None], w_fc1[layer].astype(BF16),
                       w_fc2[layer].astype(BF16), g_final[None], tm_out, ff_chunk, layer == depth - 1)
    return x
```

```python
import functools
import math

import jax
import jax.numpy as jnp
from jax import lax
from jax.experimental import pallas as pl
from jax.experimental.pallas import tpu as pltpu

F32 = jnp.float32
BF16 = jnp.bfloat16

D_MODEL = 1024
NSA_GROUPS = 2
NSA_HG = 4
NSA_DH = 64
CMP_LEN = 32
CMP_STRIDE = 16
CMP_HIDDEN = 256
SEL_BLOCK = 64
SEL_TOPK = 16
WINDOW = 512
MLA_HEADS = 8
MLA_NOPE = 64
MLA_ROPE = 32
MLA_V = 64
MLA_Q_RANK = 384
MLA_KV_RANK = 256
ROPE_THETA = 10000.0
D_FF = 4 * D_MODEL
EPS = 1e-6
FORCE_BONUS = 1000.0

LANE = 128
HALF = LANE // 2
NEG = -1e30
LOG2E = math.log2(math.e)
VMEM_LIMIT = 52 * 1024 * 1024
GATE_ROWS = 16
ONES_ROWS = 16
MXU_COLS = 256
OFF_Q = 0
OFF_KC_VC = 512
OFF_KS_KW = 768
OFF_VS_VW = 1024
OFF_G_KR = 1280
OFF_QD = 1536
OFF_KVD = 1920
OFF_GA = 2176
OFF_GB = 3200
IN_COLS = 4224


def _dot(a, b):
    return jnp.dot(a, b, preferred_element_type=F32)


def _dot_nt(a, b):
    return lax.dot_general(a, b, (((1,), (1,)), ((), ())), preferred_element_type=F32)


def _rms(x, g):
    return x * lax.rsqrt(jnp.mean(x * x, axis=-1, keepdims=True) + EPS) * g


def _sigmoid(x):
    return 1.0 / (1.0 + jnp.exp(-x))


def _params(sem):
    return pltpu.CompilerParams(dimension_semantics=sem, vmem_limit_bytes=VMEM_LIMIT)


def _ada_body(c_ref, w_ref, b_ref, o_ref):
    o_ref[...] = jnp.dot(c_ref[...], w_ref[...], preferred_element_type=F32,
                         precision=lax.Precision.HIGHEST) + b_ref[...]


def _ada(c, w, b):
    nb, d = c.shape
    n = w.shape[1]
    rows = 8
    cp = jnp.zeros((rows, d), F32).at[:nb].set(c)
    tn = 1536
    out = pl.pallas_call(
        _ada_body,
        grid=(n // tn,),
        in_specs=[pl.BlockSpec((rows, d), lambda j: (0, 0)),
                  pl.BlockSpec((d, tn), lambda j: (0, j)),
                  pl.BlockSpec((1, tn), lambda j: (0, j))],
        out_specs=pl.BlockSpec((rows, tn), lambda j: (0, j)),
        out_shape=jax.ShapeDtypeStruct((rows, n), F32),
        compiler_params=_params(("arbitrary",)),
        name="ada",
    )(cp, w, b.reshape(1, n))
    return out[:nb].reshape(nb, 6, d)


def _inproj_body(x_ref, mod_ref, gmix_ref, w_ref, gcq_ref, wq2_ref, gckv_ref, wkv_ref,
                 cs_ref, sn_ref,
                 q_ref, kc_ref, vc_ref, ks_ref, vst_ref, kw_ref, vwt_ref, gt_ref,
                 qm_ref, km_ref, vmt_ref, ga_ref, gb_ref):
    x = x_ref[0]
    h = _rms(x, gmix_ref[...]) * (1.0 + mod_ref[0, 1:2, :]) + mod_ref[0, 0:1, :]
    hb = h.astype(BF16)

    def proj(lo, n):
        return _dot(hb, w_ref[:, lo:lo + n])

    lane = lax.broadcasted_iota(jnp.int32, (x.shape[0], LANE), 1)
    half = MLA_ROPE // 2

    def dup_groups(z):
        swapped = pltpu.roll(z, HALF, 1)
        lower = lane < HALF
        return jnp.concatenate([jnp.where(lower, z, swapped), jnp.where(lower, swapped, z)], axis=1)

    def rotary(a):
        partner = jnp.where(lane < MLA_NOPE + half, -pltpu.roll(a, LANE - half, 1), pltpu.roll(a, half, 1))
        return a * cs_ref[...] + partner * sn_ref[...]

    q_ref[0] = (proj(OFF_Q, 512) * (NSA_DH ** -0.5 * LOG2E)).astype(BF16)
    z = proj(OFF_KC_VC, MXU_COLS)
    kc_ref[0] = z[:, 0:LANE].astype(BF16)
    vc_ref[0] = z[:, LANE:].astype(BF16)
    z = proj(OFF_KS_KW, MXU_COLS)
    ks_ref[0] = dup_groups(z[:, 0:LANE]).astype(BF16)
    kw_ref[0] = dup_groups(z[:, LANE:]).astype(BF16)
    z = proj(OFF_VS_VW, MXU_COLS)
    vst_ref[0] = z[:, 0:LANE].T.astype(BF16)
    vwt_ref[0] = z[:, LANE:].T.astype(BF16)
    z = proj(OFF_G_KR, MXU_COLS)
    gt_ref[0] = _sigmoid(z[:, 0:LANE]).T[0:NSA_GROUPS * GATE_ROWS]
    kr = rotary(z[:, LANE:])
    ga_ref[0] = _sigmoid(proj(OFF_GA, 1024)).astype(BF16)
    gb_ref[0] = _sigmoid(proj(OFF_GB, 1024)).astype(BF16)

    nh = MLA_HEADS * LANE
    cq = _rms(proj(OFF_QD, MLA_Q_RANK), gcq_ref[...]).astype(BF16)
    q2 = _dot(cq, wq2_ref[...])
    scale = (MLA_NOPE + MLA_ROPE) ** -0.5 * LOG2E
    for hh in range(MLA_HEADS):
        qm_ref[0, :, LANE * hh:LANE * (hh + 1)] = (rotary(q2[:, LANE * hh:LANE * (hh + 1)]) * scale).astype(BF16)
    ckv = _rms(proj(OFF_KVD, MLA_KV_RANK), gckv_ref[...]).astype(BF16)
    kv = _dot(ckv, wkv_ref[...])
    for hh in range(MLA_HEADS):
        km_ref[0, :, LANE * hh:LANE * (hh + 1)] = (kv[:, LANE * hh:LANE * (hh + 1)] + kr).astype(BF16)
    vmt_ref[0] = kv[:, nh:].T.astype(BF16)


def _inproj(x, mod, g_mix, w_all, g_cq, wq2, g_ckv, wkv, cs, sn, tm):
    B, S, D = x.shape
    row = lambda n: pl.BlockSpec((1, tm, n), lambda b, i: (b, i, 0))
    col = lambda n: pl.BlockSpec((1, n, tm), lambda b, i: (b, 0, i))
    full = lambda a: pl.BlockSpec(a.shape, lambda b, i: (0,) * a.ndim, pipeline_mode=pl.Buffered(1))
    tok = lambda n, dt: (row(n), jax.ShapeDtypeStruct((B, S, n), dt))
    tr = lambda n, dt: (col(n), jax.ShapeDtypeStruct((B, n, S), dt))
    outs = [tok(512, BF16), tok(128, BF16), tok(128, BF16), tok(256, BF16), tr(LANE, BF16),
            tok(256, BF16), tr(LANE, BF16), tr(NSA_GROUPS * GATE_ROWS, F32),
            tok(1024, BF16), tok(1024, BF16), tr(MLA_HEADS * MLA_V, BF16), tok(1024, BF16), tok(1024, BF16)]
    return pl.pallas_call(
        _inproj_body,
        grid=(B, S // tm),
        in_specs=[row(D),
                  pl.BlockSpec((1, 6, D), lambda b, i: (b, 0, 0)),
                  full(g_mix), full(w_all), full(g_cq), full(wq2), full(g_ckv), full(wkv),
                  pl.BlockSpec((tm, LANE), lambda b, i: (i, 0)),
                  pl.BlockSpec((tm, LANE), lambda b, i: (i, 0))],
        out_specs=[o[0] for o in outs],
        out_shape=[o[1] for o in outs],
        compiler_params=_params(("parallel", "parallel")),
        name="inproj",
    )(x, mod, g_mix, w_all, g_cq, wq2, g_ckv, wkv, cs, sn)


def _compress_one(x, w1big, pe, w1f, w2d):
    hcat = _dot(x, w1big)
    const = jnp.dot(pe, w1f, preferred_element_type=F32, precision=lax.Precision.HIGHEST)[0:1]
    n = x.shape[0]
    rowid = lax.broadcasted_iota(jnp.int32, (n, LANE), 0)
    hid_w = CMP_HIDDEN
    outs = []
    for g in range(NSA_GROUPS):
        a = hcat[:, 2 * hid_w * g:2 * hid_w * g + hid_w]
        b = hcat[:, 2 * hid_w * g + hid_w:2 * hid_w * (g + 1)]
        hid = a + pltpu.roll(b, n - 1, 0) + const
        act = hid * _sigmoid(hid)
        out = _dot(act.astype(BF16), w2d)
        outs.append(jnp.where(rowid < n - 1, out, 0.0))
    return outs


def _compress_body(xk_ref, xv_ref, w1k_ref, w1v_ref, pek_ref, pev_ref, w1kf_ref, w1vf_ref,
                   w2k_ref, w2v_ref, kc_ref, vct_ref):
    kc = _compress_one(xk_ref[0], w1k_ref[...], pek_ref[...], w1kf_ref[...], w2k_ref[...])
    vc = _compress_one(xv_ref[0], w1v_ref[...], pev_ref[...], w1vf_ref[...], w2v_ref[...])
    for g in range(NSA_GROUPS):
        kc_ref[0, g] = kc[g].astype(BF16)
        vct_ref[0, g] = vc[g].T[0:NSA_DH].astype(BF16)


def _compress(xk, xv, w1k, w1v, pek, pev, w1kf, w1vf, w2k, w2v):
    B, n, width = xk.shape
    full = lambda a: pl.BlockSpec(a.shape, lambda b: (0,) * a.ndim)
    xs = pl.BlockSpec((1, n, width), lambda b: (b, 0, 0))
    return pl.pallas_call(
        _compress_body,
        grid=(B,),
        in_specs=[xs, xs, full(w1k), full(w1v), full(pek), full(pev), full(w1kf), full(w1vf),
                  full(w2k), full(w2v)],
        out_specs=[pl.BlockSpec((1, NSA_GROUPS, n, LANE), lambda b: (b, 0, 0, 0)),
                   pl.BlockSpec((1, NSA_GROUPS, NSA_DH, n), lambda b: (b, 0, 0, 0))],
        out_shape=[jax.ShapeDtypeStruct((B, NSA_GROUPS, n, LANE), BF16),
                   jax.ShapeDtypeStruct((B, NSA_GROUPS, NSA_DH, n), BF16)],
        compiler_params=_params(("parallel",)),
        name="compress",
    )(xk, xv, w1k, w1v, pek, pev, w1kf, w1vf, w2k, w2v)


def _stack_heads(q_ref, tq):
    upper = lax.broadcasted_iota(jnp.int32, (tq, LANE), 1) >= HALF
    parts = []
    for hh in range(NSA_HG):
        slab = q_ref[0, :, LANE * (hh // 2):LANE * (hh // 2 + 1)]
        keep = upper if hh % 2 else jnp.logical_not(upper)
        parts.append(jnp.where(keep, slab, jnp.zeros_like(slab)))
    return jnp.concatenate(parts, axis=0)


def _slope(g, hh):
    return jnp.where(g == 0, 1.0, 2.0 ** -NSA_HG).astype(F32) * (2.0 ** -(hh + 1) * LOG2E)


def _with_ones(vt):
    return jnp.concatenate([vt, jnp.ones((ONES_ROWS, vt.shape[1]), vt.dtype)], axis=0)


def _nsa_cw_body(q_ref, kc_ref, vct_ref, kw_ref, vwt_ref, gt_ref, ovt_ref, gm_ref,
                 ocw_ref, sel_ref, cnt_ref, bc_sc, bw_sc, qst_sc, *, tq, tw, tf, seq, n_parts):
    g = pl.program_id(1)
    i = pl.program_id(2)
    t0 = i * tq
    qst_sc[...] = _stack_heads(q_ref, tq)
    gates = gt_ref[0]
    ncp = kc_ref.shape[2]
    nsp = ovt_ref.shape[0]
    wk = WINDOW + tw

    @pl.when(i == 0)
    def _():
        c_end_f = (lax.broadcasted_iota(jnp.int32, (ncp, tq), 0) * CMP_STRIDE + (CMP_LEN - 1)).astype(F32)
        key_f = lax.broadcasted_iota(jnp.int32, (wk, tw), 0).astype(F32)
        for hh in range(NSA_HG):
            bc_sc[hh] = _slope(g, hh) * c_end_f
            bw_sc[hh] = _slope(g, hh) * key_f

    def tile_body(ncp_e, nsp_e):
        vct = _with_ones(vct_ref[0, 0, :, 0:ncp_e])
        st = _dot_nt(kc_ref[0, 0, 0:ncp_e, :], qst_sc[...])
        t_c = t0 + lax.broadcasted_iota(jnp.int32, (ncp_e, tq), 1)
        c_end = lax.broadcasted_iota(jnp.int32, (ncp_e, tq), 0) * CMP_STRIDE + (CMP_LEN - 1)
        madd_c = jnp.where(t_c >= c_end, 0.0, NEG)
        psum = jnp.zeros((ncp_e, tq), F32)
        o_cmp = []
        for hh in range(NSA_HG):
            z = st[:, hh * tq:(hh + 1) * tq] + bc_sc[hh, 0:ncp_e] + madd_c
            m = jnp.max(z, axis=0, keepdims=True)
            e = jnp.exp2(z - m)
            o_aug = _dot(vct, e.astype(BF16))
            r = jnp.where(m > 0.5 * NEG, 1.0 / jnp.maximum(o_aug[NSA_DH:NSA_DH + 1], 1e-30), 0.0)
            psum = psum + e * r
            o_cmp.append(o_aug[0:NSA_DH] * r)

        ovt = ovt_ref[0:nsp_e, 0:ncp_e]
        p1 = psum.astype(BF16)
        r1 = psum - p1.astype(F32)
        p2 = r1.astype(BF16)
        p3 = (r1 - p2.astype(F32)).astype(BF16)
        p_slc = _dot(ovt, p1) + _dot(ovt, p2) + _dot(ovt, p3)

        n_forced = 3
        sels = []
        for c in range(tq // LANE):
            blk = lax.broadcasted_iota(jnp.int32, (nsp_e, LANE), 0)
            cur = (t0 + c * LANE + lax.broadcasted_iota(jnp.int32, (nsp_e, LANE), 1)) // SEL_BLOCK
            valid = blk <= cur
            forced = (blk == 0) | (blk >= cur - 1)
            score = jnp.where(valid & jnp.logical_not(forced), p_slc[:, c * LANE:(c + 1) * LANE], -1.0)
            for _ in range(max(min(SEL_TOPK, seq // SEL_BLOCK) - n_forced, 0)):
                m = jnp.max(score, axis=0, keepdims=True)
                first = jnp.min(jnp.where(score == m, blk, nsp), axis=0, keepdims=True)
                score = jnp.where(blk == first, -1.0, score)
            sels.append(jnp.where(valid & (score < 0.0), 1.0, 0.0))
        sel = jnp.concatenate(sels, axis=1)
        sel_ref[0, 0, 0:nsp_e] = sel
        if nsp_e < nsp:
            sel_ref[0, 0, nsp_e:nsp] = jnp.zeros((nsp - nsp_e, tq), F32)
        for f in range(tq // tf):
            per_blk = _dot_nt(jnp.ones((8, tf), BF16), sel[:, f * tf:(f + 1) * tf].astype(BF16))
            cnt_ref[0, 0, f] = _dot(per_blk.astype(BF16), gm_ref[0:nsp_e])

        for sub in range(tq // tw):
            k_start = pl.multiple_of(jnp.clip(t0 + sub * tw - WINDOW, 0, seq - wk), LANE)
            kw = kw_ref[0, pl.ds(k_start, wk), :]
            vwt = _with_ones(vwt_ref[0, :, pl.ds(k_start, wk)])
            qsub = jnp.concatenate(
                [qst_sc[hh * tq + sub * tw:hh * tq + (sub + 1) * tw] for hh in range(NSA_HG)], axis=0)
            st = _dot_nt(kw, qsub)
            t_w = t0 + sub * tw + lax.broadcasted_iota(jnp.int32, (wk, tw), 1)
            dist_i = t_w - (k_start + lax.broadcasted_iota(jnp.int32, (wk, tw), 0))
            madd_w = jnp.where((dist_i >= 0) & (dist_i < WINDOW), 0.0, NEG)
            cols = slice(sub * tw, (sub + 1) * tw)
            for hh in range(NSA_HG):
                z = st[:, hh * tw:(hh + 1) * tw] + bw_sc[hh] + madd_w
                e = jnp.exp2(z - jnp.max(z, axis=0, keepdims=True))
                o_aug = _dot(vwt, e.astype(BF16))
                o_win = o_aug[0:NSA_DH] * (1.0 / jnp.maximum(o_aug[NSA_DH:NSA_DH + 1], 1e-30))
                ocw_ref[0, 0, NSA_DH * hh:NSA_DH * (hh + 1), cols] = (
                    gates[hh:hh + 1, cols] * o_cmp[hh][:, cols]
                    + gates[2 * NSA_HG + hh:2 * NSA_HG + hh + 1, cols] * o_win)

    part = (i * n_parts) // pl.num_programs(2)
    for v in range(n_parts):
        pl.when(part == v)(functools.partial(tile_body, ncp * (v + 1) // n_parts, nsp * (v + 1) // n_parts))


def _cw_parts(nq, ncp, nsp):
    for n in (4, 2):
        if nq % n == 0 and (ncp // n) % LANE == 0 and (nsp // n) % 8 == 0:
            return n
    return 1


def _nsa_cw(q, kcd, vct, kwd, vwt, gt, ovt, gmat, tq, tf):
    B, S, _ = q.shape
    ncp = kcd.shape[2]
    nsp = ovt.shape[0]
    nq = S // tq
    tw = min(MXU_COLS, tq)
    return pl.pallas_call(
        functools.partial(_nsa_cw_body, tq=tq, tw=tw, tf=tf, seq=S, n_parts=_cw_parts(nq, ncp, nsp)),
        grid=(B, NSA_GROUPS, nq),
        in_specs=[pl.BlockSpec((1, tq, 2 * LANE), lambda b, g, i: (b, i, g)),
                  pl.BlockSpec((1, 1, ncp, LANE), lambda b, g, i: (b, g, 0, 0)),
                  pl.BlockSpec((1, 1, NSA_DH, ncp), lambda b, g, i: (b, g, 0, 0)),
                  pl.BlockSpec((1, S, LANE), lambda b, g, i: (b, 0, g)),
                  pl.BlockSpec((1, NSA_DH, S), lambda b, g, i: (b, g, 0)),
                  pl.BlockSpec((1, GATE_ROWS, tq), lambda b, g, i: (b, g, i)),
                  pl.BlockSpec(ovt.shape, lambda b, g, i: (0, 0)),
                  pl.BlockSpec(gmat.shape, lambda b, g, i: (0, 0))],
        out_specs=[pl.BlockSpec((1, 1, NSA_HG * NSA_DH, tq), lambda b, g, i: (b, g, 0, i)),
                   pl.BlockSpec((1, 1, nsp, tq), lambda b, g, i: (b, g, 0, i)),
                   pl.BlockSpec((1, 1, tq // tf, 8, LANE), lambda b, g, i: (b, g, i, 0, 0))],
        out_shape=[jax.ShapeDtypeStruct((B, NSA_GROUPS, NSA_HG * NSA_DH, S), F32),
                   jax.ShapeDtypeStruct((B, NSA_GROUPS, nsp, S), F32),
                   jax.ShapeDtypeStruct((B, NSA_GROUPS, S // tf, 8, LANE), F32)],
        scratch_shapes=[pltpu.VMEM((NSA_HG, ncp, tq), F32), pltpu.VMEM((NSA_HG, WINDOW + tw, tw), F32),
                        pltpu.VMEM((NSA_HG * tq, LANE), BF16)],
        compiler_params=_params(("arbitrary", "arbitrary", "arbitrary")),
        name="nsa_cw",
    )(q, kcd, vct, kwd, vwt, gt, ovt, gmat)


def _nsa_sel_body(tiles_ref, cnt_ref, q_ref, ks_ref, vst_ref, sel_ref, gt_ref, ocw_ref, o_ref,
                  m_sc, acc_sc, skb_sc, qst_sc, sa_sc, sb_sc, *, tq, tk, n_tiles):
    g = pl.program_id(1)
    i = pl.program_id(2)
    t0 = i * tq
    j_diag = t0 // tk

    @pl.when(i == 0)
    def _():
        key_f = lax.broadcasted_iota(jnp.int32, (tk, tq), 0).astype(F32)
        for hh in range(NSA_HG):
            skb_sc[hh] = _slope(g, hh) * key_f

    qst_sc[...] = _stack_heads(q_ref, tq)
    m_sc[...] = jnp.full(m_sc.shape, NEG, F32)
    acc_sc[...] = jnp.zeros(acc_sc.shape, F32)

    step = (pl.program_id(0) * NSA_GROUPS + g) * pl.num_programs(2) + i
    n_act = cnt_ref[step]

    def tile_at(it):
        return jnp.where(it < n_act, tiles_ref[step * n_tiles + jnp.minimum(it, n_tiles - 1)], j_diag)

    def scores(j, s_ref):
        k0 = pl.multiple_of(j * tk, tk)
        s_ref[...] = _dot_nt(ks_ref[0, pl.ds(k0, tk), :], qst_sc[...])

    def consume(j, s_ref, diagonal):
        k0 = pl.multiple_of(j * tk, tk)
        vt = _with_ones(vst_ref[0, :, pl.ds(k0, tk)])
        b0 = j * (tk // SEL_BLOCK)
        pieces = []
        for r in range(tk // SEL_BLOCK):
            chosen = sel_ref[0, 0, pl.ds(b0 + r, 1), :]
            pieces.append(jnp.broadcast_to(jnp.where(chosen > 0.5, 0.0, NEG), (SEL_BLOCK, tq)))
        madd = jnp.concatenate(pieces, axis=0)
        if diagonal:
            key_i = lax.broadcasted_iota(jnp.int32, (tk, tq), 0)
            qry_i = lax.broadcasted_iota(jnp.int32, (tk, tq), 1)
            madd = jnp.where(key_i + k0 <= qry_i + t0, madd, NEG)
        for hh in range(NSA_HG):
            c = _slope(g, hh) * (k0 - t0).astype(F32)
            z = s_ref[:, hh * tq:(hh + 1) * tq] + skb_sc[hh] + madd
            m_prev = m_sc[hh]
            m_new = jnp.maximum(m_prev, jnp.max(z, axis=0, keepdims=True) + c)
            alpha = jnp.exp2(m_prev - m_new)
            p = jnp.exp2(z - (m_new - c))
            acc_sc[hh] = alpha * acc_sc[hh] + _dot(vt, p.astype(BF16))
            m_sc[hh] = m_new

    scores(tile_at(0), sa_sc)

    def pair(jj, carry):
        it = 2 * jj
        scores(tile_at(it + 1), sb_sc)
        consume(tile_at(it), sa_sc, False)
        scores(tile_at(it + 2), sa_sc)
        consume(tile_at(it + 1), sb_sc, False)
        return carry

    lax.fori_loop(0, n_act // 2, pair, 0)

    @pl.when(n_act % 2 == 0)
    def _():
        consume(j_diag, sa_sc, True)

    @pl.when(n_act % 2 == 1)
    def _():
        scores(j_diag, sb_sc)
        consume(tile_at(n_act - 1), sa_sc, False)
        consume(j_diag, sb_sc, True)

    gates = gt_ref[0]
    for pair_i in range(NSA_HG // 2):
        halves = []
        for hh in (2 * pair_i, 2 * pair_i + 1):
            acc = acc_sc[hh]
            w = gates[NSA_HG + hh:NSA_HG + hh + 1] / jnp.maximum(acc[NSA_DH:NSA_DH + 1], 1e-30)
            halves.append(acc[0:NSA_DH] * w + ocw_ref[0, 0, NSA_DH * hh:NSA_DH * (hh + 1), :])
        o_ref[0, :, LANE * pair_i:LANE * (pair_i + 1)] = jnp.concatenate(halves, axis=0).T.astype(BF16)


def _nsa_sel(tiles, counts, q, ksd, vst, sel, gt, ocw, tq, tk):
    B, S, _ = q.shape
    nsp = sel.shape[2]
    n_tiles = S // tk
    assert tk % tq == 0, "a query tile must lie inside one (the diagonal) key tile"
    qs = pl.BlockSpec((1, tq, 2 * LANE), lambda b, g, i, t, c: (b, i, g))
    return pl.pallas_call(
        functools.partial(_nsa_sel_body, tq=tq, tk=tk, n_tiles=n_tiles),
        grid_spec=pltpu.PrefetchScalarGridSpec(
            num_scalar_prefetch=2,
            grid=(B, NSA_GROUPS, S // tq),
            in_specs=[qs,
                      pl.BlockSpec((1, S, LANE), lambda b, g, i, t, c: (b, 0, g)),
                      pl.BlockSpec((1, NSA_DH, S), lambda b, g, i, t, c: (b, g, 0)),
                      pl.BlockSpec((1, 1, nsp, tq), lambda b, g, i, t, c: (b, g, 0, i)),
                      pl.BlockSpec((1, GATE_ROWS, tq), lambda b, g, i, t, c: (b, g, i)),
                      pl.BlockSpec((1, 1, NSA_HG * NSA_DH, tq), lambda b, g, i, t, c: (b, g, 0, i))],
            out_specs=qs,
            scratch_shapes=[pltpu.VMEM((NSA_HG, 1, tq), F32),
                            pltpu.VMEM((NSA_HG, NSA_DH + ONES_ROWS, tq), F32),
                            pltpu.VMEM((NSA_HG, tk, tq), F32),
                            pltpu.VMEM((NSA_HG * tq, LANE), BF16),
                            pltpu.VMEM((tk, NSA_HG * tq), F32),
                            pltpu.VMEM((tk, NSA_HG * tq), F32)]),
        out_shape=jax.ShapeDtypeStruct((B, S, 4 * LANE), BF16),
        compiler_params=_params(("arbitrary", "arbitrary", "arbitrary")),
        name="nsa_sel",
    )(tiles, counts, q, ksd, vst, sel, gt, ocw)


def _mla_body(q_ref, k_ref, vt_ref, o_ref, m_sc, acc_sc, sa_sc, sb_sc, *, tq, tc, nh):
    n_full = pl.program_id(2)
    m_sc[...] = jnp.full(m_sc.shape, NEG, F32)
    acc_sc[...] = jnp.zeros(acc_sc.shape, F32)

    def scores(j, s_ref):
        k0 = pl.multiple_of(j * tq, tq)
        for hh in range(nh):
            lanes = slice(LANE * hh, LANE * (hh + 1))
            s_ref[hh] = _dot_nt(k_ref[0, pl.ds(k0, tq), lanes], q_ref[0, :, lanes])

    def consume(j, s_ref, diagonal):
        k0 = pl.multiple_of(j * tq, tq)
        for hh in range(nh):
            vt = _with_ones(vt_ref[0, MLA_V * hh:MLA_V * (hh + 1), pl.ds(k0, tq)])
            for c in range(tq // tc):
                cols = slice(c * tc, (c + 1) * tc)
                st = s_ref[hh, :, cols]
                if diagonal:
                    causal = (lax.broadcasted_iota(jnp.int32, (tq, tc), 0)
                              <= lax.broadcasted_iota(jnp.int32, (tq, tc), 1) + c * tc)
                    st = jnp.where(causal, st, NEG)
                m_prev = m_sc[hh, :, cols]
                m_new = jnp.maximum(m_prev, jnp.max(st, axis=0, keepdims=True))
                alpha = jnp.exp2(m_prev - m_new)
                p = jnp.exp2(st - m_new)
                acc_sc[hh, :, cols] = alpha * acc_sc[hh, :, cols] + _dot(vt, p.astype(BF16))
                m_sc[hh, :, cols] = m_new

    scores(0, sa_sc)

    def pair(jj, carry):
        j = 2 * jj
        scores(j + 1, sb_sc)
        consume(j, sa_sc, False)
        scores(j + 2, sa_sc)
        consume(j + 1, sb_sc, False)
        return carry

    lax.fori_loop(0, n_full // 2, pair, 0)

    @pl.when(n_full % 2 == 0)
    def _():
        consume(n_full, sa_sc, True)

    @pl.when(n_full % 2 == 1)
    def _():
        scores(n_full, sb_sc)
        consume(n_full - 1, sa_sc, False)
        consume(n_full, sb_sc, True)

    for pair_i in range(nh // 2):
        halves = []
        for hh in (2 * pair_i, 2 * pair_i + 1):
            acc = acc_sc[hh]
            halves.append(acc[0:MLA_V] * (1.0 / jnp.maximum(acc[MLA_V:MLA_V + 1], 1e-30)))
        o_ref[0, :, LANE * pair_i:LANE * (pair_i + 1)] = jnp.concatenate(halves, axis=0).T.astype(BF16)


def _mla(qm, km, vmt, tq, nh):
    B, S, _ = qm.shape
    return pl.pallas_call(
        functools.partial(_mla_body, tq=tq, tc=min(256, tq), nh=nh),
        grid=(B, MLA_HEADS // nh, S // tq),
        in_specs=[pl.BlockSpec((1, tq, nh * LANE), lambda b, p, i: (b, i, p)),
                  pl.BlockSpec((1, S, nh * LANE), lambda b, p, i: (b, 0, p), pipeline_mode=pl.Buffered(1)),
                  pl.BlockSpec((1, nh * MLA_V, S), lambda b, p, i: (b, p, 0), pipeline_mode=pl.Buffered(1))],
        out_specs=pl.BlockSpec((1, tq, nh * MLA_V), lambda b, p, i: (b, i, p)),
        out_shape=jax.ShapeDtypeStruct((B, S, MLA_HEADS * MLA_V), BF16),
        scratch_shapes=[pltpu.VMEM((nh, 1, tq), F32), pltpu.VMEM((nh, MLA_V + ONES_ROWS, tq), F32),
                        pltpu.VMEM((nh, tq, tq), F32), pltpu.VMEM((nh, tq, tq), F32)],
        compiler_params=_params(("parallel", "parallel", "arbitrary")),
        name="mla",
    )(qm, km, vmt)


def _merge_mlp_body(on_ref, om_ref, ga_ref, gb_ref, x_ref, mod_ref, won_ref, wom_ref, wout_ref,
                    gmlp_ref, w1_ref, w2_ref, gfin_ref, o_ref, *, fc, final):
    y = (ga_ref[0].astype(F32) * _dot(on_ref[0], won_ref[...])
         + gb_ref[0].astype(F32) * _dot(om_ref[0], wom_ref[...]))
    x1 = x_ref[0] + mod_ref[0, 2:3, :] * _dot(y.astype(BF16), wout_ref[...])
    h = _rms(x1, gmlp_ref[...]) * (1.0 + mod_ref[0, 4:5, :]) + mod_ref[0, 3:4, :]
    hb = h.astype(BF16)
    acc = jnp.zeros(x1.shape, F32)
    for c in range(w1_ref.shape[1] // fc):
        a = jnp.maximum(_dot(hb, w1_ref[:, c * fc:(c + 1) * fc]), 0.0)
        acc = acc + _dot((a * a).astype(BF16), w2_ref[c * fc:(c + 1) * fc, :])
    x2 = x1 + mod_ref[0, 5:6, :] * acc
    o_ref[0] = _rms(x2, gfin_ref[...]) if final else x2


def _merge_mlp(o_nsa, o_mla, ga, gb, x, mod, w_on, w_om, w_out, g_mlp, w1, w2, g_fin, tm, fc, final):
    B, S, D = x.shape
    row = lambda n: pl.BlockSpec((1, tm, n), lambda b, i: (b, i, 0))
    full = lambda a: pl.BlockSpec(a.shape, lambda b, i: (0,) * a.ndim, pipeline_mode=pl.Buffered(1))
    return pl.pallas_call(
        functools.partial(_merge_mlp_body, fc=fc, final=final),
        grid=(B, S // tm),
        in_specs=[row(o_nsa.shape[2]), row(o_mla.shape[2]), row(D), row(D), row(D),
                  pl.BlockSpec((1, 6, D), lambda b, i: (b, 0, 0)), full(w_on), full(w_om), full(w_out),
                  full(g_mlp), full(w1), full(w2), full(g_fin)],
        out_specs=row(D),
        out_shape=jax.ShapeDtypeStruct((B, S, D), F32),
        compiler_params=_params(("parallel", "parallel")),
        name="merge_mlp",
    )(o_nsa, o_mla, ga, gb, x, mod, w_on, w_om, w_out, g_mlp, w1, w2, g_fin)


def _inproj_weight(w_in):
    d = w_in.shape[0]
    o = 0
    parts = {}
    for name, n in (("q", 512), ("kc", 128), ("vc", 128), ("ks", 128), ("vs", 128), ("kw", 128),
                    ("vw", 128), ("g", 24), ("qd", MLA_Q_RANK), ("kvd", MLA_KV_RANK), ("kr", MLA_ROPE),
                    ("ga", D_MODEL), ("gb", D_MODEL)):
        parts[name] = w_in[:, o:o + n]
        o += n
    wg = parts["g"].reshape(d, 3, NSA_GROUPS, NSA_HG).transpose(0, 2, 1, 3).reshape(d, NSA_GROUPS, 3 * NSA_HG)
    wg = jnp.pad(wg, ((0, 0), (0, 0), (0, GATE_ROWS - 3 * NSA_HG))).reshape(d, NSA_GROUPS * GATE_ROWS)
    wg = jnp.pad(wg, ((0, 0), (0, LANE - NSA_GROUPS * GATE_ROWS)))
    kr = jnp.pad(parts["kr"], ((0, 0), (MLA_NOPE, LANE - MLA_NOPE - MLA_ROPE)))
    w_all = jnp.concatenate(
        [p.astype(BF16) for p in
         (parts["q"], parts["kc"], parts["vc"], parts["ks"], parts["kw"], parts["vs"], parts["vw"], wg, kr,
          parts["qd"], parts["kvd"], parts["ga"], parts["gb"])], axis=1)
    assert w_all.shape[1] == IN_COLS
    return w_all


def _mla_q_weight(w_uq):
    r = w_uq.shape[0]
    w = w_uq.reshape(r, MLA_HEADS, MLA_NOPE + MLA_ROPE)
    w = jnp.pad(w, ((0, 0), (0, 0), (0, LANE - MLA_NOPE - MLA_ROPE)))
    return w.reshape(r, MLA_HEADS * LANE).astype(BF16)


def _mla_kv_weight(w_uk, w_uv):
    r = w_uk.shape[0]
    k = w_uk.reshape(r, MLA_HEADS, MLA_NOPE)
    k = jnp.pad(k, ((0, 0), (0, 0), (0, LANE - MLA_NOPE))).reshape(r, MLA_HEADS * LANE)
    return jnp.concatenate([k, w_uv], axis=1).astype(BF16)


def _rope_tables(seq):
    half = MLA_ROPE // 2
    pos = jnp.arange(seq, dtype=F32)
    inv_freq = ROPE_THETA ** (-jnp.arange(half, dtype=F32) / half)
    ang = pos[:, None] * inv_freq[None, :]
    cos, sin = jnp.cos(ang), jnp.sin(ang)
    ones = jnp.ones((seq, MLA_NOPE), F32)
    z_lo = jnp.zeros((seq, MLA_NOPE), F32)
    z_hi = jnp.zeros((seq, LANE - MLA_NOPE - MLA_ROPE), F32)
    return (jnp.concatenate([ones, cos, cos, z_hi], axis=1),
            jnp.concatenate([z_lo, sin, sin, z_hi], axis=1))


def _compress_weights(pe, w1, w2):
    hid = w1.shape[1]
    wa = w1[:CMP_STRIDE * NSA_DH].reshape(CMP_STRIDE, NSA_DH, hid)
    wb = w1[CMP_STRIDE * NSA_DH:].reshape(CMP_STRIDE, NSA_DH, hid)
    z = jnp.zeros_like(wa)
    cols = []
    for g in range(NSA_GROUPS):
        for w in (wa, wb):
            slots = [z, z]
            slots[g] = w
            cols.append(jnp.stack(slots, axis=1).reshape(CMP_STRIDE * NSA_GROUPS * NSA_DH, hid))
    w1big = jnp.concatenate(cols, axis=1).astype(BF16)
    pe8 = jnp.zeros((8, CMP_LEN * NSA_DH), F32).at[0].set(pe.reshape(-1))
    w2d = jnp.concatenate([w2, w2], axis=1).astype(BF16)
    return w1big, pe8, w1, w2d


def _overlap_table(seq):
    ns = seq // SEL_BLOCK
    nsp = -(-ns // LANE) * LANE
    ncp = seq // CMP_STRIDE
    s_id = jnp.arange(nsp)[:, None]
    c_id = jnp.arange(ncp)[None, :]
    c_start = c_id * CMP_STRIDE
    return ((c_start < s_id * SEL_BLOCK + SEL_BLOCK) & (c_start + CMP_LEN - 1 >= s_id * SEL_BLOCK)
            & (c_id < ncp - 1)).astype(BF16)


def _tiles(seq):
    tm_in = min(1024, seq)
    tm_out = min(512, seq)
    tq_cw = min(1024, seq)
    tq_sel = min(256, seq)
    tk_sel = min(256, seq)
    tq_mla = min(512, seq)
    mla_heads_per_step = 8
    ff_chunk = 1024
    return tm_in, tm_out, tq_cw, tq_sel, tk_sel, tq_mla, mla_heads_per_step, ff_chunk


def kernel(x, c, w_ada, b_ada, g_mix, w_in, pe_ck, w_ck1, w_ck2, pe_cv, w_cv1, w_cv2, g_cq, w_uq, g_ckv,
           w_uk, w_uv, w_o_nsa, w_o_mla, w_out, g_mlp, w_fc1, w_fc2, g_final):
    B, S, D = x.shape
    depth = w_ada.shape[0]
    tm_in, tm_out, tq_cw, tq_nsa, tk_sel, tq_mla, mla_heads_per_step, ff_chunk = _tiles(S)
    cs, sn = _rope_tables(S)
    ovt = _overlap_table(S)
    gmat = (jnp.arange(ovt.shape[0])[:, None] // (tk_sel // SEL_BLOCK) == jnp.arange(LANE)[None, :]).astype(BF16)
    n_chunks = S // CMP_STRIDE
    for layer in range(depth):
        mod = _ada(c, w_ada[layer], b_ada[layer])
        (q, kc_in, vc_in, ksd, vst, kwd, vwt, gt, qm, km, vmt, ga, gb) = _inproj(
            x, mod, g_mix[layer][None], _inproj_weight(w_in[layer]), g_cq[layer][None],
            _mla_q_weight(w_uq[layer]), g_ckv[layer][None], _mla_kv_weight(w_uk[layer], w_uv[layer]),
            cs, sn, tm_in)
        w1k, pek, w1kf, w2k = _compress_weights(pe_ck[layer], w_ck1[layer], w_ck2[layer])
        w1v, pev, w1vf, w2v = _compress_weights(pe_cv[layer], w_cv1[layer], w_cv2[layer])
        width = CMP_STRIDE * NSA_GROUPS * NSA_DH
        kcd, vct = _compress(kc_in.reshape(B, n_chunks, width), vc_in.reshape(B, n_chunks, width),
                             w1k, w1v, pek, pev, w1kf, w1vf, w2k, w2v)
        ocw, sel, cnt = _nsa_cw(q, kcd, vct, kwd, vwt, gt, ovt, gmat, tq_cw, tq_nsa)
        n_tiles = S // tk_sel
        j_diag = (jnp.arange(S // tq_nsa) * tq_nsa) // tk_sel
        active = (cnt[:, :, :, 0, :n_tiles] > 0.5) & (jnp.arange(n_tiles)[None, :] < j_diag[:, None])
        tiles = jnp.argsort(jnp.logical_not(active), axis=-1, stable=True).astype(jnp.int32).reshape(-1)
        counts = jnp.sum(active, axis=-1).astype(jnp.int32).reshape(-1)
        o_nsa = _nsa_sel(tiles, counts, q, ksd, vst, sel, gt, ocw, tq_nsa, tk_sel)
        o_mla = _mla(qm, km, vmt, tq_mla, mla_heads_per_step)
        x = _merge_mlp(o_nsa, o_mla, ga, gb, x, mod, w_o_nsa[layer].astype(BF16), w_o_mla[layer].astype(BF16),
                       w_out[layer].astype(BF16), g_mlp[layer][None], w_fc1[layer].astype(BF16),
                       w_fc2[layer].astype(BF16), g_final[None], tm_out, ff_chunk, layer == depth - 1)
    return x
```
